```python
import math
import jax, jax.numpy as jnp
from jax import lax
import numpy as np

D_MODEL = 1024
BATCH = 8
SEQ = 4096
DEPTH = 1
DEC_BATCH = 128
DEC_SEQ = 8
PAST_LEN = 8192
PAGE_SIZE = 128

F32 = jnp.float32
HD = 64
WIDTH_A = D_MODEL // 2
WIDTH_B = D_MODEL - WIDTH_A
HA = WIDTH_A // HD
KVA = 2
H_IDX = 8
D_IDX = 64
TOPK_MAX = 256
VB = 2 * HD
HB = WIDTH_B // VB
KVB = 2
N_EXPERTS = 32
TOP_K = 4
D_FF = D_MODEL
SWIGLU_LIMIT = 7.0
SWIGLU_ALPHA = 1.702
ROPE_THETA = 10000.0
EPS = 1e-6
Q_BLOCK = 128
SPLITS = (HA * HD, KVA * HD, KVA * HD, H_IDX * D_IDX, D_IDX, H_IDX, HB * 2 * HD, KVB * 2 * HD, KVB * VB)
N_IN = sum(SPLITS)

kernel_name = "hymba_dsa_diffattn_moe_step"


def rms_norm(x, g):
    xf = x.astype(F32)
    y = xf * lax.rsqrt(jnp.mean(xf * xf, axis=-1, keepdims=True) + EPS)
    return (y * g.astype(F32)).astype(x.dtype)


def rope(x, pos):
    half = x.shape[-1] // 2
    inv = ROPE_THETA ** (-jnp.arange(half, dtype=F32) / half)
    ang = pos.astype(F32)[:, None] * inv[None, :]
    shp = (pos.shape[0],) + (1,) * (x.ndim - 3) + (half,)
    cos, sin = jnp.cos(ang).reshape(shp), jnp.sin(ang).reshape(shp)
    xf = x.astype(F32)
    x1, x2 = xf[..., :half], xf[..., half:]
    return jnp.concatenate([x1 * cos - x2 * sin, x2 * cos + x1 * sin], axis=-1).astype(x.dtype)


def adaln(c, w_ada, b_ada):
    m = jnp.einsum('bd,de->be', jax.nn.silu(c), w_ada) + b_ada
    return jnp.split(m, 6, axis=-1)


def modulate(xn, shift, scale):
    return xn * (1.0 + scale[:, None, :]) + shift[:, None, :]


def project(xn, pos, lw):
    b, t = xn.shape[:2]
    z = jnp.einsum('btd,dn->btn', xn, lw['w_in'])
    offs = [int(o) for o in np.cumsum(SPLITS)[:-1]]
    qa, ka, va, qi, ki, wi, qb, kb, vb = jnp.split(z, offs, axis=-1)
    qa = rope(rms_norm(qa.reshape(b, t, HA, HD), lw['g_qa']), pos)
    ka = rope(rms_norm(ka.reshape(b, t, KVA, HD), lw['g_ka']), pos)
    va = va.reshape(b, t, KVA, HD)
    qi = rope(qi.reshape(b, t, H_IDX, D_IDX), pos)
    ki = rope(rms_norm(ki, lw['g_ki']), pos)
    wi = wi * (H_IDX ** -0.5 * D_IDX ** -0.5)
    qb = rope(rms_norm(qb.reshape(b, t, HB, 2, HD), lw['g_qb']), pos)
    kb = rope(rms_norm(kb.reshape(b, t, KVB, 2, HD), lw['g_kb']), pos)
    vb = vb.reshape(b, t, KVB, VB)
    return qa, ka, va, qi, ki, wi, qb, kb, vb


def indexer_scores(qi, wi, ki):
    s = jnp.einsum('bthd,bsd->bths', qi.astype(F32), ki.astype(F32))
    return jnp.einsum('bths,bth->bts', jax.nn.relu(s), wi.astype(F32))


def sparse_attend(qa, k_sel, v_sel, valid):
    b, t = qa.shape[:2]
    q = qa.reshape(b, t, KVA, HA // KVA, HD)
    s = jnp.einsum('btkgd,btjkd->btkgj', q, k_sel).astype(F32) * HD ** -0.5
    s = jnp.where(valid[:, :, None, None, :], s, -jnp.inf)
    p = jax.nn.softmax(s, axis=-1).astype(v_sel.dtype)
    return jnp.einsum('btkgj,btjkd->btkgd', p, v_sel).reshape(b, t, HA * HD)


def take_rows(x, idx):
    return jax.vmap(lambda xx, ii: xx[ii])(x, idx)


def dsa_prompt(qa, ka, va, qi, ki, wi, k_sel):
    b, t = qa.shape[:2]
    nb = t // Q_BLOCK
    blk = lambda a: a.reshape((b, nb, Q_BLOCK) + a.shape[2:]).swapaxes(0, 1)
    kpos = jnp.arange(t)

    def block(args):
        i, q, qq, ww = args
        qpos = i * Q_BLOCK + jnp.arange(Q_BLOCK)
        sc = indexer_scores(qq, ww, ki)
        sc = jnp.where(kpos[None, :] <= qpos[:, None], sc, -jnp.inf)
        _, idx = lax.top_k(sc, k_sel)
        valid = idx <= qpos[None, :, None]
        return sparse_attend(q, take_rows(ka, idx), take_rows(va, idx), valid)

    o = lax.map(block, (jnp.arange(nb), blk(qa), blk(qi), blk(wi)))
    return o.swapaxes(0, 1).reshape(b, t, WIDTH_A)


def dsa_sample(qa, ka, va, qi, ki, wi, ki_past, cak, cav, page_table, k_sel):
    t = qa.shape[1]
    past = ki_past.shape[1]
    page = cak.shape[1]
    causal = jnp.tril(jnp.ones((t, t), dtype=bool))
    sc = jnp.concatenate([indexer_scores(qi, wi, ki_past),
                          jnp.where(causal, indexer_scores(qi, wi, ki), -jnp.inf)], axis=-1)
    _, idx = lax.top_k(sc, k_sel)
    valid = idx <= (past + jnp.arange(t))[None, :, None]
    is_past = (idx < past)[..., None, None]
    pidx = jnp.minimum(idx, past - 1)
    phys = jax.vmap(lambda pt, ii: pt[ii])(page_table, pidx // page)
    off = pidx % page
    nidx = jnp.clip(idx - past, 0, t - 1)
    k_sel_rows = jnp.where(is_past, cak[phys, off], take_rows(ka, nidx))
    v_sel_rows = jnp.where(is_past, cav[phys, off], take_rows(va, nidx))
    return sparse_attend(qa, k_sel_rows, v_sel_rows, valid)


def diff_scores(qb, kb):
    b, t = qb.shape[:2]
    q = qb.reshape(b, t, KVB, HB // KVB, 2, HD)
    return jnp.einsum('btkgmd,bskmd->bkgmts', q, kb).astype(F32) * HD ** -0.5


def diff_prompt(qb, kb, vb, lam):
    b, t = qb.shape[:2]
    nb = t // Q_BLOCK
    qblk = qb.reshape((b, nb, Q_BLOCK) + qb.shape[2:]).swapaxes(0, 1)
    kpos = jnp.arange(t)

    def block(args):
        i, q = args
        qpos = i * Q_BLOCK + jnp.arange(Q_BLOCK)
        s = jnp.where(kpos[None, :] <= qpos[:, None], diff_scores(q, kb), -jnp.inf)
        p = jax.nn.softmax(s, axis=-1)
        a = (p[:, :, :, 0] - lam * p[:, :, :, 1]).astype(vb.dtype)
        return jnp.einsum('bkgts,bskd->btkgd', a, vb)

    o = lax.map(block, (jnp.arange(nb), qblk))
    return o.swapaxes(0, 1).reshape(b, t, KVB, HB // KVB, VB)


def diff_sample(qb, kb, vb, kb_past, vb_past, lam):
    t = qb.shape[1]
    past = kb_past.shape[1]
    causal = jnp.tril(jnp.ones((t, t), dtype=bool))
    s = jnp.concatenate([diff_scores(qb, kb_past), jnp.where(causal, diff_scores(qb, kb), -jnp.inf)], axis=-1)
    p = jax.nn.softmax(s, axis=-1)
    a = (p[:, :, :, 0] - lam * p[:, :, :, 1]).astype(vb.dtype)
    return (jnp.einsum('bkgts,bskd->btkgd', a[..., :past], vb_past)
            + jnp.einsum('bkgts,bskd->btkgd', a[..., past:], vb))


def diff_post(o, g_sub, lam_init):
    b, t = o.shape[:2]
    return (rms_norm(o, g_sub) * (1.0 - lam_init)).reshape(b, t, WIDTH_B)


def mix_prompt(xn, pos, lw, lam, lam_init):
    qa, ka, va, qi, ki, wi, qb, kb, vb = project(xn, pos, lw)
    t = xn.shape[1]
    oa = dsa_prompt(qa, ka, va, qi, ki, wi, min(TOPK_MAX, t // 4))
    ob = diff_post(diff_prompt(qb, kb, vb, lam), lw['g_sub'], lam_init)
    o = jnp.einsum('btm,md->btd', jnp.concatenate([oa, ob], axis=-1), lw['w_out'])
    return o, (ka, va, ki, kb, vb)


def mix_sample(xn, pos, lw, lam, lam_init, cak, cav, cik, cbk, cbv, page_table):
    qa, ka, va, qi, ki, wi, qb, kb, vb = project(xn, pos, lw)
    db, t = xn.shape[:2]
    past = page_table.shape[1] * cak.shape[1]
    ki_past = cik[page_table].reshape(db, past, D_IDX)
    kb_past = cbk[page_table].reshape(db, past, KVB, 2, HD)
    vb_past = cbv[page_table].reshape(db, past, KVB, VB)
    oa = dsa_sample(qa, ka, va, qi, ki, wi, ki_past, cak, cav, page_table, min(TOPK_MAX, (past + t) // 4))
    ob = diff_post(diff_sample(qb, kb, vb, kb_past, vb_past, lam), lw['g_sub'], lam_init)
    o = jnp.einsum('btm,md->btd', jnp.concatenate([oa, ob], axis=-1), lw['w_out'])
    return o, (ka, va, ki, kb, vb)


def moe(x, w_router, b_router, w_gu, b_gu, w_down, b_down):
    b, t, d = x.shape
    xt = x.reshape(b * t, d)
    logits = (xt @ w_router + b_router).astype(F32)
    vals, idx = lax.top_k(logits, TOP_K)
    gates = jax.nn.softmax(vals, axis=-1)
    comb = jnp.sum(jax.nn.one_hot(idx, N_EXPERTS, dtype=F32) * gates[..., None], axis=1).astype(x.dtype)
    y = jnp.zeros_like(xt)
    for e in range(N_EXPERTS):
        h = xt @ w_gu[e] + b_gu[e]
        g, u = h[:, :D_FF], h[:, D_FF:]
        g = jnp.minimum(g, SWIGLU_LIMIT)
        u = jnp.clip(u, -SWIGLU_LIMIT, SWIGLU_LIMIT)
        a = (u + 1.0) * (g * jax.nn.sigmoid(SWIGLU_ALPHA * g))
        y = y + comb[:, e:e + 1] * (a @ w_down[e] + b_down[e])
    return y.reshape(b, t, d)


def setup_inputs(seed: int = 0) -> dict:
    key = jax.random.key(seed)
    ks = jax.random.split(key, 40)
    nrm = lambda i, shape, scale: scale * jax.random.normal(ks[i], shape, F32)
    gain = lambda i, shape: 1.0 + 0.01 * jax.random.normal(ks[i], shape, F32)
    n_pages = PAST_LEN // PAGE_SIZE
    n_used = DEC_BATCH * n_pages
    n_pool = n_used + max(1, n_used // 4)
    page_table = jax.random.permutation(ks[7], n_pool)[:n_used].reshape(DEC_BATCH, n_pages).astype(jnp.int32)
    return {
        'x_prompt': nrm(0, (BATCH, SEQ, D_MODEL), 1.0),
        'x_sample': nrm(1, (DEC_BATCH, DEC_SEQ, D_MODEL), 1.0),
        'cache_a_k': nrm(2, (DEPTH, n_pool, PAGE_SIZE, KVA, HD), 1.0),
        'cache_a_v': nrm(3, (DEPTH, n_pool, PAGE_SIZE, KVA, HD), 1.0),
        'cache_idx_k': nrm(4, (DEPTH, n_pool, PAGE_SIZE, D_IDX), 1.0),
        'cache_b_k': nrm(5, (DEPTH, n_pool, PAGE_SIZE, KVB, 2, HD), 1.0),
        'cache_b_v': nrm(6, (DEPTH, n_pool, PAGE_SIZE, KVB, VB), 1.0),
        'page_table': page_table,
        'c_prompt': nrm(8, (BATCH, D_MODEL), 1.0),
        'c_sample': nrm(9, (DEC_BATCH, D_MODEL), 1.0),
        'w_ada': nrm(10, (DEPTH, D_MODEL, 6 * D_MODEL), D_MODEL ** -0.5),
        'b_ada': nrm(11, (DEPTH, 6 * D_MODEL), 0.01),
        'g_attn': gain(12, (DEPTH, D_MODEL)),
        'w_in': nrm(13, (DEPTH, D_MODEL, N_IN), D_MODEL ** -0.5),
        'g_qa': gain(14, (DEPTH, HD)),
        'g_ka': gain(15, (DEPTH, HD)),
        'g_ki': gain(16, (DEPTH, D_IDX)),
        'g_qb': gain(17, (DEPTH, HD)),
        'g_kb': gain(18, (DEPTH, HD)),
        'lam_q1': nrm(19, (DEPTH, HD), 0.1),
        'lam_k1': nrm(20, (DEPTH, HD), 0.1),
        'lam_q2': nrm(21, (DEPTH, HD), 0.1),
        'lam_k2': nrm(22, (DEPTH, HD), 0.1),
        'g_sub': gain(23, (DEPTH, VB)),
        'w_out': nrm(24, (DEPTH, WIDTH_A + WIDTH_B, D_MODEL), (WIDTH_A + WIDTH_B) ** -0.5),
        'g_ffn': gain(25, (DEPTH, D_MODEL)),
        'w_router': nrm(26, (DEPTH, D_MODEL, N_EXPERTS), D_MODEL ** -0.5),
        'b_router': nrm(27, (DEPTH, N_EXPERTS), 0.01),
        'w_gu': nrm(28, (DEPTH, N_EXPERTS, D_MODEL, 2 * D_FF), D_MODEL ** -0.5),
        'b_gu': nrm(29, (DEPTH, N_EXPERTS, 2 * D_FF), 0.01),
        'w_down': nrm(30, (DEPTH, N_EXPERTS, D_FF, D_MODEL), D_FF ** -0.5),
        'b_down': nrm(31, (DEPTH, N_EXPERTS, D_MODEL), 0.01),
    }


def reference(x_prompt, x_sample, cache_a_k, cache_a_v, cache_idx_k, cache_b_k, cache_b_v, page_table,
              c_prompt, c_sample, w_ada, b_ada, g_attn, w_in, g_qa, g_ka, g_ki, g_qb, g_kb,
              lam_q1, lam_k1, lam_q2, lam_k2, g_sub, w_out, g_ffn, w_router, b_router,
              w_gu, b_gu, w_down, b_down):
    t_p = x_prompt.shape[1]
    t_s = x_sample.shape[1]
    past = page_table.shape[1] * cache_a_k.shape[2]
    pos_p = jnp.arange(t_p, dtype=jnp.int32)
    pos_s = past + jnp.arange(t_s, dtype=jnp.int32)
    hp, hs = x_prompt, x_sample
    rows_p = [[] for _ in range(5)]
    rows_s = [[] for _ in range(5)]
    for l in range(DEPTH):
        lw = {'w_in': w_in[l], 'g_qa': g_qa[l], 'g_ka': g_ka[l], 'g_ki': g_ki[l], 'g_qb': g_qb[l],
              'g_kb': g_kb[l], 'g_sub': g_sub[l], 'w_out': w_out[l]}
        fw = (w_router[l], b_router[l], w_gu[l], b_gu[l], w_down[l], b_down[l])
        lam_init = 0.8 - 0.6 * math.exp(-0.3 * l)
        lam = (jnp.exp(jnp.sum(lam_q1[l].astype(F32) * lam_k1[l].astype(F32)))
               - jnp.exp(jnp.sum(lam_q2[l].astype(F32) * lam_k2[l].astype(F32))) + lam_init)
        sh1, sc1, gt1, sh2, sc2, gt2 = adaln(c_prompt, w_ada[l], b_ada[l])
        o, rows = mix_prompt(modulate(rms_norm(hp, g_attn[l]), sh1, sc1), pos_p, lw, lam, lam_init)
        hp = hp + gt1[:, None, :] * o
        hp = hp + gt2[:, None, :] * moe(modulate(rms_norm(hp, g_ffn[l]), sh2, sc2), *fw)
        for lst, r in zip(rows_p, rows):
            lst.append(r)
        sh1, sc1, gt1, sh2, sc2, gt2 = adaln(c_sample, w_ada[l], b_ada[l])
        o, rows = mix_sample(modulate(rms_norm(hs, g_attn[l]), sh1, sc1), pos_s, lw, lam, lam_init,
                             cache_a_k[l], cache_a_v[l], cache_idx_k[l], cache_b_k[l], cache_b_v[l], page_table)
        hs = hs + gt1[:, None, :] * o
        hs = hs + gt2[:, None, :] * moe(modulate(rms_norm(hs, g_ffn[l]), sh2, sc2), *fw)
        for lst, r in zip(rows_s, rows):
            lst.append(r)
    new_a_k_prompt = jnp.stack(rows_p[0])
    new_a_v_prompt = jnp.stack(rows_p[1])
    new_idx_k_prompt = jnp.stack(rows_p[2])
    new_b_k_prompt = jnp.stack(rows_p[3])
    new_b_v_prompt = jnp.stack(rows_p[4])
    new_a_k_sample = jnp.stack(rows_s[0])
    new_a_v_sample = jnp.stack(rows_s[1])
    new_idx_k_sample = jnp.stack(rows_s[2])
    new_b_k_sample = jnp.stack(rows_s[3])
    new_b_v_sample = jnp.stack(rows_s[4])
    return (hp, hs, new_a_k_prompt, new_a_v_prompt, new_idx_k_prompt, new_b_k_prompt, new_b_v_prompt,
            new_a_k_sample, new_a_v_sample, new_idx_k_sample, new_b_k_sample, new_b_v_sample)
```

```python
import functools
import math

import jax
import jax.numpy as jnp
from jax import lax
from jax.experimental import pallas as pl
from jax.experimental.pallas import tpu as pltpu

F32 = jnp.float32
BF16 = jnp.bfloat16
I32 = jnp.int32

HD = 64
HA = 8
KVA = 2
H_IDX = 8
D_IDX = 64
HB = 4
KVB = 2
VB = 128
N_EXPERTS = 32
TOP_K = 4
TOPK_MAX = 256
SWIGLU_LIMIT = 7.0
SWIGLU_ALPHA = 1.702
ROPE_THETA = 10000.0
EPS = 1e-6
Q_BLOCK = 256
K_CHUNK = 256
LANES = 128
INT_MIN = -2 ** 31
NEG_BIAS = -1e30
M_INIT = -1e20
VMEM_LIMIT = 56 * 1024 * 1024

C_QA, C_QI, C_QB, C_KA, C_KB, C_KI, C_WI, C_VA, C_VB, C_END = (
    0, 512, 1024, 1536, 1664, 1920, 2048, 2176, 2304, 2560)


def _dot(a, b):
    return jnp.dot(a, b, preferred_element_type=F32)


def _dot_nt(a, b):
    return lax.dot_general(a, b, (((1,), (1,)), ((), ())), preferred_element_type=F32)


def _split_bf16(a):
    hi = a.astype(BF16)
    lo = (a - hi.astype(F32)).astype(BF16)
    return hi, lo


def _dot3(a, b):
    ah, al = _split_bf16(a)
    bh, bl = _split_bf16(b)
    return _dot(ah, bh) + (_dot(ah, bl) + _dot(al, bh))


def _cparams(*sem):
    return pltpu.CompilerParams(dimension_semantics=sem, vmem_limit_bytes=VMEM_LIMIT)


def _adaln_kernel(c_ref, w_ref, b_ref, o_ref):
    c = c_ref[...]
    s = c * jax.nn.sigmoid(c)
    o_ref[...] = _dot3(s, w_ref[...]) + b_ref[...]


def _adaln(c, w_ada, b_ada):
    n, d = c.shape
    e = w_ada.shape[1]
    tn = 1024
    return pl.pallas_call(
        _adaln_kernel,
        out_shape=jax.ShapeDtypeStruct((n, e), F32),
        grid=(e // tn,),
        in_specs=[pl.BlockSpec((n, d), lambda j: (0, 0)),
                  pl.BlockSpec((d, tn), lambda j: (0, j)),
                  pl.BlockSpec((1, tn), lambda j: (0, j))],
        out_specs=pl.BlockSpec((n, tn), lambda j: (0, j)),
        compiler_params=_cparams("arbitrary"),
        name="adaln",
    )(c, w_ada, b_ada.reshape(1, e))


def _rot_half(v):
    lane = lax.broadcasted_iota(I32, v.shape, 1)
    return jnp.where((lane % HD) < HD // 2, pltpu.roll(v, LANES - HD // 2, 1), pltpu.roll(v, HD // 2, 1))


def _norm_rope(z, c0, width, gain, pn, cs, sn, out_scale):
    outs = []
    for g in range(width // LANES):
        v = z[:, c0 + g * LANES:c0 + (g + 1) * LANES]
        if gain is not None:
            ms = _dot((v * v).astype(BF16), pn)
            v = v * lax.rsqrt(ms + EPS) * gain
        v = v * cs + _rot_half(v) * sn
        if out_scale != 1.0:
            v = v * out_scale
        outs.append(v)
    return outs


def _proj_kernel(x_ref, sh_ref, sc_ref, g_ref, w_ref, cs_ref, sn_ref, gains_ref, pn_ref,
                 qa_ref, qi_ref, qb_ref, ka_ref, kb_ref, ki_ref, wi_ref, va_ref, vb_ref,
                 kab_ref, kbb_ref, kib_ref, vab_ref, vbb_ref):
    x = x_ref[...]
    ms = jnp.mean(x * x, axis=-1, keepdims=True)
    xn = x * lax.rsqrt(ms + EPS) * g_ref[...]
    xn = xn * (1.0 + sc_ref[...]) + sh_ref[...]
    z = _dot(xn.astype(BF16), w_ref[...])
    cs = cs_ref[...]
    sn = sn_ref[...]
    pn = pn_ref[...]
    gains = gains_ref[...]
    g_qa, g_ka, g_ki, g_qb, g_kb = (gains[r:r + 1, :] for r in range(5))

    qa = _norm_rope(z, C_QA, 512, g_qa, pn, cs, sn, HD ** -0.5)
    qi = _norm_rope(z, C_QI, 512, None, pn, cs, sn, 1.0)
    qb = _norm_rope(z, C_QB, 512, g_qb, pn, cs, sn, HD ** -0.5)
    for g in range(4):
        sl = slice(g * LANES, (g + 1) * LANES)
        qa_ref[:, sl] = qa[g].astype(BF16)
        qi_ref[:, sl] = qi[g].astype(BF16)
        qb_ref[:, sl] = qb[g].astype(BF16)
    ka = _norm_rope(z, C_KA, 128, g_ka, pn, cs, sn, 1.0)[0]
    ka_ref[...] = ka
    kab_ref[...] = ka.astype(BF16)
    kb = _norm_rope(z, C_KB, 256, g_kb, pn, cs, sn, 1.0)
    for g in range(2):
        sl = slice(g * LANES, (g + 1) * LANES)
        kb_ref[:, sl] = kb[g]
        kbb_ref[:, sl] = kb[g].astype(BF16)
    ki = _norm_rope(z, C_KI, 128, g_ki, pn, cs, sn, 1.0)[0]
    ki_ref[...] = ki[:, :D_IDX]
    kib_ref[...] = ki.astype(BF16)
    wi_ref[...] = z[:, C_WI:C_WI + LANES] * (H_IDX ** -0.5 * D_IDX ** -0.5)
    va = z[:, C_VA:C_VA + 128]
    va_ref[...] = va
    vab_ref[...] = va.astype(BF16)
    vb = z[:, C_VB:C_VB + 256]
    vb_ref[...] = vb
    vbb_ref[...] = vb.astype(BF16)


def _permute_w_in(w_in):
    d = w_in.shape[0]
    seg = lambda a, b: w_in[:, a:b]
    cols = [seg(0, 512), seg(768, 1280), seg(1352, 1864), seg(512, 640), seg(1864, 2120),
            seg(1280, 1344), seg(1280, 1344), seg(1344, 1352), jnp.zeros((d, LANES - H_IDX), w_in.dtype),
            seg(640, 768), seg(2120, 2376)]
    return jnp.concatenate(cols, axis=1).astype(BF16)


def _rope_tables(pos):
    half = HD // 2
    inv = ROPE_THETA ** (-jnp.arange(half, dtype=F32) / half)
    ang = pos.astype(F32)[:, None] * inv[None, :]
    cos, sin = jnp.cos(ang), jnp.sin(ang)
    cs = jnp.concatenate([cos, cos, cos, cos], axis=1)
    sn = jnp.concatenate([-sin, sin, -sin, sin], axis=1)
    return cs, sn


def _head_mean_matrix():
    r = lax.broadcasted_iota(I32, (LANES, LANES), 0) // HD
    c = lax.broadcasted_iota(I32, (LANES, LANES), 1) // HD
    return jnp.where(r == c, 1.0 / HD, 0.0).astype(BF16)


def _mod_spec(arr, tm, steps_per_seq):
    if arr.ndim == 3:
        return pl.BlockSpec((None, 1, arr.shape[-1]), lambda i: (i // steps_per_seq, 0, 0))
    return pl.BlockSpec((tm, arr.shape[-1]), lambda i: (i, 0))


def _proj(x2, sh, sc, g_attn, w_perm, cs, sn, gains, tm, steps_per_seq):
    n, d = x2.shape
    if cs.shape[0] == n:
        tab_spec = pl.BlockSpec((tm, LANES), lambda i: (i, 0))
    else:
        tab_spec = pl.BlockSpec((tm, LANES), lambda i: (i % steps_per_seq, 0))
    const = lambda shape: pl.BlockSpec(shape, lambda i: (0,) * len(shape))
    widths = [(512, BF16), (512, BF16), (512, BF16), (128, F32), (256, F32), (D_IDX, F32), (128, F32),
              (128, F32), (256, F32), (128, BF16), (256, BF16), (128, BF16), (128, BF16), (256, BF16)]
    return pl.pallas_call(
        _proj_kernel,
        out_shape=[jax.ShapeDtypeStruct((n, w), dt) for w, dt in widths],
        grid=(n // tm,),
        in_specs=[pl.BlockSpec((tm, d), lambda i: (i, 0)),
                  _mod_spec(sh, tm, steps_per_seq), _mod_spec(sc, tm, steps_per_seq),
                  const((1, d)), const((d, C_END)), tab_spec, tab_spec,
                  const((8, LANES)), const((LANES, LANES))],
        out_specs=[pl.BlockSpec((tm, w), lambda i: (i, 0)) for w, _ in widths],
        compiler_params=_cparams("arbitrary"),
        name="proj",
    )(x2, sh, sc, g_attn.reshape(1, d), w_perm, cs, sn, gains, _head_mean_matrix())


def _sortable_key(x):
    bits = lax.bitcast_convert_type(x, I32)
    return bits ^ ((bits >> 31) & 0x7FFFFFFF)


def _fold_lanes(x):
    s = x[:, 0:LANES]
    for g in range(1, x.shape[1] // LANES):
        s = s + x[:, g * LANES:(g + 1) * LANES]
    return s


def _count(skey_ref, nkc, pred, rows):
    def body(c, acc):
        return acc + _fold_lanes(jnp.where(pred(skey_ref[c], c), 1.0, 0.0))
    acc = lax.fori_loop(0, nkc, body, jnp.zeros((rows, LANES), F32))
    return jnp.sum(acc, axis=-1, keepdims=True)


def _select_topk(skey_ref, nkc, ksel, rows, ck):
    kf = float(ksel)

    def bit_body(i, t):
        bit = lax.shift_left(jnp.int32(1), 31 - i)
        cand_u = t | bit
        cand_s = cand_u ^ INT_MIN
        cnt = _count(skey_ref, nkc, lambda k, c: k >= cand_s, rows)
        return jnp.where(cnt >= kf, cand_u, t)

    t_u = lax.fori_loop(0, 32, bit_body, jnp.zeros((rows, 1), I32))
    thr = t_u ^ INT_MIN
    n_ge = _count(skey_ref, nkc, lambda k, c: k >= thr, rows)
    n_gt = _count(skey_ref, nkc, lambda k, c: k > thr, rows)
    excess = jnp.logical_and(n_ge > kf, thr != INT_MIN)
    need = kf - n_gt

    @pl.when(jnp.max(jnp.where(excess, 1.0, 0.0)) > 0.0)
    def _():
        col = lambda c: c * ck + lax.broadcasted_iota(I32, (rows, ck), 1)
        nbits = max(1, (skey_ref.shape[0] * ck).bit_length())

        def idx_body(i, p):
            cand = p | lax.shift_left(jnp.int32(1), nbits - 1 - i)
            cnt = _count(skey_ref, nkc, lambda k, c: jnp.logical_and(k == thr, col(c) < cand), rows)
            return jnp.where(cnt <= need, cand, p)

        p_keep = lax.fori_loop(0, nbits, idx_body, jnp.zeros((rows, 1), I32))

        def drop_body(c, _):
            k = skey_ref[c]
            drop = jnp.logical_and(jnp.logical_and(k == thr, col(c) >= p_keep), excess)
            skey_ref[c] = jnp.where(drop, INT_MIN, k)
            return 0

        lax.fori_loop(0, nkc, drop_body, 0)

    return jnp.maximum(thr, INT_MIN + 1)


def _flash_init(m_ref, acc_ref):
    m_ref[...] = jnp.full(m_ref.shape, M_INIT, F32)
    acc_ref[...] = jnp.zeros(acc_ref.shape, F32)


def _flash_update(s, vaug_list, m_ref, acc_ref):
    m_prev = m_ref[...]
    m_new = jnp.maximum(m_prev, jnp.max(s, axis=-1, keepdims=True))
    alpha = jnp.exp(m_prev - m_new)
    nrep = s.shape[1] // LANES
    p = jnp.exp(s - (pltpu.repeat(m_new, nrep, 1) if nrep > 1 else m_new)).astype(BF16)
    wv = acc_ref.shape[1]
    upd = None
    for i, va in enumerate(vaug_list):
        nk = va.shape[0]
        d = _dot(p[:, i * nk:(i + 1) * nk], va)
        upd = d if upd is None else upd + d
    acc_ref[...] = pltpu.repeat(alpha, wv // LANES, 1) * acc_ref[...] + upd
    m_ref[...] = m_new


def _with_ones(v):
    return jnp.concatenate([v, jnp.ones((v.shape[0], LANES), v.dtype)], axis=1)


def _dsa_kernel(qi_ref, wi_ref, ki_ref, qa_ref, ka_ref, va_ref, o_ref, skey_ref, m_ref, acc_ref,
                *, tq, ck, ksel):
    i = pl.program_id(1)
    nkc = (i + 1) * (tq // ck)
    lane = lax.broadcasted_iota(I32, (tq, LANES), 1)
    hi_half = lane >= HD
    qi = qi_ref[...]
    w = wi_ref[...]
    zero_b = jnp.zeros((tq, LANES), BF16)
    q_heads = []
    for h in range(H_IDX):
        grp = qi[:, (h // 2) * LANES:(h // 2 + 1) * LANES]
        q_heads.append(jnp.where(hi_half if h % 2 else jnp.logical_not(hi_half), grp, zero_b))
    w_heads = [w[:, h:h + 1] for h in range(H_IDX)]
    rows = i * tq + lax.broadcasted_iota(I32, (tq, ck), 0)
    col_in = lax.broadcasted_iota(I32, (tq, ck), 1)

    def score_body(c, _):
        kc = ki_ref[pl.ds(pl.multiple_of(c * ck, ck), ck), :]
        acc = jnp.zeros((tq, ck), F32)
        for h in range(H_IDX):
            acc = acc + jnp.maximum(_dot_nt(q_heads[h], kc), 0.0) * w_heads[h]
        key = _sortable_key(acc)
        skey_ref[c] = jnp.where(c * ck + col_in <= rows, key, INT_MIN)
        return 0

    lax.fori_loop(0, nkc, score_body, 0)
    thr = _select_topk(skey_ref, nkc, ksel, tq, ck)

    qa = qa_ref[...].astype(F32)

    def head_at_half(h, half):
        grp = qa[:, (h // 2) * LANES:(h // 2 + 1) * LANES]
        if h % 2 != half:
            grp = pltpu.roll(grp, HD, 1)
        return jnp.where(hi_half if half else jnp.logical_not(hi_half), grp, 0.0)

    g_per = HA // KVA
    outs = []
    for j in range(KVA):
        q_rows = jnp.concatenate([head_at_half(g_per * j + g, j) for g in range(g_per)], axis=0).astype(BF16)
        _flash_init(m_ref, acc_ref)

        def att_body(c, _):
            off = pl.multiple_of(c * ck, ck)
            s = _dot_nt(q_rows, ka_ref[pl.ds(off, ck), :])
            bias = jnp.where(skey_ref[c] >= thr, 0.0, NEG_BIAS)
            s = (s.reshape(g_per, tq, ck) + bias[None]).reshape(g_per * tq, ck)
            _flash_update(s, [_with_ones(va_ref[pl.ds(off, ck), :])], m_ref, acc_ref)
            return 0

        lax.fori_loop(0, nkc, att_body, 0)
        acc = acc_ref[...]
        outs.append(acc[:, :LANES] / acc[:, LANES:])

    for gg in range(HA // 2):
        j = (2 * gg) // g_per
        a = outs[j][((2 * gg) % g_per) * tq:((2 * gg) % g_per + 1) * tq]
        b = outs[j][((2 * gg + 1) % g_per) * tq:((2 * gg + 1) % g_per + 1) * tq]
        if j == 1:
            a = pltpu.roll(a, HD, 1)
        else:
            b = pltpu.roll(b, HD, 1)
        o_ref[:, gg * LANES:(gg + 1) * LANES] = jnp.where(hi_half, b, a).astype(o_ref.dtype)


def _dsa_prompt(qi, wi, kib, qa, kab, vab, ksel):
    b, t, _ = qi.shape
    tq, ck = Q_BLOCK, K_CHUNK
    qspec = lambda w: pl.BlockSpec((None, tq, w), lambda bb, i: (bb, i, 0))
    kspec = lambda w: pl.BlockSpec((None, t, w), lambda bb, i: (bb, 0, 0))
    g_per = HA // KVA
    return pl.pallas_call(
        functools.partial(_dsa_kernel, tq=tq, ck=ck, ksel=ksel),
        out_shape=jax.ShapeDtypeStruct((b, t, 512), BF16),
        grid=(b, t // tq),
        in_specs=[qspec(512), qspec(128), kspec(128), qspec(512), kspec(128), kspec(128)],
        out_specs=qspec(512),
        scratch_shapes=[pltpu.VMEM((t // ck, tq, ck), I32),
                        pltpu.VMEM((g_per * tq, LANES), F32),
                        pltpu.VMEM((g_per * tq, 2 * LANES), F32)],
        compiler_params=_cparams("arbitrary", "arbitrary"),
        name="dsa_prompt",
    )(qi, wi, kib, qa, kab, vab)


def _lambda_value(lamv, lam_init):
    l1 = jnp.sum(lamv[0:1, :] * lamv[1:2, :], axis=-1, keepdims=True)
    l2 = jnp.sum(lamv[2:3, :] * lamv[3:4, :], axis=-1, keepdims=True)
    return jnp.exp(l1) - jnp.exp(l2) + lam_init


def _sub_norm(o, gsub, lam_init):
    ms = jnp.mean(o * o, axis=-1, keepdims=True)
    return o * lax.rsqrt(ms + EPS) * gsub * (1.0 - lam_init)


def _diff_kernel(qb_ref, kb_ref, vb_ref, lamv_ref, gsub_ref, o_ref, m_ref, acc_ref, *, tq, ck, lam_init):
    i = pl.program_id(1)
    n_full = i * (tq // ck)
    lane = lax.broadcasted_iota(I32, (tq, LANES), 1)
    hi_half = lane >= HD
    qb = qb_ref[...]
    zero_b = jnp.zeros((tq, LANES), BF16)
    lam = _lambda_value(lamv_ref[...], lam_init)
    gsub = gsub_ref[...]
    g_per = HB // KVB
    r_in = lax.broadcasted_iota(I32, (tq, ck), 0)
    c_in = lax.broadcasted_iota(I32, (tq, ck), 1)

    for j in range(KVB):
        parts = []
        for g in range(g_per):
            grp = qb[:, (g_per * j + g) * LANES:(g_per * j + g + 1) * LANES]
            parts.append(jnp.where(jnp.logical_not(hi_half), grp, zero_b))
            parts.append(jnp.where(hi_half, grp, zero_b))
        q_rows = jnp.concatenate(parts, axis=0)
        kcols = slice(j * LANES, (j + 1) * LANES)
        _flash_init(m_ref, acc_ref)

        def step(c, bias):
            off = pl.multiple_of(c * ck, ck)
            s = _dot_nt(q_rows, kb_ref[pl.ds(off, ck), kcols])
            if bias is not None:
                s = (s.reshape(2 * g_per, tq, ck) + bias[None]).reshape(2 * g_per * tq, ck)
            _flash_update(s, [_with_ones(vb_ref[pl.ds(off, ck), kcols])], m_ref, acc_ref)

        def full_body(c, _):
            step(c, None)
            return 0

        lax.fori_loop(0, n_full, full_body, 0)
        for d in range(tq // ck):
            bias = jnp.where(d * ck + c_in <= r_in, 0.0, NEG_BIAS)
            step(n_full + d, bias)

        acc = acc_ref[...]
        o = acc[:, :LANES] / acc[:, LANES:]
        for g in range(g_per):
            o1 = o[(2 * g) * tq:(2 * g + 1) * tq]
            o2 = o[(2 * g + 1) * tq:(2 * g + 2) * tq]
            hh = g_per * j + g
            o_ref[:, hh * LANES:(hh + 1) * LANES] = _sub_norm(o1 - lam * o2, gsub, lam_init).astype(o_ref.dtype)


def _diff_prompt(qb, kbb, vbb, lamv, gsub, lam_init):
    b, t, _ = qb.shape
    tq, ck = Q_BLOCK, K_CHUNK
    qspec = lambda w: pl.BlockSpec((None, tq, w), lambda bb, i: (bb, i, 0))
    kspec = lambda w: pl.BlockSpec((None, t, w), lambda bb, i: (bb, 0, 0))
    const = lambda shape: pl.BlockSpec(shape, lambda bb, i: (0,) * len(shape))
    rows = 2 * (HB // KVB) * tq
    return pl.pallas_call(
        functools.partial(_diff_kernel, tq=tq, ck=ck, lam_init=lam_init),
        out_shape=jax.ShapeDtypeStruct((b, t, 512), BF16),
        grid=(b, t // tq),
        in_specs=[qspec(512), kspec(256), kspec(256), const((8, LANES)), const((1, LANES))],
        out_specs=qspec(512),
        scratch_shapes=[pltpu.VMEM((rows, LANES), F32), pltpu.VMEM((rows, 2 * LANES), F32)],
        compiler_params=_cparams("arbitrary", "arbitrary"),
        name="diff_prompt",
    )(qb, kbb, vbb, lamv, gsub)


def _head_sum(r):
    r3 = r.reshape(H_IDX, 8, r.shape[1])
    s = r3[0]
    for h in range(1, H_IDX):
        s = s + r3[h]
    return s + 0.0


def _s1_kernel(pt_ref, qm_ref, wcol_ref, knew_ref, *rest, pg, tnew):
    pages = rest[:pg]
    sc_ref, scn_ref = rest[pg], rest[pg + 1]
    qm = qm_ref[...]
    wcol = wcol_ref[...]
    for i in range(pg):
        s = _dot_nt(qm, pages[i][...].astype(BF16))
        sc_ref[:, i * LANES:(i + 1) * LANES] = _head_sum(jnp.maximum(s, 0.0) * wcol)

    @pl.when(pl.program_id(1) == 0)
    def _():
        s = _dot_nt(qm, knew_ref[...])
        sn = _head_sum(jnp.maximum(s, 0.0) * wcol)
        r = lax.broadcasted_iota(I32, sn.shape, 0)
        c = lax.broadcasted_iota(I32, sn.shape, 1)
        scn_ref[...] = jnp.where(jnp.logical_and(c <= r, c < tnew), sn, -jnp.inf)


def _page_specs(width, pg, n):
    return [pl.BlockSpec((None, LANES, width), functools.partial(lambda b, g, pt, i: (pt[b, g * pg + i], 0, 0), i=i))
            for i in range(n)]


def _sample_scores(page_table, qm, wcol, knew_i, cik, pg, tnew):
    db, n_pages = page_table.shape
    grid_spec = pltpu.PrefetchScalarGridSpec(
        num_scalar_prefetch=1,
        grid=(db, n_pages // pg),
        in_specs=[pl.BlockSpec((None, 64, D_IDX), lambda b, g, pt: (b, 0, 0)),
                  pl.BlockSpec((None, 64, LANES), lambda b, g, pt: (b, 0, 0)),
                  pl.BlockSpec((None, LANES, D_IDX), lambda b, g, pt: (b, 0, 0))]
                 + _page_specs(D_IDX, pg, pg),
        out_specs=[pl.BlockSpec((None, 8, pg * LANES), lambda b, g, pt: (b, 0, g)),
                   pl.BlockSpec((None, 8, LANES), lambda b, g, pt: (b, 0, 0))],
    )
    return pl.pallas_call(
        functools.partial(_s1_kernel, pg=pg, tnew=tnew),
        out_shape=[jax.ShapeDtypeStruct((db, 8, n_pages * LANES), F32),
                   jax.ShapeDtypeStruct((db, 8, LANES), F32)],
        grid_spec=grid_spec,
        compiler_params=_cparams("arbitrary", "arbitrary"),
        name="sample_scores",
    )(page_table, qm, wcol, knew_i, *([cik] * pg))


def _s1b_kernel(sc_ref, o_ref, skey_ref, *, ksel):
    nc, rows, ck = skey_ref.shape

    def conv(c, _):
        x = sc_ref[c]
        skey_ref[c] = jnp.where(x == -jnp.inf, INT_MIN, _sortable_key(x))
        return 0

    lax.fori_loop(0, nc, conv, 0)
    thr = _select_topk(skey_ref, nc, ksel, rows, ck)

    def emit(c, _):
        o_ref[c] = jnp.where(skey_ref[c] >= thr, 0.0, NEG_BIAS)
        return 0

    lax.fori_loop(0, nc, emit, 0)


def _sample_select(sct, ksel):
    nc, r, _ = sct.shape
    rb = min(r, 128)
    spec = pl.BlockSpec((nc, rb, LANES), lambda i: (0, i, 0))
    return pl.pallas_call(
        functools.partial(_s1b_kernel, ksel=ksel),
        out_shape=jax.ShapeDtypeStruct(sct.shape, F32),
        grid=(r // rb,),
        in_specs=[spec],
        out_specs=spec,
        scratch_shapes=[pltpu.VMEM((nc, rb, LANES), I32)],
        compiler_params=_cparams("arbitrary"),
        name="sample_select",
    )(sct)


def _s2_kernel(pt_ref, qa_ref, qb_ref, selp_ref, seln_ref, kan_ref, van_ref, kbn_ref, vbn_ref,
               lamv_ref, gsub_ref, *rest, pg, tnew, lam_init):
    akp, avp, bkp, bvp = (rest[k * pg:(k + 1) * pg] for k in range(4))
    oa_ref, ob_ref, ma_ref, acca_ref, mb_ref, accb_ref = rest[4 * pg:]
    g = pl.program_id(1)
    qa = qa_ref[...]
    qb = qb_ref[...]

    @pl.when(g == 0)
    def _():
        _flash_init(ma_ref, acca_ref)
        _flash_init(mb_ref, accb_ref)
        sa = _dot_nt(qa, kan_ref[...]) + jnp.tile(seln_ref[...], (HA, 1))
        _flash_update(sa, [_with_ones(van_ref[...])], ma_ref, acca_ref)
        r = lax.broadcasted_iota(I32, (8, LANES), 0)
        c = lax.broadcasted_iota(I32, (8, LANES), 1)
        causal = jnp.where(jnp.logical_and(c <= r, c < tnew), 0.0, NEG_BIAS)
        sb = _dot_nt(qb, kbn_ref[...]) + jnp.tile(causal, (8, 1))
        _flash_update(sb, [_with_ones(vbn_ref[...])], mb_ref, accb_ref)

    sa = jnp.concatenate([_dot_nt(qa, akp[i][...].astype(BF16)) for i in range(pg)], axis=1)
    sa = sa + jnp.tile(selp_ref[...], (HA, 1))
    _flash_update(sa, [_with_ones(avp[i][...].astype(BF16)) for i in range(pg)], ma_ref, acca_ref)
    sb = jnp.concatenate([_dot_nt(qb, bkp[i][...].astype(BF16)) for i in range(pg)], axis=1)
    _flash_update(sb, [_with_ones(bvp[i][...].astype(BF16)) for i in range(pg)], mb_ref, accb_ref)

    @pl.when(g == pl.num_programs(1) - 1)
    def _():
        acca = acca_ref[...]
        oa_ref[...] = acca[:, :LANES] / acca[:, LANES:]
        accb = accb_ref[...]
        l = accb[:, 2 * LANES:]
        x = (accb[:, :2 * LANES] / jnp.concatenate([l, l], axis=1)).reshape(HB, 2, 8, 2 * LANES)
        lam = _lambda_value(lamv_ref[...], lam_init)
        gsub = gsub_ref[...]
        for hh in range(HB):
            j = hh // (HB // KVB)
            o = x[hh, 0] - lam * x[hh, 1]
            ob_ref[:, hh * LANES:(hh + 1) * LANES] = _sub_norm(o[:, j * LANES:(j + 1) * LANES], gsub, lam_init)


def _sample_attend(page_table, qa_rows, qb_rows, selp, seln, kan, van, kbn, vbn, lamv, gsub,
                   cak, cav, cbk, cbv, pg, tnew, lam_init):
    db, n_pages = page_table.shape
    per_b = lambda *shape: pl.BlockSpec((None,) + shape, lambda b, g, pt: (b,) + (0,) * len(shape))
    const = lambda shape: pl.BlockSpec(shape, lambda b, g, pt: (0,) * len(shape))
    grid_spec = pltpu.PrefetchScalarGridSpec(
        num_scalar_prefetch=1,
        grid=(db, n_pages // pg),
        in_specs=[per_b(64, 128), per_b(64, 256),
                  pl.BlockSpec((None, 8, pg * LANES), lambda b, g, pt: (b, 0, g)), per_b(8, LANES),
                  per_b(LANES, 128), per_b(LANES, 128), per_b(LANES, 256), per_b(LANES, 256),
                  const((8, LANES)), const((1, LANES))]
                 + _page_specs(128, pg, pg) + _page_specs(128, pg, pg)
                 + _page_specs(256, pg, pg) + _page_specs(256, pg, pg),
        out_specs=[per_b(64, 128), per_b(8, 512)],
        scratch_shapes=[pltpu.VMEM((64, LANES), F32), pltpu.VMEM((64, 2 * LANES), F32),
                        pltpu.VMEM((64, LANES), F32), pltpu.VMEM((64, 3 * LANES), F32)],
    )
    return pl.pallas_call(
        functools.partial(_s2_kernel, pg=pg, tnew=tnew, lam_init=lam_init),
        out_shape=[jax.ShapeDtypeStruct((db, 64, 128), F32), jax.ShapeDtypeStruct((db, 8, 512), F32)],
        grid_spec=grid_spec,
        compiler_params=_cparams("arbitrary", "arbitrary"),
        name="sample_attend",
    )(page_table, qa_rows, qb_rows, selp, seln, kan, van, kbn, vbn, lamv, gsub,
      *([cak] * pg), *([cav] * pg), *([cbk] * pg), *([cbv] * pg))


def _outproj_kernel(oa_ref, ob_ref, x_ref, gt_ref, sh_ref, sc_ref, g_ref, wo_ref, wr_ref, br_ref,
                    h_ref, xm_ref, comb_ref):
    half = oa_ref.shape[1]
    o = _dot(oa_ref[...].astype(BF16), wo_ref[:half, :]) + _dot(ob_ref[...].astype(BF16), wo_ref[half:, :])
    h = x_ref[...] + gt_ref[...] * o
    h_ref[...] = h
    ms = jnp.mean(h * h, axis=-1, keepdims=True)
    xm = h * lax.rsqrt(ms + EPS) * g_ref[...]
    xm = xm * (1.0 + sc_ref[...]) + sh_ref[...]
    xm_ref[...] = xm.astype(BF16)
    logits = _dot3(xm, wr_ref[...]) + br_ref[...]
    lane = lax.broadcasted_iota(I32, logits.shape, 1).astype(F32)
    vals, hots = [], []
    for _ in range(TOP_K):
        m = jnp.max(logits, axis=-1, keepdims=True)
        idx = jnp.min(jnp.where(logits == m, lane, float(LANES)), axis=-1, keepdims=True)
        hot = lane == idx
        vals.append(m)
        hots.append(hot)
        logits = jnp.where(hot, -3e38, logits)
    es = [jnp.exp(v - vals[0]) for v in vals]
    den = es[0] + es[1] + es[2] + es[3]
    comb = jnp.zeros(logits.shape, F32)
    for e, hot in zip(es, hots):
        comb = comb + jnp.where(hot, e / den, 0.0)
    comb_ref[...] = comb


def _outproj(oa, ob, x2, gt, sh, sc, g_ffn, w_out_b, w_router_p, b_router_p, tm, steps_per_seq):
    n, d = x2.shape
    const = lambda shape: pl.BlockSpec(shape, lambda i: (0,) * len(shape))
    row = lambda w: pl.BlockSpec((tm, w), lambda i: (i, 0))
    return pl.pallas_call(
        _outproj_kernel,
        out_shape=[jax.ShapeDtypeStruct((n, d), F32), jax.ShapeDtypeStruct((n, d), BF16),
                   jax.ShapeDtypeStruct((n, LANES), F32)],
        grid=(n // tm,),
        in_specs=[row(oa.shape[1]), row(ob.shape[1]), row(d),
                  _mod_spec(gt, tm, steps_per_seq), _mod_spec(sh, tm, steps_per_seq),
                  _mod_spec(sc, tm, steps_per_seq),
                  const((1, d)), const(w_out_b.shape), const((d, LANES)), const((1, LANES))],
        out_specs=[row(d), row(d), row(LANES)],
        compiler_params=_cparams("arbitrary"),
        name="outproj",
    )(oa, ob, x2, gt, sh, sc, g_ffn.reshape(1, d), w_out_b, w_router_p, b_router_p)


def _moe_kernel(x_ref, comb_ref, h_ref, gt_ref, wgu_ref, bgu_ref, wd_ref, bd_ref, y_ref, acc_ref):
    e = pl.program_id(1)
    dff = wd_ref.shape[0]

    @pl.when(e == 0)
    def _():
        acc_ref[...] = jnp.zeros(acc_ref.shape, F32)

    hgu = _dot(x_ref[...], wgu_ref[...]) + bgu_ref[...]
    gate = jnp.minimum(hgu[:, :dff], SWIGLU_LIMIT)
    up = jnp.clip(hgu[:, dff:], -SWIGLU_LIMIT, SWIGLU_LIMIT)
    a = (up + 1.0) * (gate * jax.nn.sigmoid(SWIGLU_ALPHA * gate))
    yo = _dot(a.astype(BF16), wd_ref[...]) + bd_ref[...]
    comb = comb_ref[...]
    lane = lax.broadcasted_iota(I32, comb.shape, 1)
    ce = jnp.sum(jnp.where(lane == e, comb, 0.0), axis=-1, keepdims=True)
    acc_ref[...] += ce * yo

    @pl.when(e == pl.num_programs(1) - 1)
    def _():
        y_ref[...] = h_ref[...] + gt_ref[...] * acc_ref[...]


def _moe(xm, comb, h, gt, wgu_b, b_gu, wd_b, b_down, tm, steps_per_seq):
    n, d = h.shape
    ne, _, dff2 = wgu_b.shape
    dff = dff2 // 2
    if gt.ndim == 3:
        gt_spec = pl.BlockSpec((None, 1, d), lambda i, e: (i // steps_per_seq, 0, 0))
    else:
        gt_spec = pl.BlockSpec((tm, d), lambda i, e: (i, 0))
    row = lambda w: pl.BlockSpec((tm, w), lambda i, e: (i, 0))
    return pl.pallas_call(
        _moe_kernel,
        out_shape=jax.ShapeDtypeStruct((n, d), F32),
        grid=(n // tm, ne),
        in_specs=[row(d), row(LANES), row(d), gt_spec,
                  pl.BlockSpec((None, d, dff2), lambda i, e: (e, 0, 0)),
                  pl.BlockSpec((None, 1, dff2), lambda i, e: (e, 0, 0)),
                  pl.BlockSpec((None, dff, d), lambda i, e: (e, 0, 0)),
                  pl.BlockSpec((None, 1, d), lambda i, e: (e, 0, 0))],
        out_specs=row(d),
        scratch_shapes=[pltpu.VMEM((tm, d), F32)],
        compiler_params=_cparams("arbitrary", "arbitrary"),
        name="moe",
    )(xm, comb, h, gt, wgu_b, b_gu.reshape(ne, 1, dff2), wd_b, b_down.reshape(ne, 1, d))


def _token_tile(n):
    for tm in (512, 256, 128, 64, 32, 16, 8):
        if n % tm == 0:
            return tm
    raise ValueError(f"token count {n} must be a multiple of 8")


def kernel(x_prompt, x_sample, cache_a_k, cache_a_v, cache_idx_k, cache_b_k, cache_b_v, page_table,
           c_prompt, c_sample, w_ada, b_ada, g_attn, w_in, g_qa, g_ka, g_ki, g_qb, g_kb,
           lam_q1, lam_k1, lam_q2, lam_k2, g_sub, w_out, g_ffn, w_router, b_router,
           w_gu, b_gu, w_down, b_down):
    bsz, t_p, d = x_prompt.shape
    db, t_s, _ = x_sample.shape
    depth = w_in.shape[0]
    n_pool, page = cache_a_k.shape[1], cache_a_k.shape[2]
    n_pages = page_table.shape[1]
    past = n_pages * page
    assert page == LANES and t_s == 8 and t_p % Q_BLOCK == 0 and d == 1024
    n_p, n_s = bsz * t_p, db * t_s
    pg = math.gcd(n_pages, 8)

    cs_p, sn_p = _rope_tables(jnp.arange(t_p, dtype=I32))
    cs_s, sn_s = _rope_tables(past + jnp.arange(t_s, dtype=I32))
    cs_s, sn_s = jnp.tile(cs_s, (db, 1)), jnp.tile(sn_s, (db, 1))
    tile2 = lambda g: jnp.concatenate([g, g]).astype(F32)
    ksel_p = min(TOPK_MAX, t_p // 4)
    ksel_s = min(TOPK_MAX, (past + t_s) // 4)
    tm_p, tm_s = _token_tile(t_p), _token_tile(n_s)
    sps_p = t_p // tm_p
    per_tok = lambda m: jnp.repeat(m, t_s, axis=0)

    hp = x_prompt.reshape(n_p, d)
    hs = x_sample.reshape(n_s, d)
    rows_p = [[] for _ in range(5)]
    rows_s = [[] for _ in range(5)]
    for l in range(depth):
        lam_init = 0.8 - 0.6 * math.exp(-0.3 * l)
        mods = _adaln(jnp.concatenate([c_prompt, c_sample], axis=0), w_ada[l], b_ada[l])
        mp = [m.reshape(bsz, 1, d) for m in jnp.split(mods[:bsz], 6, axis=-1)]
        ms_ = [per_tok(m) for m in jnp.split(mods[bsz:], 6, axis=-1)]
        w_perm = _permute_w_in(w_in[l])
        gains = jnp.zeros((8, LANES), F32).at[:5].set(
            jnp.stack([tile2(g_qa[l]), tile2(g_ka[l]), tile2(g_ki[l]), tile2(g_qb[l]), tile2(g_kb[l])]))
        lamv = jnp.zeros((8, LANES), F32).at[:4, :HD].set(
            jnp.stack([lam_q1[l], lam_k1[l], lam_q2[l], lam_k2[l]]).astype(F32))
        gsub = g_sub[l].reshape(1, VB).astype(F32)
        w_out_b = w_out[l].astype(BF16)
        w_router_p = jnp.zeros((d, LANES), F32).at[:, :N_EXPERTS].set(w_router[l])
        b_router_p = jnp.full((1, LANES), NEG_BIAS, F32).at[0, :N_EXPERTS].set(b_router[l])
        wgu_b = w_gu[l].astype(BF16)
        wd_b = w_down[l].astype(BF16)

        (qa, qi, qb, ka, kb, ki, wi, va, vb, kab, kbb, kib, vab, vbb) = _proj(
            hp, mp[0], mp[1], g_attn[l], w_perm, cs_p, sn_p, gains, tm_p, sps_p)
        r3 = lambda a: a.reshape(bsz, t_p, a.shape[-1])
        oa = _dsa_prompt(r3(qi), r3(wi), r3(kib), r3(qa), r3(kab), r3(vab), ksel_p)
        ob = _diff_prompt(r3(qb), r3(kbb), r3(vbb), lamv, gsub, lam_init)
        h1, xm, comb = _outproj(oa.reshape(n_p, 512), ob.reshape(n_p, 512), hp, mp[2], mp[3], mp[4],
                                g_ffn[l], w_out_b, w_router_p, b_router_p, tm_p, sps_p)
        hp = _moe(xm, comb, h1, mp[5], wgu_b, b_gu[l], wd_b, b_down[l], tm_p, sps_p)
        for lst, r in zip(rows_p, (ka.reshape(bsz, t_p, KVA, HD), va.reshape(bsz, t_p, KVA, HD),
                                   ki.reshape(bsz, t_p, D_IDX), kb.reshape(bsz, t_p, KVB, 2, HD),
                                   vb.reshape(bsz, t_p, KVB, VB))):
            lst.append(r)

        (qa, qi, qb, ka, kb, ki, wi, va, vb, kab, kbb, kib, vab, vbb) = _proj(
            hs, ms_[0], ms_[1], g_attn[l], w_perm, cs_s, sn_s, gains, tm_s, 1)
        qm = qi.reshape(db, t_s, H_IDX, D_IDX).transpose(0, 2, 1, 3).reshape(db, 64, D_IDX)
        wcol = jnp.broadcast_to(
            wi[:, :H_IDX].reshape(db, t_s, H_IDX).transpose(0, 2, 1).reshape(db, 64, 1), (db, 64, LANES))
        pad_new = lambda a: jnp.pad(a.reshape(db, t_s, a.shape[-1]), ((0, 0), (0, LANES - t_s), (0, 0)))
        sc_past, sc_new = _sample_scores(page_table, qm, wcol, pad_new(kib[:, :D_IDX]),
                                         cache_idx_k[l], pg, t_s)
        sct = jnp.concatenate([sc_past, sc_new], axis=-1).reshape(n_s, n_pages + 1, LANES).transpose(1, 0, 2)
        sel = _sample_select(sct, ksel_s).transpose(1, 0, 2).reshape(db, t_s, (n_pages + 1) * LANES)
        qa4 = qa.reshape(db, t_s, HA, HD).transpose(0, 2, 1, 3)
        zq = jnp.zeros_like(qa4[:, :HA // 2])
        qa_rows = jnp.concatenate([jnp.concatenate([qa4[:, :HA // 2], zq], axis=-1),
                                   jnp.concatenate([zq, qa4[:, HA // 2:]], axis=-1)], axis=1).reshape(db, 64, 128)
        qb5 = qb.reshape(db, t_s, HB, 2, HD).transpose(0, 2, 3, 1, 4)
        zb = jnp.zeros_like(qb5[:, 0, 0])
        blocks = []
        for hh in range(HB):
            for m in range(2):
                pos = (hh // (HB // KVB)) * 2 + m
                blocks.append(jnp.concatenate([qb5[:, hh, m] if p == pos else zb for p in range(4)], axis=-1))
        qb_rows = jnp.stack(blocks, axis=1).reshape(db, 64, 256)
        oa_rows, ob = _sample_attend(
            page_table, qa_rows, qb_rows, sel[:, :, :past], sel[:, :, past:],
            pad_new(kab), pad_new(vab), pad_new(kbb), pad_new(vbb), lamv, gsub,
            cache_a_k[l].reshape(n_pool, page, KVA * HD), cache_a_v[l].reshape(n_pool, page, KVA * HD),
            cache_b_k[l].reshape(n_pool, page, KVB * 2 * HD), cache_b_v[l].reshape(n_pool, page, KVB * VB),
            pg, t_s, lam_init)
        oa5 = oa_rows.reshape(db, HA, t_s, KVA, HD)
        oa = jnp.concatenate([oa5[:, :HA // 2, :, 0], oa5[:, HA // 2:, :, 1]], axis=1)
        oa = oa.transpose(0, 2, 1, 3).reshape(n_s, HA * HD)
        h1, xm, comb = _outproj(oa, ob.reshape(n_s, 512), hs, ms_[2], ms_[3], ms_[4],
                                g_ffn[l], w_out_b, w_router_p, b_router_p, tm_s, 1)
        hs = _moe(xm, comb, h1, ms_[5], wgu_b, b_gu[l], wd_b, b_down[l], tm_s, 1)
        for lst, r in zip(rows_s, (ka.reshape(db, t_s, KVA, HD), va.reshape(db, t_s, KVA, HD),
                                   ki.reshape(db, t_s, D_IDX), kb.reshape(db, t_s, KVB, 2, HD),
                                   vb.reshape(db, t_s, KVB, VB))):
            lst.append(r)

    outs_p = [jnp.stack(r) for r in rows_p]
    outs_s = [jnp.stack(r) for r in rows_s]
    return (hp.reshape(bsz, t_p, d), hs.reshape(db, t_s, d), *outs_p, *outs_s)
```

```python
import functools
import math

import jax
import jax.numpy as jnp
from jax import lax
from jax.experimental import pallas as pl
from jax.experimental.pallas import tpu as pltpu

F32 = jnp.float32
BF16 = jnp.bfloat16
I32 = jnp.int32

HD = 64
HA = 8
KVA = 2
H_IDX = 8
D_IDX = 64
HB = 4
KVB = 2
VB = 128
N_EXPERTS = 32
TOP_K = 4
TOPK_MAX = 256
SWIGLU_LIMIT = 7.0
SWIGLU_ALPHA = 1.702
ROPE_THETA = 10000.0
EPS = 1e-6
Q_BLOCK = 256
K_CHUNK = 256
LANES = 128
INT_MIN = -2 ** 31
NEG_BIAS = -1e30
M_INIT = -1e20
VMEM_LIMIT = 56 * 1024 * 1024

C_QA, C_QI, C_QB, C_KA, C_KB, C_KI, C_WI, C_VA, C_VB, C_END = (
    0, 512, 1024, 1536, 1664, 1920, 2048, 2176, 2304, 2560)


def _dot(a, b):
    return jnp.dot(a, b, preferred_element_type=F32)


def _dot_nt(a, b):
    return lax.dot_general(a, b, (((1,), (1,)), ((), ())), preferred_element_type=F32)


def _split_bf16(a):
    hi = a.astype(BF16)
    lo = (a - hi.astype(F32)).astype(BF16)
    return hi, lo


def _dot3(a, b):
    ah, al = _split_bf16(a)
    bh, bl = _split_bf16(b)
    return _dot(ah, bh) + (_dot(ah, bl) + _dot(al, bh))


def _cparams(*sem):
    return pltpu.CompilerParams(dimension_semantics=sem, vmem_limit_bytes=VMEM_LIMIT)


def _adaln_kernel(c_ref, w_ref, b_ref, o_ref):
    c = c_ref[...]
    s = c * jax.nn.sigmoid(c)
    o_ref[...] = _dot3(s, w_ref[...]) + b_ref[...]


def _adaln(c, w_ada, b_ada):
    n, d = c.shape
    e = w_ada.shape[1]
    tn = 1024
    return pl.pallas_call(
        _adaln_kernel,
        out_shape=jax.ShapeDtypeStruct((n, e), F32),
        grid=(e // tn,),
        in_specs=[pl.BlockSpec((n, d), lambda j: (0, 0)),
                  pl.BlockSpec((d, tn), lambda j: (0, j)),
                  pl.BlockSpec((1, tn), lambda j: (0, j))],
        out_specs=pl.BlockSpec((n, tn), lambda j: (0, j)),
        compiler_params=_cparams("arbitrary"),
        name="adaln",
    )(c, w_ada, b_ada.reshape(1, e))


def _rot_half(v):
    lane = lax.broadcasted_iota(I32, v.shape, 1)
    return jnp.where((lane % HD) < HD // 2, pltpu.roll(v, LANES - HD // 2, 1), pltpu.roll(v, HD // 2, 1))


def _norm_rope(z, c0, width, gain, pn, cs, sn, out_scale):
    outs = []
    for g in range(width // LANES):
        v = z[:, c0 + g * LANES:c0 + (g + 1) * LANES]
        if gain is not None:
            ms = _dot((v * v).astype(BF16), pn)
            v = v * lax.rsqrt(ms + EPS) * gain
        v = v * cs + _rot_half(v) * sn
        if out_scale != 1.0:
            v = v * out_scale
        outs.append(v)
    return outs


def _proj_kernel(x_ref, sh_ref, sc_ref, g_ref, w_ref, cs_ref, sn_ref, gains_ref, pn_ref,
                 qa_ref, qi_ref, qb_ref, ka_ref, kb_ref, ki_ref, wi_ref, va_ref, vb_ref,
                 kab_ref, kbb_ref, kib_ref, vab_ref, vbb_ref):
    x = x_ref[...]
    ms = jnp.mean(x * x, axis=-1, keepdims=True)
    xn = x * lax.rsqrt(ms + EPS) * g_ref[...]
    xn = xn * (1.0 + sc_ref[...]) + sh_ref[...]
    z = _dot(xn.astype(BF16), w_ref[...])
    cs = cs_ref[...]
    sn = sn_ref[...]
    pn = pn_ref[...]
    gains = gains_ref[...]
    g_qa, g_ka, g_ki, g_qb, g_kb = (gains[r:r + 1, :] for r in range(5))

    qa = _norm_rope(z, C_QA, 512, g_qa, pn, cs, sn, HD ** -0.5)
    qi = _norm_rope(z, C_QI, 512, None, pn, cs, sn, 1.0)
    qb = _norm_rope(z, C_QB, 512, g_qb, pn, cs, sn, HD ** -0.5)
    for g in range(4):
        sl = slice(g * LANES, (g + 1) * LANES)
        qa_ref[:, sl] = qa[g].astype(BF16)
        qi_ref[:, sl] = qi[g].astype(BF16)
        qb_ref[:, sl] = qb[g].astype(BF16)
    ka = _norm_rope(z, C_KA, 128, g_ka, pn, cs, sn, 1.0)[0]
    ka_ref[...] = ka
    kab_ref[...] = ka.astype(BF16)
    kb = _norm_rope(z, C_KB, 256, g_kb, pn, cs, sn, 1.0)
    for g in range(2):
        sl = slice(g * LANES, (g + 1) * LANES)
        kb_ref[:, sl] = kb[g]
        kbb_ref[:, sl] = kb[g].astype(BF16)
    ki = _norm_rope(z, C_KI, 128, g_ki, pn, cs, sn, 1.0)[0]
    ki_ref[...] = ki[:, :D_IDX]
    kib_ref[...] = ki.astype(BF16)
    wi_ref[...] = z[:, C_WI:C_WI + LANES] * (H_IDX ** -0.5 * D_IDX ** -0.5)
    va = z[:, C_VA:C_VA + 128]
    va_ref[...] = va
    vab_ref[...] = va.astype(BF16)
    vb = z[:, C_VB:C_VB + 256]
    vb_ref[...] = vb
    vbb_ref[...] = vb.astype(BF16)


def _permute_w_in(w_in):
    d = w_in.shape[0]
    seg = lambda a, b: w_in[:, a:b]
    cols = [seg(0, 512), seg(768, 1280), seg(1352, 1864), seg(512, 640), seg(1864, 2120),
            seg(1280, 1344), seg(1280, 1344), seg(1344, 1352), jnp.zeros((d, LANES - H_IDX), w_in.dtype),
            seg(640, 768), seg(2120, 2376)]
    return jnp.concatenate(cols, axis=1).astype(BF16)


def _rope_tables(pos):
    half = HD // 2
    inv = ROPE_THETA ** (-jnp.arange(half, dtype=F32) / half)
    ang = pos.astype(F32)[:, None] * inv[None, :]
    cos, sin = jnp.cos(ang), jnp.sin(ang)
    cs = jnp.concatenate([cos, cos, cos, cos], axis=1)
    sn = jnp.concatenate([-sin, sin, -sin, sin], axis=1)
    return cs, sn


def _head_mean_matrix():
    r = lax.broadcasted_iota(I32, (LANES, LANES), 0) // HD
    c = lax.broadcasted_iota(I32, (LANES, LANES), 1) // HD
    return jnp.where(r == c, 1.0 / HD, 0.0).astype(BF16)


def _mod_spec(arr, tm, steps_per_seq):
    if arr.ndim == 3:
        return pl.BlockSpec((None, 1, arr.shape[-1]), lambda i: (i // steps_per_seq, 0, 0))
    return pl.BlockSpec((tm, arr.shape[-1]), lambda i: (i, 0))


def _proj(x2, sh, sc, g_attn, w_perm, cs, sn, gains, tm, steps_per_seq):
    n, d = x2.shape
    if cs.shape[0] == n:
        tab_spec = pl.BlockSpec((tm, LANES), lambda i: (i, 0))
    else:
        tab_spec = pl.BlockSpec((tm, LANES), lambda i: (i % steps_per_seq, 0))
    const = lambda shape: pl.BlockSpec(shape, lambda i: (0,) * len(shape))
    widths = [(512, BF16), (512, BF16), (512, BF16), (128, F32), (256, F32), (D_IDX, F32), (128, F32),
              (128, F32), (256, F32), (128, BF16), (256, BF16), (128, BF16), (128, BF16), (256, BF16)]
    return pl.pallas_call(
        _proj_kernel,
        out_shape=[jax.ShapeDtypeStruct((n, w), dt) for w, dt in widths],
        grid=(n // tm,),
        in_specs=[pl.BlockSpec((tm, d), lambda i: (i, 0)),
                  _mod_spec(sh, tm, steps_per_seq), _mod_spec(sc, tm, steps_per_seq),
                  const((1, d)), const((d, C_END)), tab_spec, tab_spec,
                  const((8, LANES)), const((LANES, LANES))],
        out_specs=[pl.BlockSpec((tm, w), lambda i: (i, 0)) for w, _ in widths],
        compiler_params=_cparams("arbitrary"),
        name="proj",
    )(x2, sh, sc, g_attn.reshape(1, d), w_perm, cs, sn, gains, _head_mean_matrix())


def _sortable_key(x):
    bits = lax.bitcast_convert_type(x, I32)
    return bits ^ ((bits >> 31) & 0x7FFFFFFF)


COUNT_ROWS = 128


def _count(skey_ref, nkc, pred, rows, ck):
    rb = min(rows, COUNT_ROWS)
    parts = []
    for r0 in range(0, rows, rb):
        def body(c, acc, r0=r0):
            for g in range(ck // LANES):
                k = skey_ref[c, r0:r0 + rb, g * LANES:(g + 1) * LANES]
                acc = acc + jnp.where(pred(k, c, g * LANES, r0), 1.0, 0.0)
            return acc
        parts.append(lax.fori_loop(0, nkc, body, jnp.zeros((rb, LANES), F32)))
    acc = parts[0] if len(parts) == 1 else jnp.concatenate(parts, axis=0)
    return jnp.broadcast_to(jnp.sum(acc, axis=-1, keepdims=True), (rows, LANES))


def _select_topk(skey_ref, nkc, ksel, rows, ck):
    kf = float(ksel)
    rb = min(rows, COUNT_ROWS)
    strip = lambda a, r0: a[r0:r0 + rb]

    def bit_body(i, t):
        cand = t + lax.shift_left(jnp.int32(1), 31 - i)
        cnt = _count(skey_ref, nkc, lambda k, c, c0, r0: k >= strip(cand, r0), rows, ck)
        return jnp.where(cnt >= kf, cand, t)

    thr = lax.fori_loop(0, 32, bit_body, jnp.full((rows, LANES), INT_MIN, I32))
    n_ge = _count(skey_ref, nkc, lambda k, c, c0, r0: k >= strip(thr, r0), rows, ck)
    n_gt = _count(skey_ref, nkc, lambda k, c, c0, r0: k > strip(thr, r0), rows, ck)
    excess = jnp.logical_and(n_ge > kf, thr != INT_MIN)
    need = kf - n_gt

    @pl.when(jnp.max(jnp.where(excess, 1.0, 0.0)) > 0.0)
    def _():
        lane = lax.broadcasted_iota(I32, (rb, LANES), 1)
        nbits = max(1, (skey_ref.shape[0] * ck).bit_length())

        def idx_body(i, p):
            cand = p | lax.shift_left(jnp.int32(1), nbits - 1 - i)
            tied_before = lambda k, c, c0, r0: jnp.logical_and(
                k == strip(thr, r0), c * ck + c0 + lane < strip(cand, r0))
            cnt = _count(skey_ref, nkc, tied_before, rows, ck)
            return jnp.where(cnt <= need, cand, p)

        p_keep = lax.fori_loop(0, nbits, idx_body, jnp.zeros((rows, LANES), I32))

        def drop_body(c, _):
            for r0 in range(0, rows, rb):
                for g in range(ck // LANES):
                    k = skey_ref[c, r0:r0 + rb, g * LANES:(g + 1) * LANES]
                    drop = jnp.logical_and(k == strip(thr, r0), c * ck + g * LANES + lane >= strip(p_keep, r0))
                    drop = jnp.logical_and(drop, strip(excess, r0))
                    skey_ref[c, r0:r0 + rb, g * LANES:(g + 1) * LANES] = jnp.where(drop, INT_MIN, k)
            return 0

        lax.fori_loop(0, nkc, drop_body, 0)

    return jnp.maximum(thr, INT_MIN + 1)


def _flash_init(m_ref, acc_ref):
    m_ref[...] = jnp.full(m_ref.shape, M_INIT, F32)
    acc_ref[...] = jnp.zeros(acc_ref.shape, F32)


def _flash_update(s, pv_fn, m_ref, acc_ref):
    m_prev = m_ref[...]
    m_new = jnp.maximum(m_prev, jnp.max(s, axis=-1, keepdims=True))
    alpha = jnp.exp(m_prev - m_new)
    rep = lambda a, n: a if n == 1 else jnp.concatenate([a] * n, axis=1)
    p = jnp.exp(s - rep(m_new, s.shape[1] // LANES)).astype(BF16)
    acc_ref[...] = rep(alpha, acc_ref.shape[1] // LANES) * acc_ref[...] + pv_fn(p)
    m_ref[...] = m_new


def _pv_pages(vaug_list):
    def fn(p):
        upd = None
        for i, va in enumerate(vaug_list):
            nk = va.shape[0]
            d = _dot(p[:, i * nk:(i + 1) * nk], va)
            upd = d if upd is None else upd + d
        return upd
    return fn


def _with_ones(v):
    return jnp.concatenate([v, jnp.ones((v.shape[0], LANES), v.dtype)], axis=1)


def _dsa_kernel(qi_ref, wi_ref, ki_ref, qa_ref, ka_ref, va_ref, o_ref, skey_ref, m_ref, acc_ref,
                *, tq, ck, ksel):
    i = pl.program_id(1)
    nkc = (i + 1) * (tq // ck)
    lane = lax.broadcasted_iota(I32, (tq, LANES), 1)
    hi_half = lane >= HD
    qi = qi_ref[...]
    w = wi_ref[...]
    zero_b = jnp.zeros((tq, LANES), BF16)
    q_heads = []
    for h in range(H_IDX):
        grp = qi[:, (h // 2) * LANES:(h // 2 + 1) * LANES]
        q_heads.append(jnp.where(hi_half if h % 2 else jnp.logical_not(hi_half), grp, zero_b))
    w_heads = [w[:, h:h + 1] for h in range(H_IDX)]
    rows = i * tq + lax.broadcasted_iota(I32, (tq, ck), 0)
    col_in = lax.broadcasted_iota(I32, (tq, ck), 1)

    def score_body(c, _):
        kc = ki_ref[pl.ds(pl.multiple_of(c * ck, ck), ck), :]
        acc = jnp.zeros((tq, ck), F32)
        for h in range(H_IDX):
            acc = acc + jnp.maximum(_dot_nt(q_heads[h], kc), 0.0) * w_heads[h]
        key = _sortable_key(acc)
        skey_ref[c] = jnp.where(c * ck + col_in <= rows, key, INT_MIN)
        return 0

    lax.fori_loop(0, nkc, score_body, 0)
    thr = _select_topk(skey_ref, nkc, ksel, tq, ck)
    thr = jnp.concatenate([thr] * (ck // LANES), axis=1)

    qa = qa_ref[...].astype(F32)

    def head_at_half(h, half):
        grp = qa[:, (h // 2) * LANES:(h // 2 + 1) * LANES]
        if h % 2 != half:
            grp = pltpu.roll(grp, HD, 1)
        return jnp.where(hi_half if half else jnp.logical_not(hi_half), grp, 0.0)

    g_per = HA // KVA
    outs = []
    for j in range(KVA):
        q_rows = jnp.concatenate([head_at_half(g_per * j + g, j) for g in range(g_per)], axis=0).astype(BF16)
        _flash_init(m_ref, acc_ref)

        def att_body(c, _):
            off = pl.multiple_of(c * ck, ck)
            s = _dot_nt(q_rows, ka_ref[pl.ds(off, ck), :])
            bias = jnp.where(skey_ref[c] >= thr, 0.0, NEG_BIAS)
            s = (s.reshape(g_per, tq, ck) + bias[None]).reshape(g_per * tq, ck)
            _flash_update(s, _pv_pages([_with_ones(va_ref[pl.ds(off, ck), :])]), m_ref, acc_ref)
            return 0

        lax.fori_loop(0, nkc, att_body, 0)
        acc = acc_ref[...]
        outs.append(acc[:, :LANES] / acc[:, LANES:])

    for gg in range(HA // 2):
        j = (2 * gg) // g_per
        a = outs[j][((2 * gg) % g_per) * tq:((2 * gg) % g_per + 1) * tq]
        b = outs[j][((2 * gg + 1) % g_per) * tq:((2 * gg + 1) % g_per + 1) * tq]
        if j == 1:
            a = pltpu.roll(a, HD, 1)
        else:
            b = pltpu.roll(b, HD, 1)
        o_ref[:, gg * LANES:(gg + 1) * LANES] = jnp.where(hi_half, b, a).astype(o_ref.dtype)


def _dsa_prompt(qi, wi, kib, qa, kab, vab, ksel):
    b, t, _ = qi.shape
    tq, ck = Q_BLOCK, K_CHUNK
    qspec = lambda w: pl.BlockSpec((None, tq, w), lambda bb, i: (bb, i, 0))
    kspec = lambda w: pl.BlockSpec((None, t, w), lambda bb, i: (bb, 0, 0))
    g_per = HA // KVA
    return pl.pallas_call(
        functools.partial(_dsa_kernel, tq=tq, ck=ck, ksel=ksel),
        out_shape=jax.ShapeDtypeStruct((b, t, 512), BF16),
        grid=(b, t // tq),
        in_specs=[qspec(512), qspec(128), kspec(128), qspec(512), kspec(128), kspec(128)],
        out_specs=qspec(512),
        scratch_shapes=[pltpu.VMEM((t // ck, tq, ck), I32),
                        pltpu.VMEM((g_per * tq, LANES), F32),
                        pltpu.VMEM((g_per * tq, 2 * LANES), F32)],
        compiler_params=_cparams("arbitrary", "arbitrary"),
        name="dsa_prompt",
    )(qi, wi, kib, qa, kab, vab)


def _lambda_value(lamv, lam_init):
    l1 = jnp.sum(lamv[0:1, :] * lamv[1:2, :], axis=-1, keepdims=True)
    l2 = jnp.sum(lamv[2:3, :] * lamv[3:4, :], axis=-1, keepdims=True)
    return jnp.exp(l1) - jnp.exp(l2) + lam_init


def _sub_norm(o, gsub, lam_init):
    ms = jnp.mean(o * o, axis=-1, keepdims=True)
    return o * lax.rsqrt(ms + EPS) * gsub * (1.0 - lam_init)


def _diff_kernel(qb_ref, kb_ref, vb_ref, lamv_ref, gsub_ref, o_ref, m_ref, acc_ref, *, tq, ck, lam_init):
    i = pl.program_id(1)
    n_full = i * (tq // ck)
    lane = lax.broadcasted_iota(I32, (tq, LANES), 1)
    hi_half = lane >= HD
    qb = qb_ref[...]
    zero_b = jnp.zeros((tq, LANES), BF16)
    lam = _lambda_value(lamv_ref[...], lam_init)
    gsub = gsub_ref[...]
    g_per = HB // KVB
    r_in = lax.broadcasted_iota(I32, (tq, ck), 0)
    c_in = lax.broadcasted_iota(I32, (tq, ck), 1)

    for j in range(KVB):
        parts = []
        for g in range(g_per):
            grp = qb[:, (g_per * j + g) * LANES:(g_per * j + g + 1) * LANES]
            parts.append(jnp.where(jnp.logical_not(hi_half), grp, zero_b))
            parts.append(jnp.where(hi_half, grp, zero_b))
        q_rows = jnp.concatenate(parts, axis=0)
        kcols = slice(j * LANES, (j + 1) * LANES)
        _flash_init(m_ref, acc_ref)

        def step(c, bias):
            off = pl.multiple_of(c * ck, ck)
            s = _dot_nt(q_rows, kb_ref[pl.ds(off, ck), kcols])
            if bias is not None:
                s = (s.reshape(2 * g_per, tq, ck) + bias[None]).reshape(2 * g_per * tq, ck)
            _flash_update(s, _pv_pages([_with_ones(vb_ref[pl.ds(off, ck), kcols])]), m_ref, acc_ref)

        def full_body(c, _):
            step(c, None)
            return 0

        lax.fori_loop(0, n_full, full_body, 0)
        for d in range(tq // ck):
            bias = jnp.where(d * ck + c_in <= r_in, 0.0, NEG_BIAS)
            step(n_full + d, bias)

        acc = acc_ref[...]
        o = acc[:, :LANES] / acc[:, LANES:]
        for g in range(g_per):
            o1 = o[(2 * g) * tq:(2 * g + 1) * tq]
            o2 = o[(2 * g + 1) * tq:(2 * g + 2) * tq]
            hh = g_per * j + g
            o_ref[:, hh * LANES:(hh + 1) * LANES] = _sub_norm(o1 - lam * o2, gsub, lam_init).astype(o_ref.dtype)


def _diff_prompt(qb, kbb, vbb, lamv, gsub, lam_init):
    b, t, _ = qb.shape
    tq, ck = Q_BLOCK, K_CHUNK
    qspec = lambda w: pl.BlockSpec((None, tq, w), lambda bb, i: (bb, i, 0))
    kspec = lambda w: pl.BlockSpec((None, t, w), lambda bb, i: (bb, 0, 0))
    const = lambda shape: pl.BlockSpec(shape, lambda bb, i: (0,) * len(shape))
    rows = 2 * (HB // KVB) * tq
    return pl.pallas_call(
        functools.partial(_diff_kernel, tq=tq, ck=ck, lam_init=lam_init),
        out_shape=jax.ShapeDtypeStruct((b, t, 512), BF16),
        grid=(b, t // tq),
        in_specs=[qspec(512), kspec(256), kspec(256), const((8, LANES)), const((1, LANES))],
        out_specs=qspec(512),
        scratch_shapes=[pltpu.VMEM((rows, LANES), F32), pltpu.VMEM((rows, 2 * LANES), F32)],
        compiler_params=_cparams("arbitrary", "arbitrary"),
        name="diff_prompt",
    )(qb, kbb, vbb, lamv, gsub)


def _head_sum(r):
    r3 = r.reshape(H_IDX, 8, r.shape[1])
    s = r3[0]
    for h in range(1, H_IDX):
        s = s + r3[h]
    return s + 0.0


def _s1_kernel(pt_ref, qm_ref, wcol_ref, knew_ref, *rest, pg, tnew):
    pages = rest[:pg]
    sc_ref, scn_ref = rest[pg], rest[pg + 1]
    qm = qm_ref[...]
    wcol = wcol_ref[...]
    for i in range(pg):
        s = _dot(qm, pages[i][...].astype(BF16))
        sc_ref[:, i * LANES:(i + 1) * LANES] = _head_sum(jnp.maximum(s, 0.0) * wcol)

    @pl.when(pl.program_id(1) == 0)
    def _():
        s = _dot_nt(qm, knew_ref[...])
        sn = _head_sum(jnp.maximum(s, 0.0) * wcol)
        r = lax.broadcasted_iota(I32, sn.shape, 0)
        c = lax.broadcasted_iota(I32, sn.shape, 1)
        scn_ref[...] = jnp.where(jnp.logical_and(c <= r, c < tnew), sn, -jnp.inf)


def _page_specs(rows, pg):
    return [pl.BlockSpec((None, rows, LANES), functools.partial(lambda b, g, pt, i: (pt[b, g * pg + i], 0, 0), i=i))
            for i in range(pg)]


def _sample_scores(page_table, qm, wcol, knew_i, cik, pg, tnew):
    db, n_pages = page_table.shape
    grid_spec = pltpu.PrefetchScalarGridSpec(
        num_scalar_prefetch=1,
        grid=(db, n_pages // pg),
        in_specs=[pl.BlockSpec((None, 64, D_IDX), lambda b, g, pt: (b, 0, 0)),
                  pl.BlockSpec((None, 64, LANES), lambda b, g, pt: (b, 0, 0)),
                  pl.BlockSpec((None, LANES, D_IDX), lambda b, g, pt: (b, 0, 0))]
                 + _page_specs(D_IDX, pg),
        out_specs=[pl.BlockSpec((None, 8, pg * LANES), lambda b, g, pt: (b, 0, g)),
                   pl.BlockSpec((None, 8, LANES), lambda b, g, pt: (b, 0, 0))],
    )
    return pl.pallas_call(
        functools.partial(_s1_kernel, pg=pg, tnew=tnew),
        out_shape=[jax.ShapeDtypeStruct((db, 8, n_pages * LANES), F32),
                   jax.ShapeDtypeStruct((db, 8, LANES), F32)],
        grid_spec=grid_spec,
        compiler_params=_cparams("arbitrary", "arbitrary"),
        name="sample_scores",
    )(page_table, qm, wcol, knew_i, *([cik] * pg))


def _s1b_kernel(sc_ref, o_ref, skey_ref, *, ksel):
    nc, rows, ck = skey_ref.shape

    def conv(c, _):
        x = sc_ref[c]
        skey_ref[c] = jnp.where(x == -jnp.inf, INT_MIN, _sortable_key(x))
        return 0

    lax.fori_loop(0, nc, conv, 0)
    thr = _select_topk(skey_ref, nc, ksel, rows, ck)

    def emit(c, _):
        o_ref[c] = jnp.where(skey_ref[c] >= thr, 0.0, NEG_BIAS)
        return 0

    lax.fori_loop(0, nc, emit, 0)


def _sample_select(sct, ksel):
    nc, r, _ = sct.shape
    rb = min(r, 128)
    spec = pl.BlockSpec((nc, rb, LANES), lambda i: (0, i, 0))
    return pl.pallas_call(
        functools.partial(_s1b_kernel, ksel=ksel),
        out_shape=jax.ShapeDtypeStruct(sct.shape, F32),
        grid=(r // rb,),
        in_specs=[spec],
        out_specs=spec,
        scratch_shapes=[pltpu.VMEM((nc, rb, LANES), I32)],
        compiler_params=_cparams("arbitrary"),
        name="sample_select",
    )(sct)


def _s2_kernel(pt_ref, qa_ref, qb_ref, selp_ref, seln_ref, kan_ref, van_ref, kbn_ref, vbn_ref,
               lamv_ref, gsub_ref, *rest, pg, tnew, lam_init):
    akp, avp, bkp, bvp = (rest[k * pg:(k + 1) * pg] for k in range(4))
    oa_ref, ob_ref, ma_ref, acca_ref, mb_ref, accb_ref = rest[4 * pg:]
    g = pl.program_id(1)
    qa = qa_ref[...]
    qb = qb_ref[...]

    rows_j = qb.shape[0] // KVB
    ones_kd = jnp.ones((LANES, LANES), BF16)

    def pv_b(values_of_kv):
        def fn(p):
            parts = []
            for j in range(KVB):
                upd = None
                for i in range(p.shape[1] // LANES):
                    va = jnp.concatenate([values_of_kv(i, j), ones_kd], axis=1)
                    d = _dot(p[j * rows_j:(j + 1) * rows_j, i * LANES:(i + 1) * LANES], va)
                    upd = d if upd is None else upd + d
                parts.append(upd)
            return jnp.concatenate(parts, axis=0)
        return fn

    @pl.when(g == 0)
    def _():
        _flash_init(ma_ref, acca_ref)
        _flash_init(mb_ref, accb_ref)
        sa = _dot_nt(qa, kan_ref[...]) + jnp.tile(seln_ref[...], (HA, 1))
        _flash_update(sa, _pv_pages([_with_ones(van_ref[...])]), ma_ref, acca_ref)
        r = lax.broadcasted_iota(I32, (8, LANES), 0)
        c = lax.broadcasted_iota(I32, (8, LANES), 1)
        causal = jnp.where(jnp.logical_and(c <= r, c < tnew), 0.0, NEG_BIAS)
        sb = _dot_nt(qb, kbn_ref[...]) + jnp.tile(causal, (8, 1))
        _flash_update(sb, pv_b(lambda i, j: vbn_ref[:, j * LANES:(j + 1) * LANES]), mb_ref, accb_ref)

    sa = jnp.concatenate([_dot(qa, akp[i][...].astype(BF16)) for i in range(pg)], axis=1)
    sa = sa + jnp.tile(selp_ref[...], (HA, 1))

    def pv_a(p):
        upd = None
        for i in range(pg):
            vt = jnp.concatenate([avp[i][...].astype(BF16), ones_kd], axis=0)
            d = _dot_nt(p[:, i * LANES:(i + 1) * LANES], vt)
            upd = d if upd is None else upd + d
        return upd

    _flash_update(sa, pv_a, ma_ref, acca_ref)
    sb = jnp.concatenate([_dot(qb, bkp[i][...].astype(BF16)) for i in range(pg)], axis=1)
    _flash_update(sb, pv_b(lambda i, j: bvp[i][pl.ds(j, LANES, stride=KVB), :].astype(BF16)), mb_ref, accb_ref)

    @pl.when(g == pl.num_programs(1) - 1)
    def _():
        acca = acca_ref[...]
        oa_ref[...] = acca[:, :LANES] / acca[:, LANES:]
        accb = accb_ref[...]
        x = (accb[:, :LANES] / accb[:, LANES:]).reshape(HB, 2, 8, LANES)
        lam = _lambda_value(lamv_ref[...], lam_init)
        gsub = gsub_ref[...]
        for hh in range(HB):
            ob_ref[:, hh * LANES:(hh + 1) * LANES] = _sub_norm(x[hh, 0] - lam * x[hh, 1], gsub, lam_init)


def _sample_attend(page_table, qa_rows, qb_rows, selp, seln, kan, van, kbn, vbn, lamv, gsub,
                   cak, cav, cbk, cbv, pg, tnew, lam_init):
    db, n_pages = page_table.shape
    per_b = lambda *shape: pl.BlockSpec((None,) + shape, lambda b, g, pt: (b,) + (0,) * len(shape))
    const = lambda shape: pl.BlockSpec(shape, lambda b, g, pt: (0,) * len(shape))
    grid_spec = pltpu.PrefetchScalarGridSpec(
        num_scalar_prefetch=1,
        grid=(db, n_pages // pg),
        in_specs=[per_b(64, 128), per_b(64, 256),
                  pl.BlockSpec((None, 8, pg * LANES), lambda b, g, pt: (b, 0, g)), per_b(8, LANES),
                  per_b(LANES, 128), per_b(LANES, 128), per_b(LANES, 256), per_b(LANES, 256),
                  const((8, LANES)), const((1, LANES))]
                 + _page_specs(128, pg) + _page_specs(128, pg)
                 + _page_specs(256, pg) + _page_specs(256, pg),
        out_specs=[per_b(64, 128), per_b(8, 512)],
        scratch_shapes=[pltpu.VMEM((64, LANES), F32), pltpu.VMEM((64, 2 * LANES), F32),
                        pltpu.VMEM((64, LANES), F32), pltpu.VMEM((64, 2 * LANES), F32)],
    )
    return pl.pallas_call(
        functools.partial(_s2_kernel, pg=pg, tnew=tnew, lam_init=lam_init),
        out_shape=[jax.ShapeDtypeStruct((db, 64, 128), F32), jax.ShapeDtypeStruct((db, 8, 512), F32)],
        grid_spec=grid_spec,
        compiler_params=_cparams("arbitrary", "arbitrary"),
        name="sample_attend",
    )(page_table, qa_rows, qb_rows, selp, seln, kan, van, kbn, vbn, lamv, gsub,
      *([cak] * pg), *([cav] * pg), *([cbk] * pg), *([cbv] * pg))


TOKEN_ROWS = 8


def _outproj_kernel(oa_ref, ob_ref, x_ref, gt_ref, sh_ref, sc_ref, g_ref, wo_ref, wr_ref, br_ref,
                    h_ref, xm_ref, comb_ref, sel_ref):
    half = oa_ref.shape[1]
    tm = x_ref.shape[0]
    o = _dot(oa_ref[...].astype(BF16), wo_ref[:half, :]) + _dot(ob_ref[...].astype(BF16), wo_ref[half:, :])
    h = x_ref[...] + gt_ref[...] * o
    h_ref[...] = h
    ms = jnp.mean(h * h, axis=-1, keepdims=True)
    xm = h * lax.rsqrt(ms + EPS) * g_ref[...]
    xm = xm * (1.0 + sc_ref[...]) + sh_ref[...]
    for c in range(TOKEN_ROWS):
        xm_ref[pl.ds(c, tm, stride=TOKEN_ROWS), :] = xm[:, c * LANES:(c + 1) * LANES]
    logits = _dot3(xm, wr_ref[...]) + br_ref[...]
    lane = lax.broadcasted_iota(I32, logits.shape, 1).astype(F32)
    vals, hots = [], []
    for _ in range(TOP_K):
        m = jnp.max(logits, axis=-1, keepdims=True)
        idx = jnp.min(jnp.where(logits == m, lane, float(LANES)), axis=-1, keepdims=True)
        hot = lane == idx
        vals.append(m)
        hots.append(hot)
        logits = jnp.where(hot, -3e38, logits)
    es = [jnp.exp(v - vals[0]) for v in vals]
    den = es[0] + es[1] + es[2] + es[3]
    comb = jnp.zeros(logits.shape, F32)
    sel = jnp.zeros(logits.shape, F32)
    for e, hot in zip(es, hots):
        comb = comb + jnp.where(hot, e / den, 0.0)
        sel = sel + jnp.where(hot, 1.0, 0.0)
    comb_ref[...] = comb
    sel_ref[...] = sel.astype(BF16)


def _outproj(oa, ob, x2, gt, sh, sc, g_ffn, w_out_b, w_router_p, b_router_p, tm, steps_per_seq):
    n, d = x2.shape
    assert d == TOKEN_ROWS * LANES
    const = lambda shape: pl.BlockSpec(shape, lambda i: (0,) * len(shape))
    row = lambda w: pl.BlockSpec((tm, w), lambda i: (i, 0))
    return pl.pallas_call(
        _outproj_kernel,
        out_shape=[jax.ShapeDtypeStruct((n, d), F32), jax.ShapeDtypeStruct((n * TOKEN_ROWS, LANES), F32),
                   jax.ShapeDtypeStruct((n, LANES), F32), jax.ShapeDtypeStruct((n, LANES), BF16)],
        grid=(n // tm,),
        in_specs=[row(oa.shape[1]), row(ob.shape[1]), row(d),
                  _mod_spec(gt, tm, steps_per_seq), _mod_spec(sh, tm, steps_per_seq),
                  _mod_spec(sc, tm, steps_per_seq),
                  const((1, d)), const(w_out_b.shape), const((d, LANES)), const((1, LANES))],
        out_specs=[row(d), pl.BlockSpec((tm * TOKEN_ROWS, LANES), lambda i: (i, 0)), row(LANES), row(LANES)],
        compiler_params=_cparams("arbitrary"),
        name="outproj",
    )(oa, ob, x2, gt, sh, sc, g_ffn.reshape(1, d), w_out_b, w_router_p, b_router_p)


MOE_TILE = 256
WAIT_UNROLL = 32


def _rank_kernel(sel_ref, tri_ref, rank_ref, cnt_ref, carry_ref):
    @pl.when(pl.program_id(0) == 0)
    def _():
        carry_ref[...] = jnp.zeros(carry_ref.shape, F32)

    sel = sel_ref[...]
    tm = sel.shape[0]
    before = _dot(tri_ref[...], sel)
    carry = carry_ref[...]
    rank_ref[...] = before + carry[0:1, :]
    carry = carry + (before[tm - 1:tm, :] + sel[tm - 1:tm, :].astype(F32))
    carry_ref[...] = carry
    cnt_ref[...] = carry


def _moe_rank(sel, tm):
    n = sel.shape[0]
    r = lax.broadcasted_iota(I32, (tm, tm), 0)
    c = lax.broadcasted_iota(I32, (tm, tm), 1)
    tri = jnp.where(c < r, 1.0, 0.0).astype(BF16)
    return pl.pallas_call(
        _rank_kernel,
        out_shape=[jax.ShapeDtypeStruct((n, LANES), F32), jax.ShapeDtypeStruct((8, LANES), F32)],
        grid=(n // tm,),
        in_specs=[pl.BlockSpec((tm, LANES), lambda i: (i, 0)), pl.BlockSpec((tm, tm), lambda i: (0, 0))],
        out_specs=[pl.BlockSpec((tm, LANES), lambda i: (i, 0)), pl.BlockSpec((8, LANES), lambda i: (0, 0))],
        scratch_shapes=[pltpu.VMEM((8, LANES), F32)],
        compiler_params=_cparams("arbitrary"),
        name="moe_rank",
    )(sel, tri)


def _slots_kernel(rank_ref, sel_ref, comb_ref, starts_ref, slot_ref, gate_ref):
    slotv = rank_ref[...] + starts_ref[...]
    comb = comb_ref[...]
    sel = sel_ref[...].astype(F32) > 0.0
    lane = lax.broadcasted_iota(I32, slotv.shape, 1)
    lanef = lane.astype(F32)
    slots = jnp.zeros(slotv.shape, F32)
    gates = jnp.zeros(slotv.shape, F32)
    for k in range(TOP_K):
        idx = jnp.min(jnp.where(sel, lanef, float(LANES)), axis=-1, keepdims=True)
        hot = lanef == idx
        sk = jnp.sum(jnp.where(hot, slotv, 0.0), axis=-1, keepdims=True)
        gk = jnp.sum(jnp.where(hot, comb, 0.0), axis=-1, keepdims=True)
        slots = jnp.where(lane == k, sk, slots)
        gates = jnp.where(lane == k, gk, gates)
        sel = jnp.logical_and(sel, jnp.logical_not(hot))
    slot_ref[...] = slots.astype(I32)
    gate_ref[...] = gates


def _moe_slots(rank, sel, comb, starts, tm):
    n = rank.shape[0]
    row = pl.BlockSpec((tm, LANES), lambda i: (i, 0))
    return pl.pallas_call(
        _slots_kernel,
        out_shape=[jax.ShapeDtypeStruct((n, LANES), I32), jax.ShapeDtypeStruct((n, LANES), F32)],
        grid=(n // tm,),
        in_specs=[row, row, row, pl.BlockSpec((1, LANES), lambda i: (0, 0))],
        out_specs=[row, row],
        compiler_params=_cparams("arbitrary"),
        name="moe_slots",
    )(rank, sel, comb, starts)


def _slab(ref, index):
    if isinstance(index, int):
        return ref.at[pl.ds(index * TOKEN_ROWS, TOKEN_ROWS), :]
    return ref.at[pl.ds(pl.multiple_of(index * TOKEN_ROWS, TOKEN_ROWS), TOKEN_ROWS), :]


def _wait_all(copy, count):
    def body(_, carry):
        for _u in range(WAIT_UNROLL):
            copy.wait()
        return carry
    lax.fori_loop(0, count // WAIT_UNROLL, body, 0)
    for _u in range(count % WAIT_UNROLL):
        copy.wait()


def _dispatch_kernel(slots_ref, x_ref, xs_ref, sem):
    tm = x_ref.shape[0] // TOKEN_ROWS

    def body(t, carry):
        src = _slab(x_ref, t)
        for k in range(TOP_K):
            pltpu.make_async_copy(src, _slab(xs_ref, slots_ref[t * TOP_K + k]), sem).start()
        return carry

    lax.fori_loop(0, tm, body, 0)
    _wait_all(pltpu.make_async_copy(_slab(x_ref, 0), _slab(xs_ref, 0), sem), tm * TOP_K)


def _moe_dispatch(slots_flat, xm_slabs, tm):
    n = xm_slabs.shape[0] // TOKEN_ROWS
    return pl.pallas_call(
        _dispatch_kernel,
        out_shape=jax.ShapeDtypeStruct((n * TOP_K * TOKEN_ROWS, LANES), F32),
        grid=(n // tm,),
        in_specs=[pl.BlockSpec((tm * TOP_K,), lambda i: (i,), memory_space=pltpu.SMEM),
                  pl.BlockSpec((tm * TOKEN_ROWS, LANES), lambda i: (i, 0))],
        out_specs=pl.BlockSpec(memory_space=pl.ANY),
        scratch_shapes=[pltpu.SemaphoreType.DMA],
        compiler_params=_cparams("arbitrary"),
        name="moe_dispatch",
    )(slots_flat, xm_slabs)


def _ffn_kernel(tile_ref, exp_ref, lo_ref, hi_ref, xs_ref, wgu_ref, bgu_ref, wd_ref, bd_ref, ys_ref):
    g = pl.program_id(0)
    lo, hi = lo_ref[g], hi_ref[g]
    tme = xs_ref.shape[0] // TOKEN_ROWS
    dff = wd_ref.shape[0]

    @pl.when(hi > lo)
    def _():
        x = jnp.concatenate([xs_ref[pl.ds(c, tme, stride=TOKEN_ROWS), :] for c in range(TOKEN_ROWS)], axis=1)
        hgu = _dot(x.astype(BF16), wgu_ref[...]) + bgu_ref[...]
        gate = jnp.minimum(hgu[:, :dff], SWIGLU_LIMIT)
        up = jnp.clip(hgu[:, dff:], -SWIGLU_LIMIT, SWIGLU_LIMIT)
        a = (up + 1.0) * (gate * jax.nn.sigmoid(SWIGLU_ALPHA * gate))
        yo = _dot(a.astype(BF16), wd_ref[...]) + bd_ref[...]
        row = lax.broadcasted_iota(I32, (tme, LANES), 0)
        mine = jnp.logical_and(row >= lo, row < hi)

        @pl.when(lo == 0)
        def _():
            for c in range(TOKEN_ROWS):
                ys_ref[pl.ds(c, tme, stride=TOKEN_ROWS), :] = yo[:, c * LANES:(c + 1) * LANES]

        @pl.when(lo != 0)
        def _():
            for c in range(TOKEN_ROWS):
                dst = ys_ref.at[pl.ds(c, tme, stride=TOKEN_ROWS), :]
                dst[...] = jnp.where(mine, yo[:, c * LANES:(c + 1) * LANES], dst[...])


def _moe_plan(counts, n_slots, tme):
    cnt = counts[:N_EXPERTS].astype(I32)
    ends = jnp.cumsum(cnt)
    starts = ends - cnt
    first_tile = starts // tme
    n_items = jnp.where(cnt > 0, (ends - 1) // tme - first_tile + 1, 0)
    item_end = jnp.cumsum(n_items)
    item_start = item_end - n_items
    n_work = n_slots // tme + N_EXPERTS - 1
    g = jnp.arange(n_work, dtype=I32)
    gi = jnp.minimum(g, item_end[-1] - 1)
    e = jnp.searchsorted(item_end, gi, side="right").astype(I32)
    tile = first_tile[e] + (gi - item_start[e])
    lo = jnp.maximum(starts[e], tile * tme) - tile * tme
    hi = jnp.minimum(ends[e], (tile + 1) * tme) - tile * tme
    real = g < item_end[-1]
    return tile, e, jnp.where(real, lo, 0), jnp.where(real, hi, 0), starts


def _moe_ffn(plan, xs, wgu_b, b_gu, wd_b, b_down, tme):
    tile, e, lo, hi = plan
    ne, d, dff2 = wgu_b.shape
    dff = dff2 // 2
    slab_spec = pl.BlockSpec((tme * TOKEN_ROWS, LANES), lambda g, tile, e, lo, hi: (tile[g], 0))
    per_e = lambda *shape: pl.BlockSpec((None,) + shape, lambda g, tile, e, lo, hi: (e[g],) + (0,) * len(shape))
    grid_spec = pltpu.PrefetchScalarGridSpec(
        num_scalar_prefetch=4,
        grid=(tile.shape[0],),
        in_specs=[slab_spec, per_e(d, dff2), per_e(1, dff2), per_e(dff, d), per_e(1, d)],
        out_specs=slab_spec,
    )
    return pl.pallas_call(
        _ffn_kernel,
        out_shape=jax.ShapeDtypeStruct(xs.shape, F32),
        grid_spec=grid_spec,
        compiler_params=_cparams("arbitrary"),
        name="moe_ffn",
    )(tile, e, lo, hi, xs, wgu_b, b_gu.reshape(ne, 1, dff2), wd_b, b_down.reshape(ne, 1, d))


def _combine_kernel(slots_ref, gate_ref, h_ref, gt_ref, ys_ref, y_ref, buf_ref, sem):
    tm = h_ref.shape[0]

    def body(t, carry):
        for k in range(TOP_K):
            pltpu.make_async_copy(_slab(ys_ref, slots_ref[t * TOP_K + k]),
                                  _slab(buf_ref, t * TOP_K + k), sem).start()
        return carry

    lax.fori_loop(0, tm, body, 0)
    _wait_all(pltpu.make_async_copy(_slab(ys_ref, 0), _slab(buf_ref, 0), sem), tm * TOP_K)
    gate = gate_ref[...]
    gt = gt_ref[...]
    for c in range(TOKEN_ROWS):
        acc = None
        for k in range(TOP_K):
            part = buf_ref[pl.ds(k * TOKEN_ROWS + c, tm, stride=TOP_K * TOKEN_ROWS), :] * gate[:, k:k + 1]
            acc = part if acc is None else acc + part
        sl = slice(c * LANES, (c + 1) * LANES)
        y_ref[:, sl] = h_ref[:, sl] + gt[:, sl] * acc


def _moe_combine(slots_flat, gate4, h, gt, ys, tm, steps_per_seq):
    n, d = h.shape
    return pl.pallas_call(
        _combine_kernel,
        out_shape=jax.ShapeDtypeStruct((n, d), F32),
        grid=(n // tm,),
        in_specs=[pl.BlockSpec((tm * TOP_K,), lambda i: (i,), memory_space=pltpu.SMEM),
                  pl.BlockSpec((tm, LANES), lambda i: (i, 0)), pl.BlockSpec((tm, d), lambda i: (i, 0)),
                  _mod_spec(gt, tm, steps_per_seq), pl.BlockSpec(memory_space=pl.ANY)],
        out_specs=pl.BlockSpec((tm, d), lambda i: (i, 0)),
        scratch_shapes=[pltpu.VMEM((tm * TOP_K * TOKEN_ROWS, LANES), F32), pltpu.SemaphoreType.DMA],
        compiler_params=_cparams("arbitrary"),
        name="moe_combine",
    )(slots_flat, gate4, h, gt, ys)


def _moe(xm_slabs, comb, sel, h, gt, wgu_b, b_gu, wd_b, b_down, tokens_per_seq):
    n, d = h.shape
    tm = min(MOE_TILE, _token_tile(n))
    tme = min(512, _token_tile(n * TOP_K))
    rank, counts = _moe_rank(sel, _token_tile(n))
    tile, e, lo, hi, starts = _moe_plan(counts[0], n * TOP_K, tme)
    starts_row = jnp.zeros((1, LANES), F32).at[0, :N_EXPERTS].set(starts.astype(F32))
    slot4, gate4 = _moe_slots(rank, sel, comb, starts_row, _token_tile(n))
    slots_flat = slot4[:, :TOP_K].reshape(n * TOP_K)
    xs = _moe_dispatch(slots_flat, xm_slabs, tm)
    ys = _moe_ffn((tile, e, lo, hi), xs, wgu_b, b_gu, wd_b, b_down, tme)
    return _moe_combine(slots_flat, gate4, h, gt, ys, tm, max(1, tokens_per_seq // tm))


def _token_tile(n):
    for tm in (512, 256, 128, 64, 32, 16, 8):
        if n % tm == 0:
            return tm
    raise ValueError(f"token count {n} must be a multiple of 8")


def kernel(x_prompt, x_sample, cache_a_k, cache_a_v, cache_idx_k, cache_b_k, cache_b_v, page_table,
           c_prompt, c_sample, w_ada, b_ada, g_attn, w_in, g_qa, g_ka, g_ki, g_qb, g_kb,
           lam_q1, lam_k1, lam_q2, lam_k2, g_sub, w_out, g_ffn, w_router, b_router,
           w_gu, b_gu, w_down, b_down):
    bsz, t_p, d = x_prompt.shape
    db, t_s, _ = x_sample.shape
    depth = w_in.shape[0]
    n_pool, page = cache_a_k.shape[1], cache_a_k.shape[2]
    n_pages = page_table.shape[1]
    past = n_pages * page
    assert page == LANES and t_s == 8 and t_p % Q_BLOCK == 0 and d == 1024
    n_p, n_s = bsz * t_p, db * t_s
    pg = math.gcd(n_pages, 8)

    cs_p, sn_p = _rope_tables(jnp.arange(t_p, dtype=I32))
    cs_s, sn_s = _rope_tables(past + jnp.arange(t_s, dtype=I32))
    cs_s, sn_s = jnp.tile(cs_s, (db, 1)), jnp.tile(sn_s, (db, 1))
    tile2 = lambda g: jnp.concatenate([g, g]).astype(F32)
    ksel_p = min(TOPK_MAX, t_p // 4)
    ksel_s = min(TOPK_MAX, (past + t_s) // 4)
    tm_p, tm_s = _token_tile(t_p), _token_tile(n_s)
    sps_p = t_p // tm_p
    per_tok = lambda m: jnp.repeat(m, t_s, axis=0)

    hp = x_prompt.reshape(n_p, d)
    hs = x_sample.reshape(n_s, d)
    rows_p = [[] for _ in range(5)]
    rows_s = [[] for _ in range(5)]
    for l in range(depth):
        lam_init = 0.8 - 0.6 * math.exp(-0.3 * l)
        mods = _adaln(jnp.concatenate([c_prompt, c_sample], axis=0), w_ada[l], b_ada[l])
        mp = [m.reshape(bsz, 1, d) for m in jnp.split(mods[:bsz], 6, axis=-1)]
        ms_ = [per_tok(m) for m in jnp.split(mods[bsz:], 6, axis=-1)]
        w_perm = _permute_w_in(w_in[l])
        gains = jnp.zeros((8, LANES), F32).at[:5].set(
            jnp.stack([tile2(g_qa[l]), tile2(g_ka[l]), tile2(g_ki[l]), tile2(g_qb[l]), tile2(g_kb[l])]))
        lamv = jnp.zeros((8, LANES), F32).at[:4, :HD].set(
            jnp.stack([lam_q1[l], lam_k1[l], lam_q2[l], lam_k2[l]]).astype(F32))
        gsub = g_sub[l].reshape(1, VB).astype(F32)
        w_out_b = w_out[l].astype(BF16)
        w_router_p = jnp.zeros((d, LANES), F32).at[:, :N_EXPERTS].set(w_router[l])
        b_router_p = jnp.full((1, LANES), NEG_BIAS, F32).at[0, :N_EXPERTS].set(b_router[l])
        wgu_b = w_gu[l].astype(BF16)
        wd_b = w_down[l].astype(BF16)

        (qa, qi, qb, ka, kb, ki, wi, va, vb, kab, kbb, kib, vab, vbb) = _proj(
            hp, mp[0], mp[1], g_attn[l], w_perm, cs_p, sn_p, gains, tm_p, sps_p)
        r3 = lambda a: a.reshape(bsz, t_p, a.shape[-1])
        oa = _dsa_prompt(r3(qi), r3(wi), r3(kib), r3(qa), r3(kab), r3(vab), ksel_p)
        ob = _diff_prompt(r3(qb), r3(kbb), r3(vbb), lamv, gsub, lam_init)
        h1, xm, comb, sel = _outproj(oa.reshape(n_p, 512), ob.reshape(n_p, 512), hp, mp[2], mp[3], mp[4],
                                     g_ffn[l], w_out_b, w_router_p, b_router_p, tm_p, sps_p)
        hp = _moe(xm, comb, sel, h1, mp[5], wgu_b, b_gu[l], wd_b, b_down[l], t_p)
        for lst, r in zip(rows_p, (ka.reshape(bsz, t_p, KVA, HD), va.reshape(bsz, t_p, KVA, HD),
                                   ki.reshape(bsz, t_p, D_IDX), kb.reshape(bsz, t_p, KVB, 2, HD),
                                   vb.reshape(bsz, t_p, KVB, VB))):
            lst.append(r)

        (qa, qi, qb, ka, kb, ki, wi, va, vb, kab, kbb, kib, vab, vbb) = _proj(
            hs, ms_[0], ms_[1], g_attn[l], w_perm, cs_s, sn_s, gains, tm_s, 1)
        qm = qi.reshape(db, t_s, H_IDX, D_IDX).transpose(0, 2, 1, 3).reshape(db, 64, D_IDX)
        wcol = jnp.broadcast_to(
            wi[:, :H_IDX].reshape(db, t_s, H_IDX).transpose(0, 2, 1).reshape(db, 64, 1), (db, 64, LANES))
        pad_new = lambda a: jnp.pad(a.reshape(db, t_s, a.shape[-1]), ((0, 0), (0, LANES - t_s), (0, 0)))
        idx_t = jnp.transpose(cache_idx_k[l], (0, 2, 1))
        ak_t = jnp.transpose(cache_a_k[l], (0, 2, 3, 1)).reshape(n_pool, KVA * HD, page)
        av_t = jnp.transpose(cache_a_v[l], (0, 2, 3, 1)).reshape(n_pool, KVA * HD, page)
        bk_t = jnp.transpose(cache_b_k[l], (0, 2, 3, 4, 1)).reshape(n_pool, KVB * 2 * HD, page)
        bv_r = cache_b_v[l].reshape(n_pool, page * KVB, VB)
        sc_past, sc_new = _sample_scores(page_table, qm, wcol, pad_new(kib[:, :D_IDX]), idx_t, pg, t_s)
        sct = jnp.concatenate([sc_past, sc_new], axis=-1).reshape(n_s, n_pages + 1, LANES).transpose(1, 0, 2)
        sel = _sample_select(sct, ksel_s).transpose(1, 0, 2).reshape(db, t_s, (n_pages + 1) * LANES)
        qa4 = qa.reshape(db, t_s, HA, HD).transpose(0, 2, 1, 3)
        zq = jnp.zeros_like(qa4[:, :HA // 2])
        qa_rows = jnp.concatenate([jnp.concatenate([qa4[:, :HA // 2], zq], axis=-1),
                                   jnp.concatenate([zq, qa4[:, HA // 2:]], axis=-1)], axis=1).reshape(db, 64, 128)
        qb5 = qb.reshape(db, t_s, HB, 2, HD).transpose(0, 2, 3, 1, 4)
        zb = jnp.zeros_like(qb5[:, 0, 0])
        blocks = []
        for hh in range(HB):
            for m in range(2):
                pos = (hh // (HB // KVB)) * 2 + m
                blocks.append(jnp.concatenate([qb5[:, hh, m] if p == pos else zb for p in range(4)], axis=-1))
        qb_rows = jnp.stack(blocks, axis=1).reshape(db, 64, 256)
        oa_rows, ob = _sample_attend(
            page_table, qa_rows, qb_rows, sel[:, :, :past], sel[:, :, past:],
            pad_new(kab), pad_new(vab), pad_new(kbb), pad_new(vbb), lamv, gsub,
            ak_t, av_t, bk_t, bv_r, pg, t_s, lam_init)
        oa5 = oa_rows.reshape(db, HA, t_s, KVA, HD)
        oa = jnp.concatenate([oa5[:, :HA // 2, :, 0], oa5[:, HA // 2:, :, 1]], axis=1)
        oa = oa.transpose(0, 2, 1, 3).reshape(n_s, HA * HD)
        h1, xm, comb, sel = _outproj(oa, ob.reshape(n_s, 512), hs, ms_[2], ms_[3], ms_[4],
                                     g_ffn[l], w_out_b, w_router_p, b_router_p, tm_s, 1)
        hs = _moe(xm, comb, sel, h1, ms_[5], wgu_b, b_gu[l], wd_b, b_down[l], t_s)
        for lst, r in zip(rows_s, (ka.reshape(db, t_s, KVA, HD), va.reshape(db, t_s, KVA, HD),
                                   ki.reshape(db, t_s, D_IDX), kb.reshape(db, t_s, KVB, 2, HD),
                                   vb.reshape(db, t_s, KVB, VB))):
            lst.append(r)

    outs_p = [jnp.stack(r) for r in rows_p]
    outs_s = [jnp.stack(r) for r in rows_s]
    return (hp.reshape(bsz, t_p, d), hs.reshape(db, t_s, d), *outs_p, *outs_s)
```

```python
import functools
import math

import jax
import jax.numpy as jnp
from jax import lax
from jax.experimental import pallas as pl
from jax.experimental.pallas import tpu as pltpu

F32 = jnp.float32
BF16 = jnp.bfloat16
I32 = jnp.int32

HD = 64
HA = 8
KVA = 2
H_IDX = 8
D_IDX = 64
HB = 4
KVB = 2
VB = 128
N_EXPERTS = 32
TOP_K = 4
TOPK_MAX = 256
SWIGLU_LIMIT = 7.0
SWIGLU_ALPHA = 1.702
ROPE_THETA = 10000.0
EPS = 1e-6
Q_BLOCK = 256
K_CHUNK = 256
LANES = 128
INT_MIN = -2 ** 31
NEG_BIAS = -1e30
M_INIT = -1e20
VMEM_LIMIT = 56 * 1024 * 1024
Q_SCALE = HD ** -0.5 * math.log2(math.e)

C_QA, C_QI, C_QB, C_KA, C_KB, C_KI, C_WI, C_VA, C_VB, C_END = (
    0, 512, 1024, 1536, 1664, 1920, 2048, 2176, 2304, 2560)


def _dot(a, b):
    return jnp.dot(a, b, preferred_element_type=F32)


def _dot_nt(a, b):
    return lax.dot_general(a, b, (((1,), (1,)), ((), ())), preferred_element_type=F32)


def _split_bf16(a):
    hi = a.astype(BF16)
    lo = (a - hi.astype(F32)).astype(BF16)
    return hi, lo


def _dot3(a, b):
    ah, al = _split_bf16(a)
    bh, bl = _split_bf16(b)
    return _dot(ah, bh) + (_dot(ah, bl) + _dot(al, bh))


def _cparams(*sem):
    return pltpu.CompilerParams(dimension_semantics=sem, vmem_limit_bytes=VMEM_LIMIT)


def _adaln_kernel(c_ref, w_ref, b_ref, o_ref):
    c = c_ref[...]
    s = c * jax.nn.sigmoid(c)
    o_ref[...] = _dot3(s, w_ref[...]) + b_ref[...]


def _adaln(c, w_ada, b_ada):
    n, d = c.shape
    e = w_ada.shape[1]
    tn = 1024
    return pl.pallas_call(
        _adaln_kernel,
        out_shape=jax.ShapeDtypeStruct((n, e), F32),
        grid=(e // tn,),
        in_specs=[pl.BlockSpec((n, d), lambda j: (0, 0)),
                  pl.BlockSpec((d, tn), lambda j: (0, j)),
                  pl.BlockSpec((1, tn), lambda j: (0, j))],
        out_specs=pl.BlockSpec((n, tn), lambda j: (0, j)),
        compiler_params=_cparams("arbitrary"),
        name="adaln",
    )(c, w_ada, b_ada.reshape(1, e))


def _rot_half(v):
    lane = lax.broadcasted_iota(I32, v.shape, 1)
    return jnp.where((lane % HD) < HD // 2, pltpu.roll(v, LANES - HD // 2, 1), pltpu.roll(v, HD // 2, 1))


def _norm_rope(z, c0, width, gain, pn, cs, sn, out_scale):
    outs = []
    for g in range(width // LANES):
        v = z[:, c0 + g * LANES:c0 + (g + 1) * LANES]
        if gain is not None:
            ms = _dot((v * v).astype(BF16), pn)
            v = v * lax.rsqrt(ms + EPS) * gain
        v = v * cs + _rot_half(v) * sn
        if out_scale != 1.0:
            v = v * out_scale
        outs.append(v)
    return outs


def _proj_kernel(x_ref, sh_ref, sc_ref, g_ref, w_ref, cs_ref, sn_ref, gains_ref, pn_ref,
                 qa_ref, qi_ref, qb_ref, ka_ref, kb_ref, ki_ref, wi_ref, va_ref, vb_ref,
                 kab_ref, kbb_ref, kib_ref, vab_ref, vbb_ref, *, token_minor):
    tm = x_ref.shape[0]

    def put(ref, r0, v):
        if token_minor:
            ref[r0:r0 + v.shape[1], :] = v.T
        else:
            ref[:, r0:r0 + v.shape[1]] = v

    x = x_ref[...]
    ms = jnp.mean(x * x, axis=-1, keepdims=True)
    xn = x * lax.rsqrt(ms + EPS) * g_ref[...]
    xn = xn * (1.0 + sc_ref[...]) + sh_ref[...]
    z = _dot(xn.astype(BF16), w_ref[...])
    cs = cs_ref[...]
    sn = sn_ref[...]
    pn = pn_ref[...]
    gains = gains_ref[...]
    g_qa, g_ka, g_ki, g_qb, g_kb = (gains[r:r + 1, :] for r in range(5))

    qa = _norm_rope(z, C_QA, 512, g_qa, pn, cs, sn, Q_SCALE)
    qi = _norm_rope(z, C_QI, 512, None, pn, cs, sn, 1.0)
    qb = _norm_rope(z, C_QB, 512, g_qb, pn, cs, sn, Q_SCALE)
    for g in range(4):
        sl = slice(g * LANES, (g + 1) * LANES)
        qa_ref[:, sl] = qa[g].astype(BF16)
        qi_ref[:, sl] = qi[g].astype(BF16)
        qb_ref[:, sl] = qb[g].astype(BF16)
    ka = _norm_rope(z, C_KA, 128, g_ka, pn, cs, sn, 1.0)[0]
    put(ka_ref, 0, ka)
    kab_ref[...] = ka.astype(BF16)
    kb = _norm_rope(z, C_KB, 256, g_kb, pn, cs, sn, 1.0)
    for g in range(2):
        put(kb_ref, g * LANES, kb[g])
        kbb_ref[:, g * LANES:(g + 1) * LANES] = kb[g].astype(BF16)
    ki = _norm_rope(z, C_KI, 128, g_ki, pn, cs, sn, 1.0)[0]
    if token_minor:
        ki_ref[...] = ki.T[:D_IDX, :]
    else:
        ki_ref[...] = ki[:, :D_IDX]
    kib_ref[...] = ki.astype(BF16)
    wi_ref[...] = z[:, C_WI:C_WI + LANES] * (H_IDX ** -0.5 * D_IDX ** -0.5)
    va = z[:, C_VA:C_VA + 128]
    put(va_ref, 0, va)
    vab_ref[...] = va.astype(BF16)
    vb = z[:, C_VB:C_VB + 256]
    if token_minor:
        for j in range(KVB):
            vb_ref[pl.ds(j, tm, stride=KVB), :] = vb[:, j * VB:(j + 1) * VB]
    else:
        vb_ref[...] = vb
    vbb_ref[...] = vb.astype(BF16)


def _permute_w_in(w_in):
    d = w_in.shape[0]
    seg = lambda a, b: w_in[:, a:b]
    cols = [seg(0, 512), seg(768, 1280), seg(1352, 1864), seg(512, 640), seg(1864, 2120),
            seg(1280, 1344), seg(1280, 1344), seg(1344, 1352), jnp.zeros((d, LANES - H_IDX), w_in.dtype),
            seg(640, 768), seg(2120, 2376)]
    return jnp.concatenate(cols, axis=1).astype(BF16)


def _rope_tables(pos):
    half = HD // 2
    inv = ROPE_THETA ** (-jnp.arange(half, dtype=F32) / half)
    ang = pos.astype(F32)[:, None] * inv[None, :]
    cos, sin = jnp.cos(ang), jnp.sin(ang)
    cs = jnp.concatenate([cos, cos, cos, cos], axis=1)
    sn = jnp.concatenate([-sin, sin, -sin, sin], axis=1)
    return cs, sn


def _head_mean_matrix():
    r = lax.broadcasted_iota(I32, (LANES, LANES), 0) // HD
    c = lax.broadcasted_iota(I32, (LANES, LANES), 1) // HD
    return jnp.where(r == c, 1.0 / HD, 0.0).astype(BF16)


def _mod_spec(arr, tm, steps_per_seq):
    if arr.ndim == 3:
        return pl.BlockSpec((None, 1, arr.shape[-1]), lambda i: (i // steps_per_seq, 0, 0))
    return pl.BlockSpec((tm, arr.shape[-1]), lambda i: (i, 0))


def _proj(x2, sh, sc, g_attn, w_perm, cs, sn, gains, tm, steps_per_seq, token_minor):
    n, d = x2.shape
    if cs.shape[0] == n:
        tab_spec = pl.BlockSpec((tm, LANES), lambda i: (i, 0))
    else:
        tab_spec = pl.BlockSpec((tm, LANES), lambda i: (i % steps_per_seq, 0))
    const = lambda shape: pl.BlockSpec(shape, lambda i: (0,) * len(shape))
    widths = [(512, BF16), (512, BF16), (512, BF16), (128, F32), (256, F32), (D_IDX, F32), (128, F32),
              (128, F32), (256, F32), (128, BF16), (256, BF16), (128, BF16), (128, BF16), (256, BF16)]
    shapes = [jax.ShapeDtypeStruct((n, w), dt) for w, dt in widths]
    specs = [pl.BlockSpec((tm, w), lambda i: (i, 0)) for w, _ in widths]
    if token_minor:
        nseq, t = n // (tm * steps_per_seq), tm * steps_per_seq
        for o in (3, 4, 5, 7):
            w = widths[o][0]
            shapes[o] = jax.ShapeDtypeStruct((nseq, w, t), F32)
            specs[o] = pl.BlockSpec((None, w, tm), lambda i: (i // steps_per_seq, 0, i % steps_per_seq))
        shapes[8] = jax.ShapeDtypeStruct((n * KVB, VB), F32)
        specs[8] = pl.BlockSpec((tm * KVB, VB), lambda i: (i, 0))
    return pl.pallas_call(
        functools.partial(_proj_kernel, token_minor=token_minor),
        out_shape=shapes,
        grid=(n // tm,),
        in_specs=[pl.BlockSpec((tm, d), lambda i: (i, 0)),
                  _mod_spec(sh, tm, steps_per_seq), _mod_spec(sc, tm, steps_per_seq),
                  const((1, d)), const((d, C_END)), tab_spec, tab_spec,
                  const((8, LANES)), const((LANES, LANES))],
        out_specs=specs,
        compiler_params=_cparams("arbitrary"),
        name="proj",
    )(x2, sh, sc, g_attn.reshape(1, d), w_perm, cs, sn, gains, _head_mean_matrix())


def _sortable_key(x):
    bits = lax.bitcast_convert_type(x, I32)
    return bits ^ ((bits >> 31) & 0x7FFFFFFF)


COUNT_ROWS = 128


def _count_keys(pred, skey_ref, nkc, rows, ck, group):
    rb = min(rows, COUNT_ROWS)
    ngroups = (nkc + group - 1) // group
    parts = []
    for r0 in range(0, rows, rb):
        def body(cg, acc, r0=r0):
            for u in range(group):
                c = cg * group + u
                for g in range(ck // LANES):
                    k = skey_ref[c, r0:r0 + rb, g * LANES:(g + 1) * LANES]
                    acc = acc + jnp.where(pred(k, c, g * LANES, r0), 1.0, 0.0)
            return acc
        parts.append(lax.fori_loop(0, ngroups, body, jnp.zeros((rb, LANES), F32)))
    acc = parts[0] if len(parts) == 1 else jnp.concatenate(parts, axis=0)
    return jnp.broadcast_to(jnp.sum(acc, axis=-1, keepdims=True), (rows, LANES))


def _select_topk(skey_ref, nkc, ksel, rows, ck, group=1):
    kf = float(ksel)
    rb = min(rows, COUNT_ROWS)
    strip = lambda a, r0: a[r0:r0 + rb]
    count = functools.partial(_count_keys, skey_ref=skey_ref, nkc=nkc, rows=rows, ck=ck, group=group)

    def bit_body(i, t):
        cand = t + lax.shift_left(jnp.int32(1), 31 - i)
        cnt = count(lambda k, c, c0, r0: k >= strip(cand, r0))
        return jnp.where(cnt >= kf, cand, t)

    thr = lax.fori_loop(0, 32, bit_body, jnp.full((rows, LANES), INT_MIN, I32))
    n_ge = count(lambda k, c, c0, r0: k >= strip(thr, r0))
    n_gt = count(lambda k, c, c0, r0: k > strip(thr, r0))
    excess = jnp.logical_and(n_ge > kf, thr != INT_MIN)
    need = kf - n_gt

    @pl.when(jnp.max(jnp.where(excess, 1.0, 0.0)) > 0.0)
    def _():
        lane = lax.broadcasted_iota(I32, (rb, LANES), 1)
        nbits = max(1, (skey_ref.shape[0] * ck).bit_length())

        def idx_body(i, p):
            cand = p | lax.shift_left(jnp.int32(1), nbits - 1 - i)
            tied_before = lambda k, c, c0, r0: jnp.logical_and(
                k == strip(thr, r0), c * ck + c0 + lane < strip(cand, r0))
            cnt = count(tied_before)
            return jnp.where(cnt <= need, cand, p)

        p_keep = lax.fori_loop(0, nbits, idx_body, jnp.zeros((rows, LANES), I32))

        def drop_body(c, _):
            for r0 in range(0, rows, rb):
                for g in range(ck // LANES):
                    k = skey_ref[c, r0:r0 + rb, g * LANES:(g + 1) * LANES]
                    drop = jnp.logical_and(k == strip(thr, r0), c * ck + g * LANES + lane >= strip(p_keep, r0))
                    drop = jnp.logical_and(drop, strip(excess, r0))
                    skey_ref[c, r0:r0 + rb, g * LANES:(g + 1) * LANES] = jnp.where(drop, INT_MIN, k)
            return 0

        lax.fori_loop(0, nkc, drop_body, 0)

    return jnp.maximum(thr, INT_MIN + 1)


def _flash_init(m_ref, acc_ref):
    m_ref[...] = jnp.full(m_ref.shape, M_INIT, F32)
    acc_ref[...] = jnp.zeros(acc_ref.shape, F32)


def _flash_update(s, pv_fn, m_ref, acc_ref):
    m_prev = m_ref[...]
    m_new = jnp.maximum(m_prev, jnp.max(s, axis=-1, keepdims=True))
    alpha = jnp.exp2(m_prev - m_new)
    rep = lambda a, n: a if n == 1 else jnp.concatenate([a] * n, axis=1)
    p = jnp.exp2(s - rep(m_new, s.shape[1] // LANES)).astype(BF16)
    acc_ref[...] = rep(alpha, acc_ref.shape[1] // LANES) * acc_ref[...] + pv_fn(p)
    m_ref[...] = m_new


def _pv_pages(vaug_list):
    def fn(p):
        upd = None
        for i, va in enumerate(vaug_list):
            nk = va.shape[0]
            d = _dot(p[:, i * nk:(i + 1) * nk], va)
            upd = d if upd is None else upd + d
        return upd
    return fn


def _with_ones(v):
    return jnp.concatenate([v, jnp.ones((v.shape[0], LANES), v.dtype)], axis=1)


def _dsa_kernel(qi_ref, wi_ref, ki_ref, qa_ref, ka_ref, va_ref, o_ref, skey_ref, m_ref, acc_ref,
                *, tq, ck, ksel):
    i = pl.program_id(1)
    nkc = (i + 1) * (tq // ck)
    lane = lax.broadcasted_iota(I32, (tq, LANES), 1)
    hi_half = lane >= HD
    qi = qi_ref[...]
    w = wi_ref[...]
    zero_b = jnp.zeros((tq, LANES), BF16)
    q_heads = []
    for h in range(H_IDX):
        grp = qi[:, (h // 2) * LANES:(h // 2 + 1) * LANES]
        q_heads.append(jnp.where(hi_half if h % 2 else jnp.logical_not(hi_half), grp, zero_b))
    w_heads = [w[:, h:h + 1] for h in range(H_IDX)]
    rows = i * tq + lax.broadcasted_iota(I32, (tq, ck), 0)
    col_in = lax.broadcasted_iota(I32, (tq, ck), 1)

    def score_body(c, _):
        kc = ki_ref[pl.ds(pl.multiple_of(c * ck, ck), ck), :]
        acc = jnp.zeros((tq, ck), F32)
        for h in range(H_IDX):
            acc = acc + jnp.maximum(_dot_nt(q_heads[h], kc), 0.0) * w_heads[h]
        key = _sortable_key(acc)
        skey_ref[c] = jnp.where(c * ck + col_in <= rows, key, INT_MIN)
        return 0

    lax.fori_loop(0, nkc, score_body, 0)
    group = 2 if skey_ref.shape[0] % 2 == 0 else 1

    @pl.when(nkc % group != 0)
    def _():
        skey_ref[nkc] = jnp.full((tq, ck), INT_MIN, I32)

    thr = _select_topk(skey_ref, nkc, ksel, tq, ck, group)
    thr = jnp.concatenate([thr] * (ck // LANES), axis=1)

    qa = qa_ref[...].astype(F32)

    def head_at_half(h, half):
        grp = qa[:, (h // 2) * LANES:(h // 2 + 1) * LANES]
        if h % 2 != half:
            grp = pltpu.roll(grp, HD, 1)
        return jnp.where(hi_half if half else jnp.logical_not(hi_half), grp, 0.0)

    g_per = HA // KVA
    outs = []
    for j in range(KVA):
        q_rows = jnp.concatenate([head_at_half(g_per * j + g, j) for g in range(g_per)], axis=0).astype(BF16)
        _flash_init(m_ref, acc_ref)

        def att_body(c, _):
            off = pl.multiple_of(c * ck, ck)
            s = _dot_nt(q_rows, ka_ref[pl.ds(off, ck), :])
            bias = jnp.where(skey_ref[c] >= thr, 0.0, NEG_BIAS)
            s = (s.reshape(g_per, tq, ck) + bias[None]).reshape(g_per * tq, ck)
            _flash_update(s, _pv_pages([_with_ones(va_ref[pl.ds(off, ck), :])]), m_ref, acc_ref)
            return 0

        lax.fori_loop(0, nkc, att_body, 0)
        acc = acc_ref[...]
        outs.append(acc[:, :LANES] / acc[:, LANES:])

    for gg in range(HA // 2):
        j = (2 * gg) // g_per
        a = outs[j][((2 * gg) % g_per) * tq:((2 * gg) % g_per + 1) * tq]
        b = outs[j][((2 * gg + 1) % g_per) * tq:((2 * gg + 1) % g_per + 1) * tq]
        if j == 1:
            a = pltpu.roll(a, HD, 1)
        else:
            b = pltpu.roll(b, HD, 1)
        o_ref[:, gg * LANES:(gg + 1) * LANES] = jnp.where(hi_half, b, a).astype(o_ref.dtype)


def _dsa_prompt(qi, wi, kib, qa, kab, vab, ksel):
    b, t, _ = qi.shape
    tq, ck = Q_BLOCK, K_CHUNK
    qspec = lambda w: pl.BlockSpec((None, tq, w), lambda bb, i: (bb, i, 0))
    kspec = lambda w: pl.BlockSpec((None, t, w), lambda bb, i: (bb, 0, 0))
    g_per = HA // KVA
    return pl.pallas_call(
        functools.partial(_dsa_kernel, tq=tq, ck=ck, ksel=ksel),
        out_shape=jax.ShapeDtypeStruct((b, t, 512), BF16),
        grid=(b, t // tq),
        in_specs=[qspec(512), qspec(128), kspec(128), qspec(512), kspec(128), kspec(128)],
        out_specs=qspec(512),
        scratch_shapes=[pltpu.VMEM((t // ck, tq, ck), I32),
                        pltpu.VMEM((g_per * tq, LANES), F32),
                        pltpu.VMEM((g_per * tq, 2 * LANES), F32)],
        compiler_params=_cparams("arbitrary", "arbitrary"),
        name="dsa_prompt",
    )(qi, wi, kib, qa, kab, vab)


def _lambda_value(lamv, lam_init):
    l1 = jnp.sum(lamv[0:1, :] * lamv[1:2, :], axis=-1, keepdims=True)
    l2 = jnp.sum(lamv[2:3, :] * lamv[3:4, :], axis=-1, keepdims=True)
    return jnp.exp(l1) - jnp.exp(l2) + lam_init


def _sub_norm(o, gsub, lam_init):
    ms = jnp.mean(o * o, axis=-1, keepdims=True)
    return o * lax.rsqrt(ms + EPS) * gsub * (1.0 - lam_init)


def _diff_kernel(qb_ref, kb_ref, vb_ref, lamv_ref, gsub_ref, o_ref, m_ref, acc_ref, *, tq, ck, lam_init):
    i = pl.program_id(1)
    n_full = i * (tq // ck)
    lane = lax.broadcasted_iota(I32, (tq, LANES), 1)
    hi_half = lane >= HD
    qb = qb_ref[...]
    zero_b = jnp.zeros((tq, LANES), BF16)
    lam = _lambda_value(lamv_ref[...], lam_init)
    gsub = gsub_ref[...]
    g_per = HB // KVB
    r_in = lax.broadcasted_iota(I32, (tq, ck), 0)
    c_in = lax.broadcasted_iota(I32, (tq, ck), 1)

    for j in range(KVB):
        parts = []
        for g in range(g_per):
            grp = qb[:, (g_per * j + g) * LANES:(g_per * j + g + 1) * LANES]
            parts.append(jnp.where(jnp.logical_not(hi_half), grp, zero_b))
            parts.append(jnp.where(hi_half, grp, zero_b))
        q_rows = jnp.concatenate(parts, axis=0)
        kcols = slice(j * LANES, (j + 1) * LANES)
        _flash_init(m_ref, acc_ref)

        def step(c, bias):
            off = pl.multiple_of(c * ck, ck)
            s = _dot_nt(q_rows, kb_ref[pl.ds(off, ck), kcols])
            if bias is not None:
                s = (s.reshape(2 * g_per, tq, ck) + bias[None]).reshape(2 * g_per * tq, ck)
            _flash_update(s, _pv_pages([_with_ones(vb_ref[pl.ds(off, ck), kcols])]), m_ref, acc_ref)

        def full_body(c, _):
            step(c, None)
            return 0

        lax.fori_loop(0, n_full, full_body, 0)
        for d in range(tq // ck):
            bias = jnp.where(d * ck + c_in <= r_in, 0.0, NEG_BIAS)
            step(n_full + d, bias)

        acc = acc_ref[...]
        o = acc[:, :LANES] / acc[:, LANES:]
        for g in range(g_per):
            o1 = o[(2 * g) * tq:(2 * g + 1) * tq]
            o2 = o[(2 * g + 1) * tq:(2 * g + 2) * tq]
            hh = g_per * j + g
            o_ref[:, hh * LANES:(hh + 1) * LANES] = _sub_norm(o1 - lam * o2, gsub, lam_init).astype(o_ref.dtype)


def _diff_prompt(qb, kbb, vbb, lamv, gsub, lam_init):
    b, t, _ = qb.shape
    tq, ck = Q_BLOCK, K_CHUNK
    qspec = lambda w: pl.BlockSpec((None, tq, w), lambda bb, i: (bb, i, 0))
    kspec = lambda w: pl.BlockSpec((None, t, w), lambda bb, i: (bb, 0, 0))
    const = lambda shape: pl.BlockSpec(shape, lambda bb, i: (0,) * len(shape))
    rows = 2 * (HB // KVB) * tq
    return pl.pallas_call(
        functools.partial(_diff_kernel, tq=tq, ck=ck, lam_init=lam_init),
        out_shape=jax.ShapeDtypeStruct((b, t, 512), BF16),
        grid=(b, t // tq),
        in_specs=[qspec(512), kspec(256), kspec(256), const((8, LANES)), const((1, LANES))],
        out_specs=qspec(512),
        scratch_shapes=[pltpu.VMEM((rows, LANES), F32), pltpu.VMEM((rows, 2 * LANES), F32)],
        compiler_params=_cparams("arbitrary", "arbitrary"),
        name="diff_prompt",
    )(qb, kbb, vbb, lamv, gsub)


def _head_sum(r):
    r3 = r.reshape(H_IDX, 8, r.shape[1])
    s = r3[0]
    for h in range(1, H_IDX):
        s = s + r3[h]
    return s + 0.0


def _s1_kernel(pt_ref, qm_ref, wcol_ref, knew_ref, *rest, pg, tnew):
    pages = rest[:pg]
    sc_ref, scn_ref = rest[pg], rest[pg + 1]
    qm = qm_ref[...]
    wcol = wcol_ref[...]
    for i in range(pg):
        s = _dot(qm, pages[i][...].astype(BF16))
        sc_ref[:, i * LANES:(i + 1) * LANES] = _head_sum(jnp.maximum(s, 0.0) * wcol)

    @pl.when(pl.program_id(1) == 0)
    def _():
        s = _dot_nt(qm, knew_ref[...])
        sn = _head_sum(jnp.maximum(s, 0.0) * wcol)
        r = lax.broadcasted_iota(I32, sn.shape, 0)
        c = lax.broadcasted_iota(I32, sn.shape, 1)
        scn_ref[...] = jnp.where(jnp.logical_and(c <= r, c < tnew), sn, -jnp.inf)


def _page_specs(rows, pg):
    return [pl.BlockSpec((None, rows, LANES), functools.partial(lambda b, g, pt, i: (pt[b, g * pg + i], 0, 0), i=i))
            for i in range(pg)]


def _sample_scores(page_table, qm, wcol, knew_i, cik, pg, tnew):
    db, n_pages = page_table.shape
    grid_spec = pltpu.PrefetchScalarGridSpec(
        num_scalar_prefetch=1,
        grid=(db, n_pages // pg),
        in_specs=[pl.BlockSpec((None, 64, D_IDX), lambda b, g, pt: (b, 0, 0)),
                  pl.BlockSpec((None, 64, LANES), lambda b, g, pt: (b, 0, 0)),
                  pl.BlockSpec((None, LANES, D_IDX), lambda b, g, pt: (b, 0, 0))]
                 + _page_specs(D_IDX, pg),
        out_specs=[pl.BlockSpec((None, 8, pg * LANES), lambda b, g, pt: (b, 0, g)),
                   pl.BlockSpec((None, 8, LANES), lambda b, g, pt: (b, 0, 0))],
    )
    return pl.pallas_call(
        functools.partial(_s1_kernel, pg=pg, tnew=tnew),
        out_shape=[jax.ShapeDtypeStruct((db, 8, n_pages * LANES), F32),
                   jax.ShapeDtypeStruct((db, 8, LANES), F32)],
        grid_spec=grid_spec,
        compiler_params=_cparams("arbitrary", "arbitrary"),
        name="sample_scores",
    )(page_table, qm, wcol, knew_i, *([cik] * pg))


def _s1b_kernel(sc_ref, o_ref, skey_ref, *, ksel):
    nc, rows, ck = skey_ref.shape

    def conv(c, _):
        x = sc_ref[c]
        skey_ref[c] = jnp.where(x == -jnp.inf, INT_MIN, _sortable_key(x))
        return 0

    lax.fori_loop(0, nc, conv, 0)
    thr = _select_topk(skey_ref, nc, ksel, rows, ck)

    def emit(c, _):
        o_ref[c] = jnp.where(skey_ref[c] >= thr, 0.0, NEG_BIAS)
        return 0

    lax.fori_loop(0, nc, emit, 0)


def _sample_select(sct, ksel):
    nc, r, _ = sct.shape
    rb = min(r, 128)
    spec = pl.BlockSpec((nc, rb, LANES), lambda i: (0, i, 0))
    return pl.pallas_call(
        functools.partial(_s1b_kernel, ksel=ksel),
        out_shape=jax.ShapeDtypeStruct(sct.shape, F32),
        grid=(r // rb,),
        in_specs=[spec],
        out_specs=spec,
        scratch_shapes=[pltpu.VMEM((nc, rb, LANES), I32)],
        compiler_params=_cparams("arbitrary"),
        name="sample_select",
    )(sct)


def _s2_kernel(pt_ref, qa_ref, qb_ref, selp_ref, seln_ref, kan_ref, van_ref, kbn_ref, vbn_ref,
               lamv_ref, gsub_ref, *rest, pg, tnew, lam_init):
    akp, avp, bkp, bvp = (rest[k * pg:(k + 1) * pg] for k in range(4))
    oa_ref, ob_ref, ma_ref, acca_ref, mb_ref, accb_ref = rest[4 * pg:]
    g = pl.program_id(1)
    qa = qa_ref[...]
    qb = qb_ref[...]

    rows_j = qb.shape[0] // KVB
    ones_kd = jnp.ones((LANES, LANES), BF16)

    def pv_b(values_of_kv):
        def fn(p):
            parts = []
            for j in range(KVB):
                upd = None
                for i in range(p.shape[1] // LANES):
                    va = jnp.concatenate([values_of_kv(i, j), ones_kd], axis=1)
                    d = _dot(p[j * rows_j:(j + 1) * rows_j, i * LANES:(i + 1) * LANES], va)
                    upd = d if upd is None else upd + d
                parts.append(upd)
            return jnp.concatenate(parts, axis=0)
        return fn

    @pl.when(g == 0)
    def _():
        _flash_init(ma_ref, acca_ref)
        _flash_init(mb_ref, accb_ref)
        sa = _dot_nt(qa, kan_ref[...]) + jnp.tile(seln_ref[...], (HA, 1))
        _flash_update(sa, _pv_pages([_with_ones(van_ref[...])]), ma_ref, acca_ref)
        r = lax.broadcasted_iota(I32, (8, LANES), 0)
        c = lax.broadcasted_iota(I32, (8, LANES), 1)
        causal = jnp.where(jnp.logical_and(c <= r, c < tnew), 0.0, NEG_BIAS)
        sb = _dot_nt(qb, kbn_ref[...]) + jnp.tile(causal, (8, 1))
        _flash_update(sb, pv_b(lambda i, j: vbn_ref[:, j * LANES:(j + 1) * LANES]), mb_ref, accb_ref)

    sa = jnp.concatenate([_dot(qa, akp[i][...].astype(BF16)) for i in range(pg)], axis=1)
    sa = sa + jnp.tile(selp_ref[...], (HA, 1))

    def pv_a(p):
        upd = None
        for i in range(pg):
            vt = jnp.concatenate([avp[i][...].astype(BF16), ones_kd], axis=0)
            d = _dot_nt(p[:, i * LANES:(i + 1) * LANES], vt)
            upd = d if upd is None else upd + d
        return upd

    _flash_update(sa, pv_a, ma_ref, acca_ref)
    sb = jnp.concatenate([_dot(qb, bkp[i][...].astype(BF16)) for i in range(pg)], axis=1)
    _flash_update(sb, pv_b(lambda i, j: bvp[i][pl.ds(j, LANES, stride=KVB), :].astype(BF16)), mb_ref, accb_ref)

    @pl.when(g == pl.num_programs(1) - 1)
    def _():
        acca = acca_ref[...]
        oa_ref[...] = acca[:, :LANES] / acca[:, LANES:]
        accb = accb_ref[...]
        x = (accb[:, :LANES] / accb[:, LANES:]).reshape(HB, 2, 8, LANES)
        lam = _lambda_value(lamv_ref[...], lam_init)
        gsub = gsub_ref[...]
        for hh in range(HB):
            ob_ref[:, hh * LANES:(hh + 1) * LANES] = _sub_norm(x[hh, 0] - lam * x[hh, 1], gsub, lam_init)


def _sample_attend(page_table, qa_rows, qb_rows, selp, seln, kan, van, kbn, vbn, lamv, gsub,
                   cak, cav, cbk, cbv, pg, tnew, lam_init):
    db, n_pages = page_table.shape
    per_b = lambda *shape: pl.BlockSpec((None,) + shape, lambda b, g, pt: (b,) + (0,) * len(shape))
    const = lambda shape: pl.BlockSpec(shape, lambda b, g, pt: (0,) * len(shape))
    grid_spec = pltpu.PrefetchScalarGridSpec(
        num_scalar_prefetch=1,
        grid=(db, n_pages // pg),
        in_specs=[per_b(64, 128), per_b(64, 256),
                  pl.BlockSpec((None, 8, pg * LANES), lambda b, g, pt: (b, 0, g)), per_b(8, LANES),
                  per_b(LANES, 128), per_b(LANES, 128), per_b(LANES, 256), per_b(LANES, 256),
                  const((8, LANES)), const((1, LANES))]
                 + _page_specs(128, pg) + _page_specs(128, pg)
                 + _page_specs(256, pg) + _page_specs(256, pg),
        out_specs=[per_b(64, 128), per_b(8, 512)],
        scratch_shapes=[pltpu.VMEM((64, LANES), F32), pltpu.VMEM((64, 2 * LANES), F32),
                        pltpu.VMEM((64, LANES), F32), pltpu.VMEM((64, 2 * LANES), F32)],
    )
    return pl.pallas_call(
        functools.partial(_s2_kernel, pg=pg, tnew=tnew, lam_init=lam_init),
        out_shape=[jax.ShapeDtypeStruct((db, 64, 128), F32), jax.ShapeDtypeStruct((db, 8, 512), F32)],
        grid_spec=grid_spec,
        compiler_params=_cparams("arbitrary", "arbitrary"),
        name="sample_attend",
    )(page_table, qa_rows, qb_rows, selp, seln, kan, van, kbn, vbn, lamv, gsub,
      *([cak] * pg), *([cav] * pg), *([cbk] * pg), *([cbv] * pg))


TOKEN_ROWS = 8


def _outproj_kernel(oa_ref, ob_ref, x_ref, gt_ref, sh_ref, sc_ref, g_ref, wo_ref, wr_ref, br_ref,
                    h_ref, xm_ref, comb_ref, sel_ref):
    half = oa_ref.shape[1]
    tm = x_ref.shape[0]
    o = _dot(oa_ref[...].astype(BF16), wo_ref[:half, :]) + _dot(ob_ref[...].astype(BF16), wo_ref[half:, :])
    h = x_ref[...] + gt_ref[...] * o
    h_ref[...] = h
    ms = jnp.mean(h * h, axis=-1, keepdims=True)
    xm = h * lax.rsqrt(ms + EPS) * g_ref[...]
    xm = xm * (1.0 + sc_ref[...]) + sh_ref[...]
    for c in range(TOKEN_ROWS):
        xm_ref[pl.ds(c, tm, stride=TOKEN_ROWS), :] = xm[:, c * LANES:(c + 1) * LANES]
    logits = _dot3(xm, wr_ref[...]) + br_ref[...]
    lane = lax.broadcasted_iota(I32, logits.shape, 1).astype(F32)
    vals, hots = [], []
    for _ in range(TOP_K):
        m = jnp.max(logits, axis=-1, keepdims=True)
        idx = jnp.min(jnp.where(logits == m, lane, float(LANES)), axis=-1, keepdims=True)
        hot = lane == idx
        vals.append(m)
        hots.append(hot)
        logits = jnp.where(hot, -3e38, logits)
    es = [jnp.exp(v - vals[0]) for v in vals]
    den = es[0] + es[1] + es[2] + es[3]
    comb = jnp.zeros(logits.shape, F32)
    sel = jnp.zeros(logits.shape, F32)
    for e, hot in zip(es, hots):
        comb = comb + jnp.where(hot, e / den, 0.0)
        sel = sel + jnp.where(hot, 1.0, 0.0)
    comb_ref[...] = comb
    sel_ref[...] = sel.astype(BF16)


def _outproj(oa, ob, x2, gt, sh, sc, g_ffn, w_out_b, w_router_p, b_router_p, tm, steps_per_seq):
    n, d = x2.shape
    assert d == TOKEN_ROWS * LANES
    const = lambda shape: pl.BlockSpec(shape, lambda i: (0,) * len(shape))
    row = lambda w: pl.BlockSpec((tm, w), lambda i: (i, 0))
    return pl.pallas_call(
        _outproj_kernel,
        out_shape=[jax.ShapeDtypeStruct((n, d), F32), jax.ShapeDtypeStruct((n * TOKEN_ROWS, LANES), F32),
                   jax.ShapeDtypeStruct((n, LANES), F32), jax.ShapeDtypeStruct((n, LANES), BF16)],
        grid=(n // tm,),
        in_specs=[row(oa.shape[1]), row(ob.shape[1]), row(d),
                  _mod_spec(gt, tm, steps_per_seq), _mod_spec(sh, tm, steps_per_seq),
                  _mod_spec(sc, tm, steps_per_seq),
                  const((1, d)), const(w_out_b.shape), const((d, LANES)), const((1, LANES))],
        out_specs=[row(d), pl.BlockSpec((tm * TOKEN_ROWS, LANES), lambda i: (i, 0)), row(LANES), row(LANES)],
        compiler_params=_cparams("arbitrary"),
        name="outproj",
    )(oa, ob, x2, gt, sh, sc, g_ffn.reshape(1, d), w_out_b, w_router_p, b_router_p)


MOE_TILE = 256
WAIT_UNROLL = 32


def _rank_kernel(sel_ref, tri_ref, rank_ref, cnt_ref, carry_ref):
    @pl.when(pl.program_id(0) == 0)
    def _():
        carry_ref[...] = jnp.zeros(carry_ref.shape, F32)

    sel = sel_ref[...]
    tm = sel.shape[0]
    before = _dot(tri_ref[...], sel)
    carry = carry_ref[...]
    rank_ref[...] = before + carry[0:1, :]
    carry = carry + (before[tm - 1:tm, :] + sel[tm - 1:tm, :].astype(F32))
    carry_ref[...] = carry
    cnt_ref[...] = carry


def _moe_rank(sel, tm):
    n = sel.shape[0]
    r = lax.broadcasted_iota(I32, (tm, tm), 0)
    c = lax.broadcasted_iota(I32, (tm, tm), 1)
    tri = jnp.where(c < r, 1.0, 0.0).astype(BF16)
    return pl.pallas_call(
        _rank_kernel,
        out_shape=[jax.ShapeDtypeStruct((n, LANES), F32), jax.ShapeDtypeStruct((8, LANES), F32)],
        grid=(n // tm,),
        in_specs=[pl.BlockSpec((tm, LANES), lambda i: (i, 0)), pl.BlockSpec((tm, tm), lambda i: (0, 0))],
        out_specs=[pl.BlockSpec((tm, LANES), lambda i: (i, 0)), pl.BlockSpec((8, LANES), lambda i: (0, 0))],
        scratch_shapes=[pltpu.VMEM((8, LANES), F32)],
        compiler_params=_cparams("arbitrary"),
        name="moe_rank",
    )(sel, tri)


def _slots_kernel(rank_ref, sel_ref, comb_ref, starts_ref, slot_ref, gate_ref):
    slotv = rank_ref[...] + starts_ref[...]
    comb = comb_ref[...]
    sel = sel_ref[...].astype(F32) > 0.0
    lane = lax.broadcasted_iota(I32, slotv.shape, 1)
    lanef = lane.astype(F32)
    slots = jnp.zeros(slotv.shape, F32)
    gates = jnp.zeros(slotv.shape, F32)
    for k in range(TOP_K):
        idx = jnp.min(jnp.where(sel, lanef, float(LANES)), axis=-1, keepdims=True)
        hot = lanef == idx
        sk = jnp.sum(jnp.where(hot, slotv, 0.0), axis=-1, keepdims=True)
        gk = jnp.sum(jnp.where(hot, comb, 0.0), axis=-1, keepdims=True)
        slots = jnp.where(lane == k, sk, slots)
        gates = jnp.where(lane == k, gk, gates)
        sel = jnp.logical_and(sel, jnp.logical_not(hot))
    slot_ref[...] = slots.astype(I32)
    gate_ref[...] = gates


def _moe_slots(rank, sel, comb, starts, tm):
    n = rank.shape[0]
    row = pl.BlockSpec((tm, LANES), lambda i: (i, 0))
    return pl.pallas_call(
        _slots_kernel,
        out_shape=[jax.ShapeDtypeStruct((n, LANES), I32), jax.ShapeDtypeStruct((n, LANES), F32)],
        grid=(n // tm,),
        in_specs=[row, row, row, pl.BlockSpec((1, LANES), lambda i: (0, 0))],
        out_specs=[row, row],
        compiler_params=_cparams("arbitrary"),
        name="moe_slots",
    )(rank, sel, comb, starts)


def _slab(ref, index):
    if isinstance(index, int):
        return ref.at[pl.ds(index * TOKEN_ROWS, TOKEN_ROWS), :]
    return ref.at[pl.ds(pl.multiple_of(index * TOKEN_ROWS, TOKEN_ROWS), TOKEN_ROWS), :]


def _wait_all(copy, count):
    def body(_, carry):
        for _u in range(WAIT_UNROLL):
            copy.wait()
        return carry
    lax.fori_loop(0, count // WAIT_UNROLL, body, 0)
    for _u in range(count % WAIT_UNROLL):
        copy.wait()


def _dispatch_kernel(slots_ref, x_ref, xs_ref, sem):
    tm = x_ref.shape[0] // TOKEN_ROWS

    def body(t, carry):
        src = _slab(x_ref, t)
        for k in range(TOP_K):
            pltpu.make_async_copy(src, _slab(xs_ref, slots_ref[t * TOP_K + k]), sem).start()
        return carry

    lax.fori_loop(0, tm, body, 0)
    _wait_all(pltpu.make_async_copy(_slab(x_ref, 0), _slab(xs_ref, 0), sem), tm * TOP_K)


def _moe_dispatch(slots_flat, xm_slabs, tm):
    n = xm_slabs.shape[0] // TOKEN_ROWS
    return pl.pallas_call(
        _dispatch_kernel,
        out_shape=jax.ShapeDtypeStruct((n * TOP_K * TOKEN_ROWS, LANES), F32),
        grid=(n // tm,),
        in_specs=[pl.BlockSpec((tm * TOP_K,), lambda i: (i,), memory_space=pltpu.SMEM),
                  pl.BlockSpec((tm * TOKEN_ROWS, LANES), lambda i: (i, 0))],
        out_specs=pl.BlockSpec(memory_space=pl.ANY),
        scratch_shapes=[pltpu.SemaphoreType.DMA],
        compiler_params=_cparams("arbitrary"),
        name="moe_dispatch",
    )(slots_flat, xm_slabs)


def _ffn_kernel(tile_ref, exp_ref, lo_ref, hi_ref, xs_ref, wgu_ref, bgu_ref, wd_ref, bd_ref, ys_ref):
    g = pl.program_id(0)
    lo, hi = lo_ref[g], hi_ref[g]
    tme = xs_ref.shape[0] // TOKEN_ROWS
    dff = wd_ref.shape[0]

    @pl.when(hi > lo)
    def _():
        x = jnp.concatenate([xs_ref[pl.ds(c, tme, stride=TOKEN_ROWS), :] for c in range(TOKEN_ROWS)], axis=1)
        hgu = _dot(x.astype(BF16), wgu_ref[...]) + bgu_ref[...]
        gate = jnp.minimum(hgu[:, :dff], SWIGLU_LIMIT)
        up = jnp.clip(hgu[:, dff:], -SWIGLU_LIMIT, SWIGLU_LIMIT)
        a = (up + 1.0) * (gate * jax.nn.sigmoid(SWIGLU_ALPHA * gate))
        yo = _dot(a.astype(BF16), wd_ref[...]) + bd_ref[...]
        row = lax.broadcasted_iota(I32, (tme, LANES), 0)
        mine = jnp.logical_and(row >= lo, row < hi)

        @pl.when(lo == 0)
        def _():
            for c in range(TOKEN_ROWS):
                ys_ref[pl.ds(c, tme, stride=TOKEN_ROWS), :] = yo[:, c * LANES:(c + 1) * LANES]

        @pl.when(lo != 0)
        def _():
            for c in range(TOKEN_ROWS):
                dst = ys_ref.at[pl.ds(c, tme, stride=TOKEN_ROWS), :]
                dst[...] = jnp.where(mine, yo[:, c * LANES:(c + 1) * LANES], dst[...])


def _moe_plan(counts, n_slots, tme):
    cnt = counts[:N_EXPERTS].astype(I32)
    ends = jnp.cumsum(cnt)
    starts = ends - cnt
    first_tile = starts // tme
    n_items = jnp.where(cnt > 0, (ends - 1) // tme - first_tile + 1, 0)
    item_end = jnp.cumsum(n_items)
    item_start = item_end - n_items
    n_work = n_slots // tme + N_EXPERTS - 1
    g = jnp.arange(n_work, dtype=I32)
    gi = jnp.minimum(g, item_end[-1] - 1)
    e = jnp.sum((item_end[None, :] <= gi[:, None]).astype(I32), axis=1)
    tile = first_tile[e] + (gi - item_start[e])
    lo = jnp.maximum(starts[e], tile * tme) - tile * tme
    hi = jnp.minimum(ends[e], (tile + 1) * tme) - tile * tme
    real = g < item_end[-1]
    return tile, e, jnp.where(real, lo, 0), jnp.where(real, hi, 0), starts


def _moe_ffn(plan, xs, wgu_b, b_gu, wd_b, b_down, tme):
    tile, e, lo, hi = plan
    ne, d, dff2 = wgu_b.shape
    dff = dff2 // 2
    slab_spec = pl.BlockSpec((tme * TOKEN_ROWS, LANES), lambda g, tile, e, lo, hi: (tile[g], 0))
    per_e = lambda *shape: pl.BlockSpec((None,) + shape, lambda g, tile, e, lo, hi: (e[g],) + (0,) * len(shape))
    grid_spec = pltpu.PrefetchScalarGridSpec(
        num_scalar_prefetch=4,
        grid=(tile.shape[0],),
        in_specs=[slab_spec, per_e(d, dff2), per_e(1, dff2), per_e(dff, d), per_e(1, d)],
        out_specs=slab_spec,
    )
    return pl.pallas_call(
        _ffn_kernel,
        out_shape=jax.ShapeDtypeStruct(xs.shape, F32),
        grid_spec=grid_spec,
        compiler_params=_cparams("arbitrary"),
        name="moe_ffn",
    )(tile, e, lo, hi, xs, wgu_b, b_gu.reshape(ne, 1, dff2), wd_b, b_down.reshape(ne, 1, d))


def _combine_kernel(slots_ref, gate_ref, h_ref, gt_ref, ys_ref, y_ref, buf_ref, sem):
    tm = h_ref.shape[0]

    def body(t, carry):
        for k in range(TOP_K):
            pltpu.make_async_copy(_slab(ys_ref, slots_ref[t * TOP_K + k]),
                                  _slab(buf_ref, t * TOP_K + k), sem).start()
        return carry

    lax.fori_loop(0, tm, body, 0)
    _wait_all(pltpu.make_async_copy(_slab(ys_ref, 0), _slab(buf_ref, 0), sem), tm * TOP_K)
    gate = gate_ref[...]
    gt = gt_ref[...]
    for c in range(TOKEN_ROWS):
        acc = None
        for k in range(TOP_K):
            part = buf_ref[pl.ds(k * TOKEN_ROWS + c, tm, stride=TOP_K * TOKEN_ROWS), :] * gate[:, k:k + 1]
            acc = part if acc is None else acc + part
        sl = slice(c * LANES, (c + 1) * LANES)
        y_ref[:, sl] = h_ref[:, sl] + gt[:, sl] * acc


def _moe_combine(slots_flat, gate4, h, gt, ys, tm, steps_per_seq):
    n, d = h.shape
    return pl.pallas_call(
        _combine_kernel,
        out_shape=jax.ShapeDtypeStruct((n, d), F32),
        grid=(n // tm,),
        in_specs=[pl.BlockSpec((tm * TOP_K,), lambda i: (i,), memory_space=pltpu.SMEM),
                  pl.BlockSpec((tm, LANES), lambda i: (i, 0)), pl.BlockSpec((tm, d), lambda i: (i, 0)),
                  _mod_spec(gt, tm, steps_per_seq), pl.BlockSpec(memory_space=pl.ANY)],
        out_specs=pl.BlockSpec((tm, d), lambda i: (i, 0)),
        scratch_shapes=[pltpu.VMEM((tm * TOP_K * TOKEN_ROWS, LANES), F32), pltpu.SemaphoreType.DMA],
        compiler_params=_cparams("arbitrary"),
        name="moe_combine",
    )(slots_flat, gate4, h, gt, ys)


def _moe(xm_slabs, comb, sel, h, gt, wgu_b, b_gu, wd_b, b_down, tokens_per_seq):
    n, d = h.shape
    tm = min(MOE_TILE, _token_tile(n))
    tme = min(512, _token_tile(n * TOP_K))
    rank, counts = _moe_rank(sel, _token_tile(n))
    tile, e, lo, hi, starts = _moe_plan(counts[0], n * TOP_K, tme)
    starts_row = jnp.zeros((1, LANES), F32).at[0, :N_EXPERTS].set(starts.astype(F32))
    slot4, gate4 = _moe_slots(rank, sel, comb, starts_row, _token_tile(n))
    slots_flat = slot4[:, :TOP_K].reshape(n * TOP_K)
    xs = _moe_dispatch(slots_flat, xm_slabs, tm)
    ys = _moe_ffn((tile, e, lo, hi), xs, wgu_b, b_gu, wd_b, b_down, tme)
    return _moe_combine(slots_flat, gate4, h, gt, ys, tm, max(1, tokens_per_seq // tm))


def _token_tile(n):
    for tm in (512, 256, 128, 64, 32, 16, 8):
        if n % tm == 0:
            return tm
    raise ValueError(f"token count {n} must be a multiple of 8")


def kernel(x_prompt, x_sample, cache_a_k, cache_a_v, cache_idx_k, cache_b_k, cache_b_v, page_table,
           c_prompt, c_sample, w_ada, b_ada, g_attn, w_in, g_qa, g_ka, g_ki, g_qb, g_kb,
           lam_q1, lam_k1, lam_q2, lam_k2, g_sub, w_out, g_ffn, w_router, b_router,
           w_gu, b_gu, w_down, b_down):
    bsz, t_p, d = x_prompt.shape
    db, t_s, _ = x_sample.shape
    depth = w_in.shape[0]
    n_pool, page = cache_a_k.shape[1], cache_a_k.shape[2]
    n_pages = page_table.shape[1]
    past = n_pages * page
    assert page == LANES and t_s == 8 and t_p % Q_BLOCK == 0 and d == 1024
    n_p, n_s = bsz * t_p, db * t_s
    pg_scores, pg_attend = math.gcd(n_pages, 32), math.gcd(n_pages, 16)

    cs_p, sn_p = _rope_tables(jnp.arange(t_p, dtype=I32))
    cs_s, sn_s = _rope_tables(past + jnp.arange(t_s, dtype=I32))
    cs_s, sn_s = jnp.tile(cs_s, (db, 1)), jnp.tile(sn_s, (db, 1))
    tile2 = lambda g: jnp.concatenate([g, g]).astype(F32)
    ksel_p = min(TOPK_MAX, t_p // 4)
    ksel_s = min(TOPK_MAX, (past + t_s) // 4)
    tm_p, tm_s = _token_tile(t_p), _token_tile(n_s)
    sps_p = t_p // tm_p
    per_tok = lambda m: jnp.repeat(m, t_s, axis=0)

    hp = x_prompt.reshape(n_p, d)
    hs = x_sample.reshape(n_s, d)
    rows_p = [[] for _ in range(5)]
    rows_s = [[] for _ in range(5)]
    for l in range(depth):
        lam_init = 0.8 - 0.6 * math.exp(-0.3 * l)
        mods = _adaln(jnp.concatenate([c_prompt, c_sample], axis=0), w_ada[l], b_ada[l])
        mp = [m.reshape(bsz, 1, d) for m in jnp.split(mods[:bsz], 6, axis=-1)]
        ms_ = [per_tok(m) for m in jnp.split(mods[bsz:], 6, axis=-1)]
        w_perm = _permute_w_in(w_in[l])
        gains = jnp.zeros((8, LANES), F32).at[:5].set(
            jnp.stack([tile2(g_qa[l]), tile2(g_ka[l]), tile2(g_ki[l]), tile2(g_qb[l]), tile2(g_kb[l])]))
        lamv = jnp.zeros((8, LANES), F32).at[:4, :HD].set(
            jnp.stack([lam_q1[l], lam_k1[l], lam_q2[l], lam_k2[l]]).astype(F32))
        gsub = g_sub[l].reshape(1, VB).astype(F32)
        w_out_b = w_out[l].astype(BF16)
        w_router_p = jnp.zeros((d, LANES), F32).at[:, :N_EXPERTS].set(w_router[l])
        b_router_p = jnp.full((1, LANES), NEG_BIAS, F32).at[0, :N_EXPERTS].set(b_router[l])
        wgu_b = w_gu[l].astype(BF16)
        wd_b = w_down[l].astype(BF16)

        (qa, qi, qb, ka, kb, ki, wi, va, vb, kab, kbb, kib, vab, vbb) = _proj(
            hp, mp[0], mp[1], g_attn[l], w_perm, cs_p, sn_p, gains, tm_p, sps_p, True)
        r3 = lambda a: a.reshape(bsz, t_p, a.shape[-1])
        oa = _dsa_prompt(r3(qi), r3(wi), r3(kib), r3(qa), r3(kab), r3(vab), ksel_p)
        ob = _diff_prompt(r3(qb), r3(kbb), r3(vbb), lamv, gsub, lam_init)
        h1, xm, comb, sel = _outproj(oa.reshape(n_p, 512), ob.reshape(n_p, 512), hp, mp[2], mp[3], mp[4],
                                     g_ffn[l], w_out_b, w_router_p, b_router_p, tm_p, sps_p)
        hp = _moe(xm, comb, sel, h1, mp[5], wgu_b, b_gu[l], wd_b, b_down[l], t_p)
        for lst, r in zip(rows_p, (ka.reshape(bsz, KVA, HD, t_p).transpose(0, 3, 1, 2),
                                   va.reshape(bsz, KVA, HD, t_p).transpose(0, 3, 1, 2),
                                   ki.transpose(0, 2, 1),
                                   kb.reshape(bsz, KVB, 2, HD, t_p).transpose(0, 4, 1, 2, 3),
                                   vb.reshape(bsz, t_p, KVB, VB))):
            lst.append(r)

        (qa, qi, qb, ka, kb, ki, wi, va, vb, kab, kbb, kib, vab, vbb) = _proj(
            hs, ms_[0], ms_[1], g_attn[l], w_perm, cs_s, sn_s, gains, tm_s, 1, False)
        qm = qi.reshape(db, t_s, H_IDX, D_IDX).transpose(0, 2, 1, 3).reshape(db, 64, D_IDX)
        wcol = jnp.broadcast_to(
            wi[:, :H_IDX].reshape(db, t_s, H_IDX).transpose(0, 2, 1).reshape(db, 64, 1), (db, 64, LANES))
        pad_new = lambda a: jnp.pad(a.reshape(db, t_s, a.shape[-1]), ((0, 0), (0, LANES - t_s), (0, 0)))
        idx_t = jnp.transpose(cache_idx_k[l], (0, 2, 1))
        ak_t = jnp.transpose(cache_a_k[l], (0, 2, 3, 1)).reshape(n_pool, KVA * HD, page)
        av_t = jnp.transpose(cache_a_v[l], (0, 2, 3, 1)).reshape(n_pool, KVA * HD, page)
        bk_t = jnp.transpose(cache_b_k[l], (0, 2, 3, 4, 1)).reshape(n_pool, KVB * 2 * HD, page)
        bv_r = cache_b_v[l].reshape(n_pool, page * KVB, VB)
        sc_past, sc_new = _sample_scores(page_table, qm, wcol, pad_new(kib[:, :D_IDX]), idx_t, pg_scores, t_s)
        sct = jnp.concatenate([sc_past, sc_new], axis=-1).reshape(n_s, n_pages + 1, LANES).transpose(1, 0, 2)
        sel = _sample_select(sct, ksel_s).transpose(1, 0, 2).reshape(db, t_s, (n_pages + 1) * LANES)
        qa4 = qa.reshape(db, t_s, HA, HD).transpose(0, 2, 1, 3)
        zq = jnp.zeros_like(qa4[:, :HA // 2])
        qa_rows = jnp.concatenate([jnp.concatenate([qa4[:, :HA // 2], zq], axis=-1),
                                   jnp.concatenate([zq, qa4[:, HA // 2:]], axis=-1)], axis=1).reshape(db, 64, 128)
        qb5 = qb.reshape(db, t_s, HB, 2, HD).transpose(0, 2, 3, 1, 4)
        zb = jnp.zeros_like(qb5[:, 0, 0])
        blocks = []
        for hh in range(HB):
            for m in range(2):
                pos = (hh // (HB // KVB)) * 2 + m
                blocks.append(jnp.concatenate([qb5[:, hh, m] if p == pos else zb for p in range(4)], axis=-1))
        qb_rows = jnp.stack(blocks, axis=1).reshape(db, 64, 256)
        oa_rows, ob = _sample_attend(
            page_table, qa_rows, qb_rows, sel[:, :, :past], sel[:, :, past:],
            pad_new(kab), pad_new(vab), pad_new(kbb), pad_new(vbb), lamv, gsub,
            ak_t, av_t, bk_t, bv_r, pg_attend, t_s, lam_init)
        oa5 = oa_rows.reshape(db, HA, t_s, KVA, HD)
        oa = jnp.concatenate([oa5[:, :HA // 2, :, 0], oa5[:, HA // 2:, :, 1]], axis=1)
        oa = oa.transpose(0, 2, 1, 3).reshape(n_s, HA * HD)
        h1, xm, comb, sel = _outproj(oa, ob.reshape(n_s, 512), hs, ms_[2], ms_[3], ms_[4],
                                     g_ffn[l], w_out_b, w_router_p, b_router_p, tm_s, 1)
        hs = _moe(xm, comb, sel, h1, ms_[5], wgu_b, b_gu[l], wd_b, b_down[l], t_s)
        for lst, r in zip(rows_s, (ka.reshape(db, t_s, KVA, HD), va.reshape(db, t_s, KVA, HD),
                                   ki.reshape(db, t_s, D_IDX), kb.reshape(db, t_s, KVB, 2, HD),
                                   vb.reshape(db, t_s, KVB, VB))):
            lst.append(r)

    outs_p = [jnp.stack(r) for r in rows_p]
    outs_s = [jnp.stack(r) for r in rows_s]
    return (hp.reshape(bsz, t_p, d), hs.reshape(db, t_s, d), *outs_p, *outs_s)
```

```python
import functools
import math

import jax
import jax.numpy as jnp
from jax import lax
from jax.experimental import pallas as pl
from jax.experimental.pallas import tpu as pltpu

F32 = jnp.float32
BF16 = jnp.bfloat16
I32 = jnp.int32

HD = 64
HA = 8
KVA = 2
H_IDX = 8
D_IDX = 64
HB = 4
KVB = 2
VB = 128
N_EXPERTS = 32
TOP_K = 4
TOPK_MAX = 256
SWIGLU_LIMIT = 7.0
SWIGLU_ALPHA = 1.702
ROPE_THETA = 10000.0
EPS = 1e-6
Q_BLOCK = 256
K_CHUNK = 256
LANES = 128
INT_MIN = -2 ** 31
NEG_BIAS = -1e30
M_INIT = -1e20
VMEM_LIMIT = 56 * 1024 * 1024
Q_SCALE = HD ** -0.5 * math.log2(math.e)

C_QA, C_QI, C_QB, C_KA, C_KB, C_KI, C_WI, C_VA, C_VB, C_END = (
    0, 512, 1024, 1536, 1664, 1920, 2048, 2176, 2304, 2560)


def _dot(a, b):
    return jnp.dot(a, b, preferred_element_type=F32)


def _dot_nt(a, b):
    return lax.dot_general(a, b, (((1,), (1,)), ((), ())), preferred_element_type=F32)


def _split_bf16(a):
    hi = a.astype(BF16)
    lo = (a - hi.astype(F32)).astype(BF16)
    return hi, lo


def _dot3(a, b):
    ah, al = _split_bf16(a)
    bh, bl = _split_bf16(b)
    return _dot(ah, bh) + (_dot(ah, bl) + _dot(al, bh))


def _cparams(*sem):
    return pltpu.CompilerParams(dimension_semantics=sem, vmem_limit_bytes=VMEM_LIMIT)


def _adaln_kernel(c_ref, w_ref, b_ref, o_ref):
    c = c_ref[...]
    s = c * jax.nn.sigmoid(c)
    o_ref[...] = _dot3(s, w_ref[...]) + b_ref[...]


def _adaln(c, w_ada, b_ada):
    n, d = c.shape
    e = w_ada.shape[1]
    tn = 1024
    return pl.pallas_call(
        _adaln_kernel,
        out_shape=jax.ShapeDtypeStruct((n, e), F32),
        grid=(e // tn,),
        in_specs=[pl.BlockSpec((n, d), lambda j: (0, 0)),
                  pl.BlockSpec((d, tn), lambda j: (0, j)),
                  pl.BlockSpec((1, tn), lambda j: (0, j))],
        out_specs=pl.BlockSpec((n, tn), lambda j: (0, j)),
        compiler_params=_cparams("arbitrary"),
        name="adaln",
    )(c, w_ada, b_ada.reshape(1, e))


def _rot_half(v):
    lane = lax.broadcasted_iota(I32, v.shape, 1)
    return jnp.where((lane % HD) < HD // 2, pltpu.roll(v, LANES - HD // 2, 1), pltpu.roll(v, HD // 2, 1))


def _norm_rope(z, c0, width, gain, pn, cs, sn, out_scale):
    outs = []
    for g in range(width // LANES):
        v = z[:, c0 + g * LANES:c0 + (g + 1) * LANES]
        if gain is not None:
            ms = _dot((v * v).astype(BF16), pn)
            v = v * lax.rsqrt(ms + EPS) * gain
        v = v * cs + _rot_half(v) * sn
        if out_scale != 1.0:
            v = v * out_scale
        outs.append(v)
    return outs


def _proj_kernel(x_ref, sh_ref, sc_ref, g_ref, w_ref, cs_ref, sn_ref, gains_ref, pn_ref,
                 qa_ref, qi_ref, qb_ref, ka_ref, kb_ref, ki_ref, wi_ref, va_ref, vb_ref,
                 kab_ref, kbb_ref, kib_ref, vab_ref, vbb_ref, *, token_minor):
    tm = x_ref.shape[0]

    def put(ref, r0, v):
        if token_minor:
            ref[r0:r0 + v.shape[1], :] = v.T
        else:
            ref[:, r0:r0 + v.shape[1]] = v

    x = x_ref[...]
    ms = jnp.mean(x * x, axis=-1, keepdims=True)
    xn = x * lax.rsqrt(ms + EPS) * g_ref[...]
    xn = xn * (1.0 + sc_ref[...]) + sh_ref[...]
    z = _dot(xn.astype(BF16), w_ref[...])
    cs = cs_ref[...]
    sn = sn_ref[...]
    pn = pn_ref[...]
    gains = gains_ref[...]
    g_qa, g_ka, g_ki, g_qb, g_kb = (gains[r:r + 1, :] for r in range(5))

    qa = _norm_rope(z, C_QA, 512, g_qa, pn, cs, sn, Q_SCALE)
    qi = _norm_rope(z, C_QI, 512, None, pn, cs, sn, 1.0)
    qb = _norm_rope(z, C_QB, 512, g_qb, pn, cs, sn, Q_SCALE)
    for g in range(4):
        sl = slice(g * LANES, (g + 1) * LANES)
        qa_ref[:, sl] = qa[g].astype(BF16)
        qi_ref[:, sl] = qi[g].astype(BF16)
        qb_ref[:, sl] = qb[g].astype(BF16)
    ka = _norm_rope(z, C_KA, 128, g_ka, pn, cs, sn, 1.0)[0]
    put(ka_ref, 0, ka)
    kab_ref[...] = ka.astype(BF16)
    kb = _norm_rope(z, C_KB, 256, g_kb, pn, cs, sn, 1.0)
    for g in range(2):
        put(kb_ref, g * LANES, kb[g])
        kbb_ref[:, g * LANES:(g + 1) * LANES] = kb[g].astype(BF16)
    ki = _norm_rope(z, C_KI, 128, g_ki, pn, cs, sn, 1.0)[0]
    if token_minor:
        ki_ref[...] = ki.T[:D_IDX, :]
    else:
        ki_ref[...] = ki[:, :D_IDX]
    kib_ref[...] = ki.astype(BF16)
    wi_ref[...] = z[:, C_WI:C_WI + LANES] * (H_IDX ** -0.5 * D_IDX ** -0.5)
    va = z[:, C_VA:C_VA + 128]
    put(va_ref, 0, va)
    vab_ref[...] = va.astype(BF16)
    vb = z[:, C_VB:C_VB + 256]
    if token_minor:
        for j in range(KVB):
            vb_ref[pl.ds(j, tm, stride=KVB), :] = vb[:, j * VB:(j + 1) * VB]
    else:
        vb_ref[...] = vb
    vbb_ref[...] = vb.astype(BF16)


def _permute_w_in(w_in):
    d = w_in.shape[0]
    seg = lambda a, b: w_in[:, a:b]
    cols = [seg(0, 512), seg(768, 1280), seg(1352, 1864), seg(512, 640), seg(1864, 2120),
            seg(1280, 1344), seg(1280, 1344), seg(1344, 1352), jnp.zeros((d, LANES - H_IDX), w_in.dtype),
            seg(640, 768), seg(2120, 2376)]
    return jnp.concatenate(cols, axis=1).astype(BF16)


def _rope_tables(pos):
    half = HD // 2
    inv = ROPE_THETA ** (-jnp.arange(half, dtype=F32) / half)
    ang = pos.astype(F32)[:, None] * inv[None, :]
    cos, sin = jnp.cos(ang), jnp.sin(ang)
    cs = jnp.concatenate([cos, cos, cos, cos], axis=1)
    sn = jnp.concatenate([-sin, sin, -sin, sin], axis=1)
    return cs, sn


def _head_mean_matrix():
    r = lax.broadcasted_iota(I32, (LANES, LANES), 0) // HD
    c = lax.broadcasted_iota(I32, (LANES, LANES), 1) // HD
    return jnp.where(r == c, 1.0 / HD, 0.0).astype(BF16)


def _mod_spec(arr, tm, steps_per_seq):
    if arr.ndim == 3:
        return pl.BlockSpec((None, 1, arr.shape[-1]), lambda i: (i // steps_per_seq, 0, 0))
    return pl.BlockSpec((tm, arr.shape[-1]), lambda i: (i, 0))


def _proj(x2, sh, sc, g_attn, w_perm, cs, sn, gains, tm, steps_per_seq, token_minor):
    n, d = x2.shape
    if cs.shape[0] == n:
        tab_spec = pl.BlockSpec((tm, LANES), lambda i: (i, 0))
    else:
        tab_spec = pl.BlockSpec((tm, LANES), lambda i: (i % steps_per_seq, 0))
    const = lambda shape: pl.BlockSpec(shape, lambda i: (0,) * len(shape))
    widths = [(512, BF16), (512, BF16), (512, BF16), (128, F32), (256, F32), (D_IDX, F32), (128, F32),
              (128, F32), (256, F32), (128, BF16), (256, BF16), (128, BF16), (128, BF16), (256, BF16)]
    shapes = [jax.ShapeDtypeStruct((n, w), dt) for w, dt in widths]
    specs = [pl.BlockSpec((tm, w), lambda i: (i, 0)) for w, _ in widths]
    if token_minor:
        nseq, t = n // (tm * steps_per_seq), tm * steps_per_seq
        for o in (3, 4, 5, 7):
            w = widths[o][0]
            shapes[o] = jax.ShapeDtypeStruct((nseq, w, t), F32)
            specs[o] = pl.BlockSpec((None, w, tm), lambda i: (i // steps_per_seq, 0, i % steps_per_seq))
        shapes[8] = jax.ShapeDtypeStruct((n * KVB, VB), F32)
        specs[8] = pl.BlockSpec((tm * KVB, VB), lambda i: (i, 0))
    return pl.pallas_call(
        functools.partial(_proj_kernel, token_minor=token_minor),
        out_shape=shapes,
        grid=(n // tm,),
        in_specs=[pl.BlockSpec((tm, d), lambda i: (i, 0)),
                  _mod_spec(sh, tm, steps_per_seq), _mod_spec(sc, tm, steps_per_seq),
                  const((1, d)), const((d, C_END)), tab_spec, tab_spec,
                  const((8, LANES)), const((LANES, LANES))],
        out_specs=specs,
        compiler_params=_cparams("arbitrary"),
        name="proj",
    )(x2, sh, sc, g_attn.reshape(1, d), w_perm, cs, sn, gains, _head_mean_matrix())


def _sortable_key(x):
    bits = lax.bitcast_convert_type(x, I32)
    return bits ^ ((bits >> 31) & 0x7FFFFFFF)


COUNT_ROWS = 128


def _count_keys(pred, skey_ref, nkc, rows, ck, group):
    rb = min(rows, COUNT_ROWS)
    ngroups = (nkc + group - 1) // group
    parts = []
    for r0 in range(0, rows, rb):
        def body(cg, acc, r0=r0):
            for u in range(group):
                c = cg * group + u
                for g in range(ck // LANES):
                    k = skey_ref[c, r0:r0 + rb, g * LANES:(g + 1) * LANES]
                    acc = acc + jnp.where(pred(k, c, g * LANES, r0), 1.0, 0.0)
            return acc
        parts.append(lax.fori_loop(0, ngroups, body, jnp.zeros((rb, LANES), F32)))
    acc = parts[0] if len(parts) == 1 else jnp.concatenate(parts, axis=0)
    return jnp.broadcast_to(jnp.sum(acc, axis=-1, keepdims=True), (rows, LANES))


def _select_topk(skey_ref, nkc, ksel, rows, ck, group=1):
    kf = float(ksel)
    rb = min(rows, COUNT_ROWS)
    strip = lambda a, r0: a[r0:r0 + rb]
    count = functools.partial(_count_keys, skey_ref=skey_ref, nkc=nkc, rows=rows, ck=ck, group=group)

    def bit_body(i, t):
        cand = t + lax.shift_left(jnp.int32(1), 31 - i)
        cnt = count(lambda k, c, c0, r0: k >= strip(cand, r0))
        return jnp.where(cnt >= kf, cand, t)

    thr = lax.fori_loop(0, 32, bit_body, jnp.full((rows, LANES), INT_MIN, I32))
    n_ge = count(lambda k, c, c0, r0: k >= strip(thr, r0))
    n_gt = count(lambda k, c, c0, r0: k > strip(thr, r0))
    excess = jnp.logical_and(n_ge > kf, thr != INT_MIN)
    need = kf - n_gt

    @pl.when(jnp.max(jnp.where(excess, 1.0, 0.0)) > 0.0)
    def _():
        lane = lax.broadcasted_iota(I32, (rb, LANES), 1)
        nbits = max(1, (skey_ref.shape[0] * ck).bit_length())

        def idx_body(i, p):
            cand = p | lax.shift_left(jnp.int32(1), nbits - 1 - i)
            tied_before = lambda k, c, c0, r0: jnp.logical_and(
                k == strip(thr, r0), c * ck + c0 + lane < strip(cand, r0))
            cnt = count(tied_before)
            return jnp.where(cnt <= need, cand, p)

        p_keep = lax.fori_loop(0, nbits, idx_body, jnp.zeros((rows, LANES), I32))

        def drop_body(c, _):
            for r0 in range(0, rows, rb):
                for g in range(ck // LANES):
                    k = skey_ref[c, r0:r0 + rb, g * LANES:(g + 1) * LANES]
                    drop = jnp.logical_and(k == strip(thr, r0), c * ck + g * LANES + lane >= strip(p_keep, r0))
                    drop = jnp.logical_and(drop, strip(excess, r0))
                    skey_ref[c, r0:r0 + rb, g * LANES:(g + 1) * LANES] = jnp.where(drop, INT_MIN, k)
            return 0

        lax.fori_loop(0, nkc, drop_body, 0)

    return jnp.maximum(thr, INT_MIN + 1)


def _flash_init(m_ref, acc_ref):
    m_ref[...] = jnp.full(m_ref.shape, M_INIT, F32)
    acc_ref[...] = jnp.zeros(acc_ref.shape, F32)


def _flash_update(s, pv_fn, m_ref, acc_ref):
    m_prev = m_ref[...]
    m_new = jnp.maximum(m_prev, jnp.max(s, axis=-1, keepdims=True))
    alpha = jnp.exp2(m_prev - m_new)
    rep = lambda a, n: a if n == 1 else jnp.concatenate([a] * n, axis=1)
    p = jnp.exp2(s - rep(m_new, s.shape[1] // LANES)).astype(BF16)
    acc_ref[...] = rep(alpha, acc_ref.shape[1] // LANES) * acc_ref[...] + pv_fn(p)
    m_ref[...] = m_new


def _pv_pages(vaug_list):
    def fn(p):
        upd = None
        for i, va in enumerate(vaug_list):
            nk = va.shape[0]
            d = _dot(p[:, i * nk:(i + 1) * nk], va)
            upd = d if upd is None else upd + d
        return upd
    return fn


def _with_ones(v):
    return jnp.concatenate([v, jnp.ones((v.shape[0], LANES), v.dtype)], axis=1)


def _dsa_kernel(qi_ref, wi_ref, ki_ref, qa_ref, ka_ref, va_ref, o_ref, skey_ref, m_ref, acc_ref,
                *, tq, ck, ksel):
    i = pl.program_id(1)
    nkc = (i + 1) * (tq // ck)
    lane = lax.broadcasted_iota(I32, (tq, LANES), 1)
    hi_half = lane >= HD
    qi = qi_ref[...]
    w = wi_ref[...]
    zero_b = jnp.zeros((tq, LANES), BF16)
    q_heads = []
    for h in range(H_IDX):
        grp = qi[:, (h // 2) * LANES:(h // 2 + 1) * LANES]
        q_heads.append(jnp.where(hi_half if h % 2 else jnp.logical_not(hi_half), grp, zero_b))
    w_heads = [w[:, h:h + 1] for h in range(H_IDX)]
    rows = i * tq + lax.broadcasted_iota(I32, (tq, ck), 0)
    col_in = lax.broadcasted_iota(I32, (tq, ck), 1)

    def score_body(c, _):
        kc = ki_ref[pl.ds(pl.multiple_of(c * ck, ck), ck), :]
        acc = jnp.zeros((tq, ck), F32)
        for h in range(H_IDX):
            acc = acc + jnp.maximum(_dot_nt(q_heads[h], kc), 0.0) * w_heads[h]
        key = _sortable_key(acc)
        skey_ref[c] = jnp.where(c * ck + col_in <= rows, key, INT_MIN)
        return 0

    lax.fori_loop(0, nkc, score_body, 0)
    group = 2 if skey_ref.shape[0] % 2 == 0 else 1

    @pl.when(nkc % group != 0)
    def _():
        skey_ref[nkc] = jnp.full((tq, ck), INT_MIN, I32)

    thr = _select_topk(skey_ref, nkc, ksel, tq, ck, group)
    thr = jnp.concatenate([thr] * (ck // LANES), axis=1)

    qa = qa_ref[...].astype(F32)

    def head_at_half(h, half):
        grp = qa[:, (h // 2) * LANES:(h // 2 + 1) * LANES]
        if h % 2 != half:
            grp = pltpu.roll(grp, HD, 1)
        return jnp.where(hi_half if half else jnp.logical_not(hi_half), grp, 0.0)

    g_per = HA // KVA
    outs = []
    for j in range(KVA):
        q_rows = jnp.concatenate([head_at_half(g_per * j + g, j) for g in range(g_per)], axis=0).astype(BF16)
        _flash_init(m_ref, acc_ref)

        def att_body(c, _):
            off = pl.multiple_of(c * ck, ck)
            s = _dot_nt(q_rows, ka_ref[pl.ds(off, ck), :])
            bias = jnp.where(skey_ref[c] >= thr, 0.0, NEG_BIAS)
            s = (s.reshape(g_per, tq, ck) + bias[None]).reshape(g_per * tq, ck)
            _flash_update(s, _pv_pages([_with_ones(va_ref[pl.ds(off, ck), :])]), m_ref, acc_ref)
            return 0

        lax.fori_loop(0, nkc, att_body, 0)
        acc = acc_ref[...]
        outs.append(acc[:, :LANES] / acc[:, LANES:])

    for gg in range(HA // 2):
        j = (2 * gg) // g_per
        a = outs[j][((2 * gg) % g_per) * tq:((2 * gg) % g_per + 1) * tq]
        b = outs[j][((2 * gg + 1) % g_per) * tq:((2 * gg + 1) % g_per + 1) * tq]
        if j == 1:
            a = pltpu.roll(a, HD, 1)
        else:
            b = pltpu.roll(b, HD, 1)
        o_ref[:, gg * LANES:(gg + 1) * LANES] = jnp.where(hi_half, b, a).astype(o_ref.dtype)


def _dsa_prompt(qi, wi, kib, qa, kab, vab, ksel):
    b, t, _ = qi.shape
    tq, ck = Q_BLOCK, K_CHUNK
    qspec = lambda w: pl.BlockSpec((None, tq, w), lambda bb, i: (bb, i, 0))
    kspec = lambda w: pl.BlockSpec((None, t, w), lambda bb, i: (bb, 0, 0))
    g_per = HA // KVA
    return pl.pallas_call(
        functools.partial(_dsa_kernel, tq=tq, ck=ck, ksel=ksel),
        out_shape=jax.ShapeDtypeStruct((b, t, 512), BF16),
        grid=(b, t // tq),
        in_specs=[qspec(512), qspec(128), kspec(128), qspec(512), kspec(128), kspec(128)],
        out_specs=qspec(512),
        scratch_shapes=[pltpu.VMEM((t // ck, tq, ck), I32),
                        pltpu.VMEM((g_per * tq, LANES), F32),
                        pltpu.VMEM((g_per * tq, 2 * LANES), F32)],
        compiler_params=_cparams("arbitrary", "arbitrary"),
        name="dsa_prompt",
    )(qi, wi, kib, qa, kab, vab)


def _lambda_value(lamv, lam_init):
    l1 = jnp.sum(lamv[0:1, :] * lamv[1:2, :], axis=-1, keepdims=True)
    l2 = jnp.sum(lamv[2:3, :] * lamv[3:4, :], axis=-1, keepdims=True)
    return jnp.exp(l1) - jnp.exp(l2) + lam_init


def _sub_norm(o, gsub, lam_init):
    ms = jnp.mean(o * o, axis=-1, keepdims=True)
    return o * lax.rsqrt(ms + EPS) * gsub * (1.0 - lam_init)


def _diff_kernel(qb_ref, kb_ref, vb_ref, lamv_ref, gsub_ref, o_ref, m_ref, acc_ref, *, tq, ck, lam_init):
    i = pl.program_id(1)
    n_full = i * (tq // ck)
    lane = lax.broadcasted_iota(I32, (tq, LANES), 1)
    hi_half = lane >= HD
    qb = qb_ref[...]
    zero_b = jnp.zeros((tq, LANES), BF16)
    lam = _lambda_value(lamv_ref[...], lam_init)
    gsub = gsub_ref[...]
    g_per = HB // KVB
    r_in = lax.broadcasted_iota(I32, (tq, ck), 0)
    c_in = lax.broadcasted_iota(I32, (tq, ck), 1)

    for j in range(KVB):
        parts = []
        for g in range(g_per):
            grp = qb[:, (g_per * j + g) * LANES:(g_per * j + g + 1) * LANES]
            parts.append(jnp.where(jnp.logical_not(hi_half), grp, zero_b))
            parts.append(jnp.where(hi_half, grp, zero_b))
        q_rows = jnp.concatenate(parts, axis=0)
        kcols = slice(j * LANES, (j + 1) * LANES)
        _flash_init(m_ref, acc_ref)

        def step(c, bias):
            off = pl.multiple_of(c * ck, ck)
            s = _dot_nt(q_rows, kb_ref[pl.ds(off, ck), kcols])
            if bias is not None:
                s = (s.reshape(2 * g_per, tq, ck) + bias[None]).reshape(2 * g_per * tq, ck)
            _flash_update(s, _pv_pages([_with_ones(vb_ref[pl.ds(off, ck), kcols])]), m_ref, acc_ref)

        def full_body(c, _):
            step(c, None)
            return 0

        lax.fori_loop(0, n_full, full_body, 0)
        for d in range(tq // ck):
            bias = jnp.where(d * ck + c_in <= r_in, 0.0, NEG_BIAS)
            step(n_full + d, bias)

        acc = acc_ref[...]
        o = acc[:, :LANES] / acc[:, LANES:]
        for g in range(g_per):
            o1 = o[(2 * g) * tq:(2 * g + 1) * tq]
            o2 = o[(2 * g + 1) * tq:(2 * g + 2) * tq]
            hh = g_per * j + g
            o_ref[:, hh * LANES:(hh + 1) * LANES] = _sub_norm(o1 - lam * o2, gsub, lam_init).astype(o_ref.dtype)


def _diff_prompt(qb, kbb, vbb, lamv, gsub, lam_init):
    b, t, _ = qb.shape
    tq, ck = Q_BLOCK, K_CHUNK
    qspec = lambda w: pl.BlockSpec((None, tq, w), lambda bb, i: (bb, i, 0))
    kspec = lambda w: pl.BlockSpec((None, t, w), lambda bb, i: (bb, 0, 0))
    const = lambda shape: pl.BlockSpec(shape, lambda bb, i: (0,) * len(shape))
    rows = 2 * (HB // KVB) * tq
    return pl.pallas_call(
        functools.partial(_diff_kernel, tq=tq, ck=ck, lam_init=lam_init),
        out_shape=jax.ShapeDtypeStruct((b, t, 512), BF16),
        grid=(b, t // tq),
        in_specs=[qspec(512), kspec(256), kspec(256), const((8, LANES)), const((1, LANES))],
        out_specs=qspec(512),
        scratch_shapes=[pltpu.VMEM((rows, LANES), F32), pltpu.VMEM((rows, 2 * LANES), F32)],
        compiler_params=_cparams("arbitrary", "arbitrary"),
        name="diff_prompt",
    )(qb, kbb, vbb, lamv, gsub)


def _head_sum(r):
    r3 = r.reshape(H_IDX, 8, r.shape[1])
    s = r3[0]
    for h in range(1, H_IDX):
        s = s + r3[h]
    return s + 0.0


def _s1_kernel(pt_ref, qm_ref, wcol_ref, knew_ref, *rest, pg, tnew):
    pages = rest[:pg]
    sc_ref, scn_ref = rest[pg], rest[pg + 1]
    qm = qm_ref[...]
    wcol = wcol_ref[...]
    for i in range(pg):
        s = _dot(qm, pages[i][...].astype(BF16))
        sc_ref[:, i * LANES:(i + 1) * LANES] = _head_sum(jnp.maximum(s, 0.0) * wcol)

    @pl.when(pl.program_id(1) == 0)
    def _():
        s = _dot_nt(qm, knew_ref[...])
        sn = _head_sum(jnp.maximum(s, 0.0) * wcol)
        r = lax.broadcasted_iota(I32, sn.shape, 0)
        c = lax.broadcasted_iota(I32, sn.shape, 1)
        scn_ref[...] = jnp.where(jnp.logical_and(c <= r, c < tnew), sn, -jnp.inf)


def _page_specs(rows, pg):
    return [pl.BlockSpec((None, rows, LANES), functools.partial(lambda b, g, pt, i: (pt[b, g * pg + i], 0, 0), i=i))
            for i in range(pg)]


def _sample_scores(page_table, qm, wcol, knew_i, cik, pg, tnew):
    db, n_pages = page_table.shape
    grid_spec = pltpu.PrefetchScalarGridSpec(
        num_scalar_prefetch=1,
        grid=(db, n_pages // pg),
        in_specs=[pl.BlockSpec((None, 64, D_IDX), lambda b, g, pt: (b, 0, 0)),
                  pl.BlockSpec((None, 64, LANES), lambda b, g, pt: (b, 0, 0)),
                  pl.BlockSpec((None, LANES, D_IDX), lambda b, g, pt: (b, 0, 0))]
                 + _page_specs(D_IDX, pg),
        out_specs=[pl.BlockSpec((None, 8, pg * LANES), lambda b, g, pt: (b, 0, g)),
                   pl.BlockSpec((None, 8, LANES), lambda b, g, pt: (b, 0, 0))],
    )
    return pl.pallas_call(
        functools.partial(_s1_kernel, pg=pg, tnew=tnew),
        out_shape=[jax.ShapeDtypeStruct((db, 8, n_pages * LANES), F32),
                   jax.ShapeDtypeStruct((db, 8, LANES), F32)],
        grid_spec=grid_spec,
        compiler_params=_cparams("arbitrary", "arbitrary"),
        name="sample_scores",
    )(page_table, qm, wcol, knew_i, *([cik] * pg))


def _s1b_kernel(sc_ref, o_ref, skey_ref, *, ksel):
    nc, rows, ck = skey_ref.shape

    def conv(c, _):
        x = sc_ref[c]
        skey_ref[c] = jnp.where(x == -jnp.inf, INT_MIN, _sortable_key(x))
        return 0

    lax.fori_loop(0, nc, conv, 0)
    thr = _select_topk(skey_ref, nc, ksel, rows, ck)

    def emit(c, _):
        o_ref[c] = jnp.where(skey_ref[c] >= thr, 0.0, NEG_BIAS)
        return 0

    lax.fori_loop(0, nc, emit, 0)


def _sample_select(sct, ksel):
    nc, r, _ = sct.shape
    rb = min(r, 128)
    spec = pl.BlockSpec((nc, rb, LANES), lambda i: (0, i, 0))
    return pl.pallas_call(
        functools.partial(_s1b_kernel, ksel=ksel),
        out_shape=jax.ShapeDtypeStruct(sct.shape, F32),
        grid=(r // rb,),
        in_specs=[spec],
        out_specs=spec,
        scratch_shapes=[pltpu.VMEM((nc, rb, LANES), I32)],
        compiler_params=_cparams("arbitrary"),
        name="sample_select",
    )(sct)


def _s2_kernel(pt_ref, qa_ref, qb_ref, selp_ref, seln_ref, kan_ref, van_ref, kbn_ref, vbn_ref,
               lamv_ref, gsub_ref, *rest, pg, tnew, lam_init):
    akp, avp, bkp, bvp = (rest[k * pg:(k + 1) * pg] for k in range(4))
    oa_ref, ob_ref, ma_ref, acca_ref, mb_ref, accb_ref = rest[4 * pg:]
    g = pl.program_id(1)
    qa = qa_ref[...]
    qb = qb_ref[...]

    rows_j = qb.shape[0] // KVB
    ones_kd = jnp.ones((LANES, LANES), BF16)

    def pv_b(values_of_kv):
        def fn(p):
            parts = []
            for j in range(KVB):
                upd = None
                for i in range(p.shape[1] // LANES):
                    va = jnp.concatenate([values_of_kv(i, j), ones_kd], axis=1)
                    d = _dot(p[j * rows_j:(j + 1) * rows_j, i * LANES:(i + 1) * LANES], va)
                    upd = d if upd is None else upd + d
                parts.append(upd)
            return jnp.concatenate(parts, axis=0)
        return fn

    @pl.when(g == 0)
    def _():
        _flash_init(ma_ref, acca_ref)
        _flash_init(mb_ref, accb_ref)
        sa = _dot_nt(qa, kan_ref[...]) + jnp.tile(seln_ref[...], (HA, 1))
        _flash_update(sa, _pv_pages([_with_ones(van_ref[...])]), ma_ref, acca_ref)
        r = lax.broadcasted_iota(I32, (8, LANES), 0)
        c = lax.broadcasted_iota(I32, (8, LANES), 1)
        causal = jnp.where(jnp.logical_and(c <= r, c < tnew), 0.0, NEG_BIAS)
        sb = _dot_nt(qb, kbn_ref[...]) + jnp.tile(causal, (8, 1))
        _flash_update(sb, pv_b(lambda i, j: vbn_ref[:, j * LANES:(j + 1) * LANES]), mb_ref, accb_ref)

    sa = jnp.concatenate([_dot(qa, akp[i][...].astype(BF16)) for i in range(pg)], axis=1)
    sa = sa + jnp.tile(selp_ref[...], (HA, 1))

    def pv_a(p):
        upd = None
        for i in range(pg):
            vt = jnp.concatenate([avp[i][...].astype(BF16), ones_kd], axis=0)
            d = _dot_nt(p[:, i * LANES:(i + 1) * LANES], vt)
            upd = d if upd is None else upd + d
        return upd

    _flash_update(sa, pv_a, ma_ref, acca_ref)
    sb = jnp.concatenate([_dot(qb, bkp[i][...].astype(BF16)) for i in range(pg)], axis=1)
    _flash_update(sb, pv_b(lambda i, j: bvp[i][pl.ds(j, LANES, stride=KVB), :].astype(BF16)), mb_ref, accb_ref)

    @pl.when(g == pl.num_programs(1) - 1)
    def _():
        acca = acca_ref[...]
        oa_ref[...] = acca[:, :LANES] / acca[:, LANES:]
        accb = accb_ref[...]
        x = (accb[:, :LANES] / accb[:, LANES:]).reshape(HB, 2, 8, LANES)
        lam = _lambda_value(lamv_ref[...], lam_init)
        gsub = gsub_ref[...]
        for hh in range(HB):
            ob_ref[:, hh * LANES:(hh + 1) * LANES] = _sub_norm(x[hh, 0] - lam * x[hh, 1], gsub, lam_init)


def _sample_attend(page_table, qa_rows, qb_rows, selp, seln, kan, van, kbn, vbn, lamv, gsub,
                   cak, cav, cbk, cbv, pg, tnew, lam_init):
    db, n_pages = page_table.shape
    per_b = lambda *shape: pl.BlockSpec((None,) + shape, lambda b, g, pt: (b,) + (0,) * len(shape))
    const = lambda shape: pl.BlockSpec(shape, lambda b, g, pt: (0,) * len(shape))
    grid_spec = pltpu.PrefetchScalarGridSpec(
        num_scalar_prefetch=1,
        grid=(db, n_pages // pg),
        in_specs=[per_b(64, 128), per_b(64, 256),
                  pl.BlockSpec((None, 8, pg * LANES), lambda b, g, pt: (b, 0, g)), per_b(8, LANES),
                  per_b(LANES, 128), per_b(LANES, 128), per_b(LANES, 256), per_b(LANES, 256),
                  const((8, LANES)), const((1, LANES))]
                 + _page_specs(128, pg) + _page_specs(128, pg)
                 + _page_specs(256, pg) + _page_specs(256, pg),
        out_specs=[per_b(64, 128), per_b(8, 512)],
        scratch_shapes=[pltpu.VMEM((64, LANES), F32), pltpu.VMEM((64, 2 * LANES), F32),
                        pltpu.VMEM((64, LANES), F32), pltpu.VMEM((64, 2 * LANES), F32)],
    )
    return pl.pallas_call(
        functools.partial(_s2_kernel, pg=pg, tnew=tnew, lam_init=lam_init),
        out_shape=[jax.ShapeDtypeStruct((db, 64, 128), F32), jax.ShapeDtypeStruct((db, 8, 512), F32)],
        grid_spec=grid_spec,
        compiler_params=_cparams("arbitrary", "arbitrary"),
        name="sample_attend",
    )(page_table, qa_rows, qb_rows, selp, seln, kan, van, kbn, vbn, lamv, gsub,
      *([cak] * pg), *([cav] * pg), *([cbk] * pg), *([cbv] * pg))


TOKEN_ROWS = 8


def _outproj_kernel(oa_ref, ob_ref, x_ref, gt_ref, sh_ref, sc_ref, g_ref, wo_ref, wr_ref, br_ref,
                    h_ref, xm_ref, comb_ref, sel_ref):
    half = oa_ref.shape[1]
    tm = x_ref.shape[0]
    o = _dot(oa_ref[...].astype(BF16), wo_ref[:half, :]) + _dot(ob_ref[...].astype(BF16), wo_ref[half:, :])
    h = x_ref[...] + gt_ref[...] * o
    h_ref[...] = h
    ms = jnp.mean(h * h, axis=-1, keepdims=True)
    xm = h * lax.rsqrt(ms + EPS) * g_ref[...]
    xm = xm * (1.0 + sc_ref[...]) + sh_ref[...]
    for c in range(TOKEN_ROWS):
        xm_ref[pl.ds(c, tm, stride=TOKEN_ROWS), :] = xm[:, c * LANES:(c + 1) * LANES]
    logits = _dot3(xm, wr_ref[...]) + br_ref[...]
    lane = lax.broadcasted_iota(I32, logits.shape, 1).astype(F32)
    vals, hots = [], []
    for _ in range(TOP_K):
        m = jnp.max(logits, axis=-1, keepdims=True)
        idx = jnp.min(jnp.where(logits == m, lane, float(LANES)), axis=-1, keepdims=True)
        hot = lane == idx
        vals.append(m)
        hots.append(hot)
        logits = jnp.where(hot, -3e38, logits)
    es = [jnp.exp(v - vals[0]) for v in vals]
    den = es[0] + es[1] + es[2] + es[3]
    comb = jnp.zeros(logits.shape, F32)
    sel = jnp.zeros(logits.shape, F32)
    for e, hot in zip(es, hots):
        comb = comb + jnp.where(hot, e / den, 0.0)
        sel = sel + jnp.where(hot, 1.0, 0.0)
    comb_ref[...] = comb
    sel_ref[...] = sel.astype(BF16)


def _outproj(oa, ob, x2, gt, sh, sc, g_ffn, w_out_b, w_router_p, b_router_p, tm, steps_per_seq):
    n, d = x2.shape
    assert d == TOKEN_ROWS * LANES
    const = lambda shape: pl.BlockSpec(shape, lambda i: (0,) * len(shape))
    row = lambda w: pl.BlockSpec((tm, w), lambda i: (i, 0))
    return pl.pallas_call(
        _outproj_kernel,
        out_shape=[jax.ShapeDtypeStruct((n, d), F32), jax.ShapeDtypeStruct((n * TOKEN_ROWS, LANES), F32),
                   jax.ShapeDtypeStruct((n, LANES), F32), jax.ShapeDtypeStruct((n, LANES), BF16)],
        grid=(n // tm,),
        in_specs=[row(oa.shape[1]), row(ob.shape[1]), row(d),
                  _mod_spec(gt, tm, steps_per_seq), _mod_spec(sh, tm, steps_per_seq),
                  _mod_spec(sc, tm, steps_per_seq),
                  const((1, d)), const(w_out_b.shape), const((d, LANES)), const((1, LANES))],
        out_specs=[row(d), pl.BlockSpec((tm * TOKEN_ROWS, LANES), lambda i: (i, 0)), row(LANES), row(LANES)],
        compiler_params=_cparams("arbitrary"),
        name="outproj",
    )(oa, ob, x2, gt, sh, sc, g_ffn.reshape(1, d), w_out_b, w_router_p, b_router_p)


MOE_TILE = 256
TAB_ROWS = 8


def _rank_kernel(sel_ref, tri_ref, before_ref, base_ref, cnt_ref, total_ref, carry_ref):
    @pl.when(pl.program_id(0) == 0)
    def _():
        carry_ref[...] = jnp.zeros(carry_ref.shape, F32)

    sel = sel_ref[...]
    tm = sel.shape[0]
    before = _dot(tri_ref[...], sel)
    before_ref[...] = before
    carry = carry_ref[...]
    base_ref[...] = carry
    cnt = jnp.broadcast_to(before[tm - 1:tm, :] + sel[tm - 1:tm, :].astype(F32), carry.shape)
    cnt_ref[...] = cnt
    carry = carry + cnt
    carry_ref[...] = carry
    total_ref[...] = carry


def _moe_rank(sel, tm):
    n = sel.shape[0]
    r = lax.broadcasted_iota(I32, (tm, tm), 0)
    c = lax.broadcasted_iota(I32, (tm, tm), 1)
    tri = jnp.where(c < r, 1.0, 0.0).astype(BF16)
    per_tile = pl.BlockSpec((8, LANES), lambda i: (i, 0))
    return pl.pallas_call(
        _rank_kernel,
        out_shape=[jax.ShapeDtypeStruct((n, LANES), F32), jax.ShapeDtypeStruct((n // tm * 8, LANES), F32),
                   jax.ShapeDtypeStruct((n // tm * 8, LANES), F32), jax.ShapeDtypeStruct((8, LANES), F32)],
        grid=(n // tm,),
        in_specs=[pl.BlockSpec((tm, LANES), lambda i: (i, 0)), pl.BlockSpec((tm, tm), lambda i: (0, 0))],
        out_specs=[pl.BlockSpec((tm, LANES), lambda i: (i, 0)), per_tile, per_tile,
                   pl.BlockSpec((8, LANES), lambda i: (0, 0))],
        scratch_shapes=[pltpu.VMEM((8, LANES), F32)],
        compiler_params=_cparams("arbitrary"),
        name="moe_rank",
    )(sel, tri)


def _slots_kernel(before_ref, sel_ref, comb_ref, base_ref, cnt_ref, starts_ref, upper_ref,
                  pos_ref, gate_ref, tab_ref):
    cnt = cnt_ref[...]
    offs = _dot(cnt.astype(BF16), upper_ref[...])
    posv = before_ref[...] + offs[0:1, :]
    comb = comb_ref[...]
    sel = sel_ref[...].astype(F32) > 0.0
    lane = lax.broadcasted_iota(I32, posv.shape, 1)
    lanef = lane.astype(F32)
    pos = jnp.zeros(posv.shape, F32)
    gates = jnp.zeros(posv.shape, F32)
    for k in range(TOP_K):
        idx = jnp.min(jnp.where(sel, lanef, float(LANES)), axis=-1, keepdims=True)
        hot = lanef == idx
        pk = jnp.sum(jnp.where(hot, posv, 0.0), axis=-1, keepdims=True)
        gk = jnp.sum(jnp.where(hot, comb, 0.0), axis=-1, keepdims=True)
        pos = jnp.where(lane == k, pk, pos)
        gates = jnp.where(lane == k, gk, gates)
        sel = jnp.logical_and(sel, jnp.logical_not(hot))
    pos_ref[...] = pos.astype(I32)
    gate_ref[...] = gates
    first = starts_ref[...] + base_ref[0:1, :]
    tab = jnp.concatenate([first, cnt[0:1, :], offs[0:1, :], jnp.zeros((TAB_ROWS - 3, LANES), F32)], axis=0)
    tab_ref[...] = tab.astype(I32)


def _moe_slots(before, sel, comb, base, cnt, starts, tm):
    n = before.shape[0]
    row = pl.BlockSpec((tm, LANES), lambda i: (i, 0))
    per_tile = pl.BlockSpec((8, LANES), lambda i: (i, 0))
    r = lax.broadcasted_iota(I32, (LANES, LANES), 0)
    c = lax.broadcasted_iota(I32, (LANES, LANES), 1)
    upper = jnp.where(r < c, 1.0, 0.0).astype(BF16)
    return pl.pallas_call(
        _slots_kernel,
        out_shape=[jax.ShapeDtypeStruct((n, LANES), I32), jax.ShapeDtypeStruct((n, LANES), F32),
                   jax.ShapeDtypeStruct((n // tm * TAB_ROWS, LANES), I32)],
        grid=(n // tm,),
        in_specs=[row, row, row, per_tile, per_tile, pl.BlockSpec((1, LANES), lambda i: (0, 0)),
                  pl.BlockSpec((LANES, LANES), lambda i: (0, 0))],
        out_specs=[row, row, pl.BlockSpec((TAB_ROWS, LANES), lambda i: (i, 0))],
        compiler_params=_cparams("arbitrary"),
        name="moe_slots",
    )(before, sel, comb, base, cnt, starts, upper)


def _slab_rows(ref, index, count=1):
    return ref.at[pl.ds(pl.multiple_of(index * TOKEN_ROWS, TOKEN_ROWS), count * TOKEN_ROWS), :]


def _run_copies(src_ref, dst_ref, src0, dst0, length, max_len, sem, wait):
    for b in range(max_len.bit_length() - 1, -1, -1):
        size = 1 << b

        @pl.when((length & size) != 0)
        def _():
            done = (length >> (b + 1)) << (b + 1)
            cp = pltpu.make_async_copy(_slab_rows(src_ref, src0 + done, size),
                                       _slab_rows(dst_ref, dst0 + done, size), sem)
            if wait:
                cp.wait()
            else:
                cp.start()


def _for_each_run(tab_ref, fn):
    def body(e, carry):
        fn(tab_ref[e], tab_ref[LANES + e], tab_ref[2 * LANES + e])
        return carry
    lax.fori_loop(0, N_EXPERTS, body, 0)


def _dispatch_kernel(pos_ref, tab_ref, x_ref, xs_ref, stage_ref, sem):
    tm = x_ref.shape[0] // TOKEN_ROWS

    def place(t, carry):
        v = _slab_rows(x_ref, t)[...]
        for k in range(TOP_K):
            _slab_rows(stage_ref, pos_ref[t * TOP_K + k])[...] = v
        return carry

    lax.fori_loop(0, tm, place, 0)
    for wait in (False, True):
        _for_each_run(tab_ref, lambda first, length, off, wait=wait: _run_copies(
            stage_ref, xs_ref, off, first, length, tm, sem, wait))


def _smem_words(words):
    return pl.BlockSpec((words,), lambda i: (i,), memory_space=pltpu.SMEM)


def _moe_dispatch(pos_flat, tab_flat, xm_slabs, tm):
    n = xm_slabs.shape[0] // TOKEN_ROWS
    return pl.pallas_call(
        _dispatch_kernel,
        out_shape=jax.ShapeDtypeStruct((n * TOP_K * TOKEN_ROWS, LANES), F32),
        grid=(n // tm,),
        in_specs=[_smem_words(tm * TOP_K), _smem_words(TAB_ROWS * LANES),
                  pl.BlockSpec((tm * TOKEN_ROWS, LANES), lambda i: (i, 0))],
        out_specs=pl.BlockSpec(memory_space=pl.ANY),
        scratch_shapes=[pltpu.VMEM((tm * TOP_K * TOKEN_ROWS, LANES), F32), pltpu.SemaphoreType.DMA],
        compiler_params=_cparams("arbitrary"),
        name="moe_dispatch",
    )(pos_flat, tab_flat, xm_slabs)


def _ffn_kernel(tile_ref, exp_ref, lo_ref, hi_ref, xs_ref, wgu_ref, bgu_ref, wd_ref, bd_ref, ys_ref):
    g = pl.program_id(0)
    lo, hi = lo_ref[g], hi_ref[g]
    tme = xs_ref.shape[0] // TOKEN_ROWS
    dff = wd_ref.shape[0]

    @pl.when(hi > lo)
    def _():
        x = jnp.concatenate([xs_ref[pl.ds(c, tme, stride=TOKEN_ROWS), :] for c in range(TOKEN_ROWS)], axis=1)
        hgu = _dot(x.astype(BF16), wgu_ref[...]) + bgu_ref[...]
        gate = jnp.minimum(hgu[:, :dff], SWIGLU_LIMIT)
        up = jnp.clip(hgu[:, dff:], -SWIGLU_LIMIT, SWIGLU_LIMIT)
        a = (up + 1.0) * (gate * jax.nn.sigmoid(SWIGLU_ALPHA * gate))
        yo = _dot(a.astype(BF16), wd_ref[...]) + bd_ref[...]
        row = lax.broadcasted_iota(I32, (tme, LANES), 0)
        mine = jnp.logical_and(row >= lo, row < hi)

        @pl.when(lo == 0)
        def _():
            for c in range(TOKEN_ROWS):
                ys_ref[pl.ds(c, tme, stride=TOKEN_ROWS), :] = yo[:, c * LANES:(c + 1) * LANES]

        @pl.when(lo != 0)
        def _():
            for c in range(TOKEN_ROWS):
                dst = ys_ref.at[pl.ds(c, tme, stride=TOKEN_ROWS), :]
                dst[...] = jnp.where(mine, yo[:, c * LANES:(c + 1) * LANES], dst[...])


def _moe_plan(counts, n_slots, tme):
    cnt = counts[:N_EXPERTS].astype(I32)
    ends = jnp.cumsum(cnt)
    starts = ends - cnt
    first_tile = starts // tme
    n_items = jnp.where(cnt > 0, (ends - 1) // tme - first_tile + 1, 0)
    item_end = jnp.cumsum(n_items)
    item_start = item_end - n_items
    n_work = n_slots // tme + N_EXPERTS - 1
    g = jnp.arange(n_work, dtype=I32)
    gi = jnp.minimum(g, item_end[-1] - 1)
    e = jnp.sum((item_end[None, :] <= gi[:, None]).astype(I32), axis=1)
    tile = first_tile[e] + (gi - item_start[e])
    lo = jnp.maximum(starts[e], tile * tme) - tile * tme
    hi = jnp.minimum(ends[e], (tile + 1) * tme) - tile * tme
    real = g < item_end[-1]
    return tile, e, jnp.where(real, lo, 0), jnp.where(real, hi, 0), starts


def _moe_ffn(plan, xs, wgu_b, b_gu, wd_b, b_down, tme):
    tile, e, lo, hi = plan
    ne, d, dff2 = wgu_b.shape
    dff = dff2 // 2
    slab_spec = pl.BlockSpec((tme * TOKEN_ROWS, LANES), lambda g, tile, e, lo, hi: (tile[g], 0))
    per_e = lambda *shape: pl.BlockSpec((None,) + shape, lambda g, tile, e, lo, hi: (e[g],) + (0,) * len(shape))
    grid_spec = pltpu.PrefetchScalarGridSpec(
        num_scalar_prefetch=4,
        grid=(tile.shape[0],),
        in_specs=[slab_spec, per_e(d, dff2), per_e(1, dff2), per_e(dff, d), per_e(1, d)],
        out_specs=slab_spec,
    )
    return pl.pallas_call(
        _ffn_kernel,
        out_shape=jax.ShapeDtypeStruct(xs.shape, F32),
        grid_spec=grid_spec,
        compiler_params=_cparams("arbitrary"),
        name="moe_ffn",
    )(tile, e, lo, hi, xs, wgu_b, b_gu.reshape(ne, 1, dff2), wd_b, b_down.reshape(ne, 1, d))


def _combine_kernel(pos_ref, gate_ref, tab_ref, h_ref, gt_ref, ys_ref, y_ref, stage_ref, acc_ref, sem):
    tm = h_ref.shape[0]
    for wait in (False, True):
        _for_each_run(tab_ref, lambda first, length, off, wait=wait: _run_copies(
            ys_ref, stage_ref, first, off, length, tm, sem, wait))

    def gather(t, carry):
        acc = None
        for k in range(TOP_K):
            part = gate_ref[t * TOP_K + k] * _slab_rows(stage_ref, pos_ref[t * TOP_K + k])[...]
            acc = part if acc is None else acc + part
        _slab_rows(acc_ref, t)[...] = acc
        return carry

    lax.fori_loop(0, tm, gather, 0)
    gt = gt_ref[...]
    for c in range(TOKEN_ROWS):
        sl = slice(c * LANES, (c + 1) * LANES)
        y_ref[:, sl] = h_ref[:, sl] + gt[:, sl] * acc_ref[pl.ds(c, tm, stride=TOKEN_ROWS), :]


def _moe_combine(pos_flat, gate_flat, tab_flat, h, gt, ys, tm, steps_per_seq):
    n, d = h.shape
    return pl.pallas_call(
        _combine_kernel,
        out_shape=jax.ShapeDtypeStruct((n, d), F32),
        grid=(n // tm,),
        in_specs=[_smem_words(tm * TOP_K), _smem_words(tm * TOP_K), _smem_words(TAB_ROWS * LANES),
                  pl.BlockSpec((tm, d), lambda i: (i, 0)),
                  _mod_spec(gt, tm, steps_per_seq), pl.BlockSpec(memory_space=pl.ANY)],
        out_specs=pl.BlockSpec((tm, d), lambda i: (i, 0)),
        scratch_shapes=[pltpu.VMEM((tm * TOP_K * TOKEN_ROWS, LANES), F32),
                        pltpu.VMEM((tm * TOKEN_ROWS, LANES), F32), pltpu.SemaphoreType.DMA],
        compiler_params=_cparams("arbitrary"),
        name="moe_combine",
    )(pos_flat, gate_flat, tab_flat, h, gt, ys)


def _moe(xm_slabs, comb, sel, h, gt, wgu_b, b_gu, wd_b, b_down, tokens_per_seq):
    n, d = h.shape
    tm = min(MOE_TILE, _token_tile(n))
    tme = min(512, _token_tile(n * TOP_K))
    before, base, cnt, total = _moe_rank(sel, tm)
    tile, e, lo, hi, starts = _moe_plan(total[0], n * TOP_K, tme)
    starts_row = jnp.zeros((1, LANES), F32).at[0, :N_EXPERTS].set(starts.astype(F32))
    pos4, gate4, tab = _moe_slots(before, sel, comb, base, cnt, starts_row, tm)
    pos_flat = pos4[:, :TOP_K].reshape(n * TOP_K)
    gate_flat = gate4[:, :TOP_K].reshape(n * TOP_K)
    tab_flat = tab.reshape(-1)
    xs = _moe_dispatch(pos_flat, tab_flat, xm_slabs, tm)
    ys = _moe_ffn((tile, e, lo, hi), xs, wgu_b, b_gu, wd_b, b_down, tme)
    return _moe_combine(pos_flat, gate_flat, tab_flat, h, gt, ys, tm, max(1, tokens_per_seq // tm))


def _token_tile(n):
    for tm in (512, 256, 128, 64, 32, 16, 8):
        if n % tm == 0:
            return tm
    raise ValueError(f"token count {n} must be a multiple of 8")


def kernel(x_prompt, x_sample, cache_a_k, cache_a_v, cache_idx_k, cache_b_k, cache_b_v, page_table,
           c_prompt, c_sample, w_ada, b_ada, g_attn, w_in, g_qa, g_ka, g_ki, g_qb, g_kb,
           lam_q1, lam_k1, lam_q2, lam_k2, g_sub, w_out, g_ffn, w_router, b_router,
           w_gu, b_gu, w_down, b_down):
    bsz, t_p, d = x_prompt.shape
    db, t_s, _ = x_sample.shape
    depth = w_in.shape[0]
    n_pool, page = cache_a_k.shape[1], cache_a_k.shape[2]
    n_pages = page_table.shape[1]
    past = n_pages * page
    assert page == LANES and t_s == 8 and t_p % Q_BLOCK == 0 and d == 1024
    n_p, n_s = bsz * t_p, db * t_s
    pg_scores, pg_attend = math.gcd(n_pages, 32), math.gcd(n_pages, 16)

    cs_p, sn_p = _rope_tables(jnp.arange(t_p, dtype=I32))
    cs_s, sn_s = _rope_tables(past + jnp.arange(t_s, dtype=I32))
    cs_s, sn_s = jnp.tile(cs_s, (db, 1)), jnp.tile(sn_s, (db, 1))
    tile2 = lambda g: jnp.concatenate([g, g]).astype(F32)
    ksel_p = min(TOPK_MAX, t_p // 4)
    ksel_s = min(TOPK_MAX, (past + t_s) // 4)
    tm_p, tm_s = _token_tile(t_p), _token_tile(n_s)
    sps_p = t_p // tm_p
    per_tok = lambda m: jnp.repeat(m, t_s, axis=0)

    hp = x_prompt.reshape(n_p, d)
    hs = x_sample.reshape(n_s, d)
    rows_p = [[] for _ in range(5)]
    rows_s = [[] for _ in range(5)]
    for l in range(depth):
        lam_init = 0.8 - 0.6 * math.exp(-0.3 * l)
        mods = _adaln(jnp.concatenate([c_prompt, c_sample], axis=0), w_ada[l], b_ada[l])
        mp = [m.reshape(bsz, 1, d) for m in jnp.split(mods[:bsz], 6, axis=-1)]
        ms_ = [per_tok(m) for m in jnp.split(mods[bsz:], 6, axis=-1)]
        w_perm = _permute_w_in(w_in[l])
        gains = jnp.zeros((8, LANES), F32).at[:5].set(
            jnp.stack([tile2(g_qa[l]), tile2(g_ka[l]), tile2(g_ki[l]), tile2(g_qb[l]), tile2(g_kb[l])]))
        lamv = jnp.zeros((8, LANES), F32).at[:4, :HD].set(
            jnp.stack([lam_q1[l], lam_k1[l], lam_q2[l], lam_k2[l]]).astype(F32))
        gsub = g_sub[l].reshape(1, VB).astype(F32)
        w_out_b = w_out[l].astype(BF16)
        w_router_p = jnp.zeros((d, LANES), F32).at[:, :N_EXPERTS].set(w_router[l])
        b_router_p = jnp.full((1, LANES), NEG_BIAS, F32).at[0, :N_EXPERTS].set(b_router[l])
        wgu_b = w_gu[l].astype(BF16)
        wd_b = w_down[l].astype(BF16)

        (qa, qi, qb, ka, kb, ki, wi, va, vb, kab, kbb, kib, vab, vbb) = _proj(
            hp, mp[0], mp[1], g_attn[l], w_perm, cs_p, sn_p, gains, tm_p, sps_p, True)
        r3 = lambda a: a.reshape(bsz, t_p, a.shape[-1])
        oa = _dsa_prompt(r3(qi), r3(wi), r3(kib), r3(qa), r3(kab), r3(vab), ksel_p)
        ob = _diff_prompt(r3(qb), r3(kbb), r3(vbb), lamv, gsub, lam_init)
        h1, xm, comb, sel = _outproj(oa.reshape(n_p, 512), ob.reshape(n_p, 512), hp, mp[2], mp[3], mp[4],
                                     g_ffn[l], w_out_b, w_router_p, b_router_p, tm_p, sps_p)
        hp = _moe(xm, comb, sel, h1, mp[5], wgu_b, b_gu[l], wd_b, b_down[l], t_p)
        for lst, r in zip(rows_p, (ka.reshape(bsz, KVA, HD, t_p).transpose(0, 3, 1, 2),
                                   va.reshape(bsz, KVA, HD, t_p).transpose(0, 3, 1, 2),
                                   ki.transpose(0, 2, 1),
                                   kb.reshape(bsz, KVB, 2, HD, t_p).transpose(0, 4, 1, 2, 3),
                                   vb.reshape(bsz, t_p, KVB, VB))):
            lst.append(r)

        (qa, qi, qb, ka, kb, ki, wi, va, vb, kab, kbb, kib, vab, vbb) = _proj(
            hs, ms_[0], ms_[1], g_attn[l], w_perm, cs_s, sn_s, gains, tm_s, 1, False)
        qm = qi.reshape(db, t_s, H_IDX, D_IDX).transpose(0, 2, 1, 3).reshape(db, 64, D_IDX)
        wcol = jnp.broadcast_to(
            wi[:, :H_IDX].reshape(db, t_s, H_IDX).transpose(0, 2, 1).reshape(db, 64, 1), (db, 64, LANES))
        pad_new = lambda a: jnp.pad(a.reshape(db, t_s, a.shape[-1]), ((0, 0), (0, LANES - t_s), (0, 0)))
        idx_t = jnp.transpose(cache_idx_k[l], (0, 2, 1))
        ak_t = jnp.transpose(cache_a_k[l], (0, 2, 3, 1)).reshape(n_pool, KVA * HD, page)
        av_t = jnp.transpose(cache_a_v[l], (0, 2, 3, 1)).reshape(n_pool, KVA * HD, page)
        bk_t = jnp.transpose(cache_b_k[l], (0, 2, 3, 4, 1)).reshape(n_pool, KVB * 2 * HD, page)
        bv_r = cache_b_v[l].reshape(n_pool, page * KVB, VB)
        sc_past, sc_new = _sample_scores(page_table, qm, wcol, pad_new(kib[:, :D_IDX]), idx_t, pg_scores, t_s)
        sct = jnp.concatenate([sc_past, sc_new], axis=-1).reshape(n_s, n_pages + 1, LANES).transpose(1, 0, 2)
        sel = _sample_select(sct, ksel_s).transpose(1, 0, 2).reshape(db, t_s, (n_pages + 1) * LANES)
        qa4 = qa.reshape(db, t_s, HA, HD).transpose(0, 2, 1, 3)
        zq = jnp.zeros_like(qa4[:, :HA // 2])
        qa_rows = jnp.concatenate([jnp.concatenate([qa4[:, :HA // 2], zq], axis=-1),
                                   jnp.concatenate([zq, qa4[:, HA // 2:]], axis=-1)], axis=1).reshape(db, 64, 128)
        qb5 = qb.reshape(db, t_s, HB, 2, HD).transpose(0, 2, 3, 1, 4)
        zb = jnp.zeros_like(qb5[:, 0, 0])
        blocks = []
        for hh in range(HB):
            for m in range(2):
                pos = (hh // (HB // KVB)) * 2 + m
                blocks.append(jnp.concatenate([qb5[:, hh, m] if p == pos else zb for p in range(4)], axis=-1))
        qb_rows = jnp.stack(blocks, axis=1).reshape(db, 64, 256)
        oa_rows, ob = _sample_attend(
            page_table, qa_rows, qb_rows, sel[:, :, :past], sel[:, :, past:],
            pad_new(kab), pad_new(vab), pad_new(kbb), pad_new(vbb), lamv, gsub,
            ak_t, av_t, bk_t, bv_r, pg_attend, t_s, lam_init)
        oa5 = oa_rows.reshape(db, HA, t_s, KVA, HD)
        oa = jnp.concatenate([oa5[:, :HA // 2, :, 0], oa5[:, HA // 2:, :, 1]], axis=1)
        oa = oa.transpose(0, 2, 1, 3).reshape(n_s, HA * HD)
        h1, xm, comb, sel = _outproj(oa, ob.reshape(n_s, 512), hs, ms_[2], ms_[3], ms_[4],
                                     g_ffn[l], w_out_b, w_router_p, b_router_p, tm_s, 1)
        hs = _moe(xm, comb, sel, h1, ms_[5], wgu_b, b_gu[l], wd_b, b_down[l], t_s)
        for lst, r in zip(rows_s, (ka.reshape(db, t_s, KVA, HD), va.reshape(db, t_s, KVA, HD),
                                   ki.reshape(db, t_s, D_IDX), kb.reshape(db, t_s, KVB, 2, HD),
                                   vb.reshape(db, t_s, KVB, VB))):
            lst.append(r)

    outs_p = [jnp.stack(r) for r in rows_p]
    outs_s = [jnp.stack(r) for r in rows_s]
    return (hp.reshape(bsz, t_p, d), hs.reshape(db, t_s, d), *outs_p, *outs_s)
```

```python
import functools
import math

import jax
import jax.numpy as jnp
from jax import lax
from jax.experimental import pallas as pl
from jax.experimental.pallas import tpu as pltpu

F32 = jnp.float32
BF16 = jnp.bfloat16
I32 = jnp.int32

HD = 64
HA = 8
KVA = 2
H_IDX = 8
D_IDX = 64
HB = 4
KVB = 2
VB = 128
N_EXPERTS = 32
TOP_K = 4
TOPK_MAX = 256
SWIGLU_LIMIT = 7.0
SWIGLU_ALPHA = 1.702
ROPE_THETA = 10000.0
EPS = 1e-6
Q_BLOCK = 256
DSA_Q_BLOCK = 512
K_CHUNK = 256
LANES = 128
INT_MIN = -2 ** 31
NEG_BIAS = -1e30
M_INIT = -1e20
VMEM_LIMIT = 56 * 1024 * 1024
Q_SCALE = HD ** -0.5 * math.log2(math.e)

C_QA, C_QI, C_QB, C_KA, C_KB, C_KI, C_WI, C_VA, C_VB, C_END = (
    0, 512, 1024, 1536, 1664, 1920, 2048, 2176, 2304, 2560)


def _dot(a, b):
    return jnp.dot(a, b, preferred_element_type=F32)


def _dot_nt(a, b):
    return lax.dot_general(a, b, (((1,), (1,)), ((), ())), preferred_element_type=F32)


def _split_bf16(a):
    hi = a.astype(BF16)
    lo = (a - hi.astype(F32)).astype(BF16)
    return hi, lo


def _dot3(a, b):
    ah, al = _split_bf16(a)
    bh, bl = _split_bf16(b)
    return _dot(ah, bh) + (_dot(ah, bl) + _dot(al, bh))


def _cparams(*sem):
    return pltpu.CompilerParams(dimension_semantics=sem, vmem_limit_bytes=VMEM_LIMIT)


def _adaln_kernel(c_ref, w_ref, b_ref, o_ref):
    c = c_ref[...]
    s = c * jax.nn.sigmoid(c)
    o_ref[...] = _dot3(s, w_ref[...]) + b_ref[...]


def _adaln(c, w_ada, b_ada):
    n, d = c.shape
    e = w_ada.shape[1]
    tn = 1024
    return pl.pallas_call(
        _adaln_kernel,
        out_shape=jax.ShapeDtypeStruct((n, e), F32),
        grid=(e // tn,),
        in_specs=[pl.BlockSpec((n, d), lambda j: (0, 0)),
                  pl.BlockSpec((d, tn), lambda j: (0, j)),
                  pl.BlockSpec((1, tn), lambda j: (0, j))],
        out_specs=pl.BlockSpec((n, tn), lambda j: (0, j)),
        compiler_params=_cparams("arbitrary"),
        name="adaln",
    )(c, w_ada, b_ada.reshape(1, e))


def _rot_half(v):
    lane = lax.broadcasted_iota(I32, v.shape, 1)
    return jnp.where((lane % HD) < HD // 2, pltpu.roll(v, LANES - HD // 2, 1), pltpu.roll(v, HD // 2, 1))


def _norm_rope(z, c0, width, gain, pn, cs, sn, out_scale):
    outs = []
    for g in range(width // LANES):
        v = z[:, c0 + g * LANES:c0 + (g + 1) * LANES]
        if gain is not None:
            ms = _dot((v * v).astype(BF16), pn)
            v = v * lax.rsqrt(ms + EPS) * gain
        v = v * cs + _rot_half(v) * sn
        if out_scale != 1.0:
            v = v * out_scale
        outs.append(v)
    return outs


def _proj_kernel(x_ref, sh_ref, sc_ref, g_ref, w_ref, cs_ref, sn_ref, gains_ref, pn_ref,
                 qa_ref, qi_ref, qb_ref, ka_ref, kb_ref, ki_ref, wi_ref, va_ref, vb_ref,
                 kab_ref, kbb_ref, kib_ref, vab_ref, vbb_ref, *, token_minor):
    tm = x_ref.shape[0]

    def put(ref, r0, v):
        if token_minor:
            ref[r0:r0 + v.shape[1], :] = v.T
        else:
            ref[:, r0:r0 + v.shape[1]] = v

    x = x_ref[...]
    ms = jnp.mean(x * x, axis=-1, keepdims=True)
    xn = x * lax.rsqrt(ms + EPS) * g_ref[...]
    xn = xn * (1.0 + sc_ref[...]) + sh_ref[...]
    z = _dot(xn.astype(BF16), w_ref[...])
    cs = cs_ref[...]
    sn = sn_ref[...]
    pn = pn_ref[...]
    gains = gains_ref[...]
    g_qa, g_ka, g_ki, g_qb, g_kb = (gains[r:r + 1, :] for r in range(5))

    qa = _norm_rope(z, C_QA, 512, g_qa, pn, cs, sn, Q_SCALE)
    qi = _norm_rope(z, C_QI, 512, None, pn, cs, sn, 1.0)
    qb = _norm_rope(z, C_QB, 512, g_qb, pn, cs, sn, Q_SCALE)
    for g in range(4):
        sl = slice(g * LANES, (g + 1) * LANES)
        qa_ref[:, sl] = qa[g].astype(BF16)
        qi_ref[:, sl] = qi[g].astype(BF16)
        qb_ref[:, sl] = qb[g].astype(BF16)
    ka = _norm_rope(z, C_KA, 128, g_ka, pn, cs, sn, 1.0)[0]
    put(ka_ref, 0, ka)
    kab_ref[...] = ka.astype(BF16)
    kb = _norm_rope(z, C_KB, 256, g_kb, pn, cs, sn, 1.0)
    for g in range(2):
        put(kb_ref, g * LANES, kb[g])
        kbb_ref[:, g * LANES:(g + 1) * LANES] = kb[g].astype(BF16)
    ki = _norm_rope(z, C_KI, 128, g_ki, pn, cs, sn, 1.0)[0]
    if token_minor:
        ki_ref[...] = ki.T[:D_IDX, :]
    else:
        ki_ref[...] = ki[:, :D_IDX]
    kib_ref[...] = ki.astype(BF16)
    wi_ref[...] = z[:, C_WI:C_WI + LANES] * (H_IDX ** -0.5 * D_IDX ** -0.5)
    va = z[:, C_VA:C_VA + 128]
    put(va_ref, 0, va)
    vab_ref[...] = va.astype(BF16)
    vb = z[:, C_VB:C_VB + 256]
    if token_minor:
        for j in range(KVB):
            vb_ref[pl.ds(j, tm, stride=KVB), :] = vb[:, j * VB:(j + 1) * VB]
    else:
        vb_ref[...] = vb
    vbb_ref[...] = vb.astype(BF16)


def _permute_w_in(w_in):
    d = w_in.shape[0]
    seg = lambda a, b: w_in[:, a:b]
    cols = [seg(0, 512), seg(768, 1280), seg(1352, 1864), seg(512, 640), seg(1864, 2120),
            seg(1280, 1344), seg(1280, 1344), seg(1344, 1352), jnp.zeros((d, LANES - H_IDX), w_in.dtype),
            seg(640, 768), seg(2120, 2376)]
    return jnp.concatenate(cols, axis=1).astype(BF16)


def _rope_tables(pos):
    half = HD // 2
    inv = ROPE_THETA ** (-jnp.arange(half, dtype=F32) / half)
    ang = pos.astype(F32)[:, None] * inv[None, :]
    cos, sin = jnp.cos(ang), jnp.sin(ang)
    cs = jnp.concatenate([cos, cos, cos, cos], axis=1)
    sn = jnp.concatenate([-sin, sin, -sin, sin], axis=1)
    return cs, sn


def _head_mean_matrix():
    r = lax.broadcasted_iota(I32, (LANES, LANES), 0) // HD
    c = lax.broadcasted_iota(I32, (LANES, LANES), 1) // HD
    return jnp.where(r == c, 1.0 / HD, 0.0).astype(BF16)


def _mod_spec(arr, tm, steps_per_seq):
    if arr.ndim == 3:
        return pl.BlockSpec((None, 1, arr.shape[-1]), lambda i: (i // steps_per_seq, 0, 0))
    return pl.BlockSpec((tm, arr.shape[-1]), lambda i: (i, 0))


def _proj(x2, sh, sc, g_attn, w_perm, cs, sn, gains, tm, steps_per_seq, token_minor):
    n, d = x2.shape
    if cs.shape[0] == n:
        tab_spec = pl.BlockSpec((tm, LANES), lambda i: (i, 0))
    else:
        tab_spec = pl.BlockSpec((tm, LANES), lambda i: (i % steps_per_seq, 0))
    const = lambda shape: pl.BlockSpec(shape, lambda i: (0,) * len(shape))
    widths = [(512, BF16), (512, BF16), (512, BF16), (128, F32), (256, F32), (D_IDX, F32), (128, F32),
              (128, F32), (256, F32), (128, BF16), (256, BF16), (128, BF16), (128, BF16), (256, BF16)]
    shapes = [jax.ShapeDtypeStruct((n, w), dt) for w, dt in widths]
    specs = [pl.BlockSpec((tm, w), lambda i: (i, 0)) for w, _ in widths]
    if token_minor:
        nseq, t = n // (tm * steps_per_seq), tm * steps_per_seq
        for o in (3, 4, 5, 7):
            w = widths[o][0]
            shapes[o] = jax.ShapeDtypeStruct((nseq, w, t), F32)
            specs[o] = pl.BlockSpec((None, w, tm), lambda i: (i // steps_per_seq, 0, i % steps_per_seq))
        shapes[8] = jax.ShapeDtypeStruct((n * KVB, VB), F32)
        specs[8] = pl.BlockSpec((tm * KVB, VB), lambda i: (i, 0))
    return pl.pallas_call(
        functools.partial(_proj_kernel, token_minor=token_minor),
        out_shape=shapes,
        grid=(n // tm,),
        in_specs=[pl.BlockSpec((tm, d), lambda i: (i, 0)),
                  _mod_spec(sh, tm, steps_per_seq), _mod_spec(sc, tm, steps_per_seq),
                  const((1, d)), const((d, C_END)), tab_spec, tab_spec,
                  const((8, LANES)), const((LANES, LANES))],
        out_specs=specs,
        compiler_params=_cparams("arbitrary"),
        name="proj",
    )(x2, sh, sc, g_attn.reshape(1, d), w_perm, cs, sn, gains, _head_mean_matrix())


def _sortable_key(x):
    bits = lax.bitcast_convert_type(x, I32)
    return bits ^ ((bits >> 31) & 0x7FFFFFFF)


COUNT_ROWS = 128


def _count_keys(pred, skey_ref, nkc, rows, ck, group):
    rb = min(rows, COUNT_ROWS)
    ngroups = (nkc + group - 1) // group
    parts = []
    for r0 in range(0, rows, rb):
        def body(cg, acc, r0=r0):
            for u in range(group):
                c = cg * group + u
                for g in range(ck // LANES):
                    k = skey_ref[c, r0:r0 + rb, g * LANES:(g + 1) * LANES]
                    acc = acc + jnp.where(pred(k, c, g * LANES, r0), 1.0, 0.0)
            return acc
        parts.append(lax.fori_loop(0, ngroups, body, jnp.zeros((rb, LANES), F32)))
    acc = parts[0] if len(parts) == 1 else jnp.concatenate(parts, axis=0)
    return jnp.broadcast_to(jnp.sum(acc, axis=-1, keepdims=True), (rows, LANES))


def _select_topk(skey_ref, nkc, ksel, rows, ck, group=1):
    kf = float(ksel)
    rb = min(rows, COUNT_ROWS)
    strip = lambda a, r0: a[r0:r0 + rb]
    count = functools.partial(_count_keys, skey_ref=skey_ref, nkc=nkc, rows=rows, ck=ck, group=group)

    def bit_body(i, t):
        cand = t + lax.shift_left(jnp.int32(1), 31 - i)
        cnt = count(lambda k, c, c0, r0: k >= strip(cand, r0))
        return jnp.where(cnt >= kf, cand, t)

    thr = lax.fori_loop(0, 32, bit_body, jnp.full((rows, LANES), INT_MIN, I32))
    n_ge = count(lambda k, c, c0, r0: k >= strip(thr, r0))
    n_gt = count(lambda k, c, c0, r0: k > strip(thr, r0))
    excess = jnp.logical_and(n_ge > kf, thr != INT_MIN)
    need = kf - n_gt

    @pl.when(jnp.max(jnp.where(excess, 1.0, 0.0)) > 0.0)
    def _():
        lane = lax.broadcasted_iota(I32, (rb, LANES), 1)
        nbits = max(1, (skey_ref.shape[0] * ck).bit_length())

        def idx_body(i, p):
            cand = p | lax.shift_left(jnp.int32(1), nbits - 1 - i)
            tied_before = lambda k, c, c0, r0: jnp.logical_and(
                k == strip(thr, r0), c * ck + c0 + lane < strip(cand, r0))
            cnt = count(tied_before)
            return jnp.where(cnt <= need, cand, p)

        p_keep = lax.fori_loop(0, nbits, idx_body, jnp.zeros((rows, LANES), I32))

        def drop_body(c, _):
            for r0 in range(0, rows, rb):
                for g in range(ck // LANES):
                    k = skey_ref[c, r0:r0 + rb, g * LANES:(g + 1) * LANES]
                    drop = jnp.logical_and(k == strip(thr, r0), c * ck + g * LANES + lane >= strip(p_keep, r0))
                    drop = jnp.logical_and(drop, strip(excess, r0))
                    skey_ref[c, r0:r0 + rb, g * LANES:(g + 1) * LANES] = jnp.where(drop, INT_MIN, k)
            return 0

        lax.fori_loop(0, nkc, drop_body, 0)

    return jnp.maximum(thr, INT_MIN + 1)


def _flash_init(m_ref, acc_ref):
    m_ref[...] = jnp.full(m_ref.shape, M_INIT, F32)
    acc_ref[...] = jnp.zeros(acc_ref.shape, F32)


def _flash_update(s, pv_fn, m_ref, acc_ref):
    m_prev = m_ref[...]
    m_new = jnp.maximum(m_prev, jnp.max(s, axis=-1, keepdims=True))
    alpha = jnp.exp2(m_prev - m_new)
    rep = lambda a, n: a if n == 1 else jnp.concatenate([a] * n, axis=1)
    p = jnp.exp2(s - rep(m_new, s.shape[1] // LANES)).astype(BF16)
    acc_ref[...] = rep(alpha, acc_ref.shape[1] // LANES) * acc_ref[...] + pv_fn(p)
    m_ref[...] = m_new


def _pv_pages(vaug_list):
    def fn(p):
        upd = None
        for i, va in enumerate(vaug_list):
            nk = va.shape[0]
            d = _dot(p[:, i * nk:(i + 1) * nk], va)
            upd = d if upd is None else upd + d
        return upd
    return fn


def _with_ones(v):
    return jnp.concatenate([v, jnp.ones((v.shape[0], LANES), v.dtype)], axis=1)


def _dsa_kernel(qi_ref, wi_ref, ki_ref, qa_ref, ka_ref, va_ref, o_ref, skey_ref, m_ref, acc_ref,
                *, tq, ck, ksel):
    i = pl.program_id(1)
    nkc = (i + 1) * (tq // ck)
    lane = lax.broadcasted_iota(I32, (tq, LANES), 1)
    hi_half = lane >= HD
    qi = qi_ref[...]
    w = wi_ref[...]
    zero_b = jnp.zeros((tq, LANES), BF16)
    q_heads = []
    for h in range(H_IDX):
        grp = qi[:, (h // 2) * LANES:(h // 2 + 1) * LANES]
        q_heads.append(jnp.where(hi_half if h % 2 else jnp.logical_not(hi_half), grp, zero_b))
    w_heads = [w[:, h:h + 1] for h in range(H_IDX)]
    rows = i * tq + lax.broadcasted_iota(I32, (tq, ck), 0)
    col_in = lax.broadcasted_iota(I32, (tq, ck), 1)

    def score_body(c, _):
        kc = ki_ref[pl.ds(pl.multiple_of(c * ck, ck), ck), :]
        acc = jnp.zeros((tq, ck), F32)
        for h in range(H_IDX):
            acc = acc + jnp.maximum(_dot_nt(q_heads[h], kc), 0.0) * w_heads[h]
        key = _sortable_key(acc)
        skey_ref[c] = jnp.where(c * ck + col_in <= rows, key, INT_MIN)
        return 0

    lax.fori_loop(0, nkc, score_body, 0)
    group = 2 if skey_ref.shape[0] % 2 == 0 else 1

    @pl.when(nkc % group != 0)
    def _():
        skey_ref[nkc] = jnp.full((tq, ck), INT_MIN, I32)

    thr = _select_topk(skey_ref, nkc, ksel, tq, ck, group)
    thr = jnp.concatenate([thr] * (ck // LANES), axis=1)

    qa = qa_ref[...].astype(F32)

    def head_at_half(h, half):
        grp = qa[:, (h // 2) * LANES:(h // 2 + 1) * LANES]
        if h % 2 != half:
            grp = pltpu.roll(grp, HD, 1)
        return jnp.where(hi_half if half else jnp.logical_not(hi_half), grp, 0.0)

    g_per = HA // KVA
    outs = []
    for j in range(KVA):
        q_rows = jnp.concatenate([head_at_half(g_per * j + g, j) for g in range(g_per)], axis=0).astype(BF16)
        _flash_init(m_ref, acc_ref)
        own_half = (lax.broadcasted_iota(I32, (ck, LANES), 1) >= HD) == bool(j)
        one_b = jnp.ones((ck, LANES), BF16)

        def att_body(c, _):
            off = pl.multiple_of(c * ck, ck)
            s = _dot_nt(q_rows, ka_ref[pl.ds(off, ck), :])
            bias = jnp.where(skey_ref[c] >= thr, 0.0, NEG_BIAS)
            s = (s.reshape(g_per, tq, ck) + bias[None]).reshape(g_per * tq, ck)
            va = jnp.where(own_half, va_ref[pl.ds(off, ck), :], one_b)
            _flash_update(s, _pv_pages([va]), m_ref, acc_ref)
            return 0

        lax.fori_loop(0, nkc, att_body, 0)
        acc = acc_ref[...]
        outs.append(acc / pltpu.roll(acc, HD, 1))

    for gg in range(HA // 2):
        j = (2 * gg) // g_per
        a = outs[j][((2 * gg) % g_per) * tq:((2 * gg) % g_per + 1) * tq]
        b = outs[j][((2 * gg + 1) % g_per) * tq:((2 * gg + 1) % g_per + 1) * tq]
        if j == 1:
            a = pltpu.roll(a, HD, 1)
        else:
            b = pltpu.roll(b, HD, 1)
        o_ref[:, gg * LANES:(gg + 1) * LANES] = jnp.where(hi_half, b, a).astype(o_ref.dtype)


def _dsa_prompt(qi, wi, kib, qa, kab, vab, ksel):
    b, t, _ = qi.shape
    tq, ck = (DSA_Q_BLOCK if t % DSA_Q_BLOCK == 0 else Q_BLOCK), K_CHUNK
    qspec = lambda w: pl.BlockSpec((None, tq, w), lambda bb, i: (bb, i, 0))
    kspec = lambda w: pl.BlockSpec((None, t, w), lambda bb, i: (bb, 0, 0))
    g_per = HA // KVA
    return pl.pallas_call(
        functools.partial(_dsa_kernel, tq=tq, ck=ck, ksel=ksel),
        out_shape=jax.ShapeDtypeStruct((b, t, 512), BF16),
        grid=(b, t // tq),
        in_specs=[qspec(512), qspec(128), kspec(128), qspec(512), kspec(128), kspec(128)],
        out_specs=qspec(512),
        scratch_shapes=[pltpu.VMEM((t // ck, tq, ck), I32),
                        pltpu.VMEM((g_per * tq, LANES), F32),
                        pltpu.VMEM((g_per * tq, LANES), F32)],
        compiler_params=_cparams("arbitrary", "arbitrary"),
        name="dsa_prompt",
    )(qi, wi, kib, qa, kab, vab)


def _lambda_value(lamv, lam_init):
    l1 = jnp.sum(lamv[0:1, :] * lamv[1:2, :], axis=-1, keepdims=True)
    l2 = jnp.sum(lamv[2:3, :] * lamv[3:4, :], axis=-1, keepdims=True)
    return jnp.exp(l1) - jnp.exp(l2) + lam_init


def _sub_norm(o, gsub, lam_init):
    ms = jnp.mean(o * o, axis=-1, keepdims=True)
    return o * lax.rsqrt(ms + EPS) * gsub * (1.0 - lam_init)


def _diff_kernel(qb_ref, kb_ref, vb_ref, lamv_ref, gsub_ref, o_ref, m_ref, acc_ref, *, tq, ck, lam_init):
    i = pl.program_id(1)
    n_full = i * (tq // ck)
    lane = lax.broadcasted_iota(I32, (tq, LANES), 1)
    hi_half = lane >= HD
    qb = qb_ref[...]
    zero_b = jnp.zeros((tq, LANES), BF16)
    lam = _lambda_value(lamv_ref[...], lam_init)
    gsub = gsub_ref[...]
    g_per = HB // KVB
    r_in = lax.broadcasted_iota(I32, (tq, ck), 0)
    c_in = lax.broadcasted_iota(I32, (tq, ck), 1)

    for j in range(KVB):
        parts = []
        for g in range(g_per):
            grp = qb[:, (g_per * j + g) * LANES:(g_per * j + g + 1) * LANES]
            parts.append(jnp.where(jnp.logical_not(hi_half), grp, zero_b))
            parts.append(jnp.where(hi_half, grp, zero_b))
        q_rows = jnp.concatenate(parts, axis=0)
        kcols = slice(j * LANES, (j + 1) * LANES)
        _flash_init(m_ref, acc_ref)

        def step(c, bias):
            off = pl.multiple_of(c * ck, ck)
            s = _dot_nt(q_rows, kb_ref[pl.ds(off, ck), kcols])
            if bias is not None:
                s = (s.reshape(2 * g_per, tq, ck) + bias[None]).reshape(2 * g_per * tq, ck)
            _flash_update(s, _pv_pages([_with_ones(vb_ref[pl.ds(off, ck), kcols])]), m_ref, acc_ref)

        def full_body(c, _):
            step(c, None)
            return 0

        lax.fori_loop(0, n_full, full_body, 0)
        for d in range(tq // ck):
            bias = jnp.where(d * ck + c_in <= r_in, 0.0, NEG_BIAS)
            step(n_full + d, bias)

        acc = acc_ref[...]
        o = acc[:, :LANES] / acc[:, LANES:]
        for g in range(g_per):
            o1 = o[(2 * g) * tq:(2 * g + 1) * tq]
            o2 = o[(2 * g + 1) * tq:(2 * g + 2) * tq]
            hh = g_per * j + g
            o_ref[:, hh * LANES:(hh + 1) * LANES] = _sub_norm(o1 - lam * o2, gsub, lam_init).astype(o_ref.dtype)


def _diff_prompt(qb, kbb, vbb, lamv, gsub, lam_init):
    b, t, _ = qb.shape
    tq, ck = Q_BLOCK, K_CHUNK
    qspec = lambda w: pl.BlockSpec((None, tq, w), lambda bb, i: (bb, i, 0))
    kspec = lambda w: pl.BlockSpec((None, t, w), lambda bb, i: (bb, 0, 0))
    const = lambda shape: pl.BlockSpec(shape, lambda bb, i: (0,) * len(shape))
    rows = 2 * (HB // KVB) * tq
    return pl.pallas_call(
        functools.partial(_diff_kernel, tq=tq, ck=ck, lam_init=lam_init),
        out_shape=jax.ShapeDtypeStruct((b, t, 512), BF16),
        grid=(b, t // tq),
        in_specs=[qspec(512), kspec(256), kspec(256), const((8, LANES)), const((1, LANES))],
        out_specs=qspec(512),
        scratch_shapes=[pltpu.VMEM((rows, LANES), F32), pltpu.VMEM((rows, 2 * LANES), F32)],
        compiler_params=_cparams("arbitrary", "arbitrary"),
        name="diff_prompt",
    )(qb, kbb, vbb, lamv, gsub)


def _head_sum(r):
    r3 = r.reshape(H_IDX, 8, r.shape[1])
    s = r3[0]
    for h in range(1, H_IDX):
        s = s + r3[h]
    return s + 0.0


def _s1_kernel(pt_ref, qm_ref, wcol_ref, knew_ref, *rest, pg, tnew):
    pages = rest[:pg]
    sc_ref, scn_ref = rest[pg], rest[pg + 1]
    qm = qm_ref[...]
    wcol = wcol_ref[...]
    for i in range(pg):
        s = _dot(qm, pages[i][...].astype(BF16))
        sc_ref[:, i * LANES:(i + 1) * LANES] = _head_sum(jnp.maximum(s, 0.0) * wcol)

    @pl.when(pl.program_id(1) == 0)
    def _():
        s = _dot_nt(qm, knew_ref[...])
        sn = _head_sum(jnp.maximum(s, 0.0) * wcol)
        r = lax.broadcasted_iota(I32, sn.shape, 0)
        c = lax.broadcasted_iota(I32, sn.shape, 1)
        scn_ref[...] = jnp.where(jnp.logical_and(c <= r, c < tnew), sn, -jnp.inf)


def _page_specs(rows, pg):
    return [pl.BlockSpec((None, rows, LANES), functools.partial(lambda b, g, pt, i: (pt[b, g * pg + i], 0, 0), i=i))
            for i in range(pg)]


def _sample_scores(page_table, qm, wcol, knew_i, cik, pg, tnew):
    db, n_pages = page_table.shape
    grid_spec = pltpu.PrefetchScalarGridSpec(
        num_scalar_prefetch=1,
        grid=(db, n_pages // pg),
        in_specs=[pl.BlockSpec((None, 64, D_IDX), lambda b, g, pt: (b, 0, 0)),
                  pl.BlockSpec((None, 64, LANES), lambda b, g, pt: (b, 0, 0)),
                  pl.BlockSpec((None, LANES, D_IDX), lambda b, g, pt: (b, 0, 0))]
                 + _page_specs(D_IDX, pg),
        out_specs=[pl.BlockSpec((None, 8, pg * LANES), lambda b, g, pt: (b, 0, g)),
                   pl.BlockSpec((None, 8, LANES), lambda b, g, pt: (b, 0, 0))],
    )
    return pl.pallas_call(
        functools.partial(_s1_kernel, pg=pg, tnew=tnew),
        out_shape=[jax.ShapeDtypeStruct((db, 8, n_pages * LANES), F32),
                   jax.ShapeDtypeStruct((db, 8, LANES), F32)],
        grid_spec=grid_spec,
        compiler_params=_cparams("arbitrary", "arbitrary"),
        name="sample_scores",
    )(page_table, qm, wcol, knew_i, *([cik] * pg))


def _s1b_kernel(sc_ref, o_ref, skey_ref, *, ksel):
    nc, rows, ck = skey_ref.shape

    def conv(c, _):
        x = sc_ref[c]
        skey_ref[c] = jnp.where(x == -jnp.inf, INT_MIN, _sortable_key(x))
        return 0

    lax.fori_loop(0, nc, conv, 0)
    thr = _select_topk(skey_ref, nc, ksel, rows, ck)

    def emit(c, _):
        o_ref[c] = jnp.where(skey_ref[c] >= thr, 0.0, NEG_BIAS)
        return 0

    lax.fori_loop(0, nc, emit, 0)


def _sample_select(sct, ksel):
    nc, r, _ = sct.shape
    rb = min(r, 128)
    spec = pl.BlockSpec((nc, rb, LANES), lambda i: (0, i, 0))
    return pl.pallas_call(
        functools.partial(_s1b_kernel, ksel=ksel),
        out_shape=jax.ShapeDtypeStruct(sct.shape, F32),
        grid=(r // rb,),
        in_specs=[spec],
        out_specs=spec,
        scratch_shapes=[pltpu.VMEM((nc, rb, LANES), I32)],
        compiler_params=_cparams("arbitrary"),
        name="sample_select",
    )(sct)


def _s2_kernel(pt_ref, qa_ref, qb_ref, selp_ref, seln_ref, kan_ref, van_ref, kbn_ref, vbn_ref,
               lamv_ref, gsub_ref, *rest, pg, tnew, lam_init):
    akp, avp, bkp, bvp = (rest[k * pg:(k + 1) * pg] for k in range(4))
    oa_ref, ob_ref, ma_ref, acca_ref, mb_ref, accb_ref = rest[4 * pg:]
    g = pl.program_id(1)
    qa = qa_ref[...]
    qb = qb_ref[...]

    rows_j = qb.shape[0] // KVB
    ones_kd = jnp.ones((LANES, LANES), BF16)

    def pv_b(values_of_kv):
        def fn(p):
            parts = []
            for j in range(KVB):
                upd = None
                for i in range(p.shape[1] // LANES):
                    va = jnp.concatenate([values_of_kv(i, j), ones_kd], axis=1)
                    d = _dot(p[j * rows_j:(j + 1) * rows_j, i * LANES:(i + 1) * LANES], va)
                    upd = d if upd is None else upd + d
                parts.append(upd)
            return jnp.concatenate(parts, axis=0)
        return fn

    @pl.when(g == 0)
    def _():
        _flash_init(ma_ref, acca_ref)
        _flash_init(mb_ref, accb_ref)
        sa = _dot_nt(qa, kan_ref[...]) + jnp.tile(seln_ref[...], (HA, 1))
        _flash_update(sa, _pv_pages([_with_ones(van_ref[...])]), ma_ref, acca_ref)
        r = lax.broadcasted_iota(I32, (8, LANES), 0)
        c = lax.broadcasted_iota(I32, (8, LANES), 1)
        causal = jnp.where(jnp.logical_and(c <= r, c < tnew), 0.0, NEG_BIAS)
        sb = _dot_nt(qb, kbn_ref[...]) + jnp.tile(causal, (8, 1))
        _flash_update(sb, pv_b(lambda i, j: vbn_ref[:, j * LANES:(j + 1) * LANES]), mb_ref, accb_ref)

    sa = jnp.concatenate([_dot(qa, akp[i][...].astype(BF16)) for i in range(pg)], axis=1)
    sa = sa + jnp.tile(selp_ref[...], (HA, 1))

    def pv_a(p):
        upd = None
        for i in range(pg):
            vt = jnp.concatenate([avp[i][...].astype(BF16), ones_kd], axis=0)
            d = _dot_nt(p[:, i * LANES:(i + 1) * LANES], vt)
            upd = d if upd is None else upd + d
        return upd

    _flash_update(sa, pv_a, ma_ref, acca_ref)
    sb = jnp.concatenate([_dot(qb, bkp[i][...].astype(BF16)) for i in range(pg)], axis=1)
    _flash_update(sb, pv_b(lambda i, j: bvp[i][pl.ds(j, LANES, stride=KVB), :].astype(BF16)), mb_ref, accb_ref)

    @pl.when(g == pl.num_programs(1) - 1)
    def _():
        acca = acca_ref[...]
        oa_ref[...] = acca[:, :LANES] / acca[:, LANES:]
        accb = accb_ref[...]
        x = (accb[:, :LANES] / accb[:, LANES:]).reshape(HB, 2, 8, LANES)
        lam = _lambda_value(lamv_ref[...], lam_init)
        gsub = gsub_ref[...]
        for hh in range(HB):
            ob_ref[:, hh * LANES:(hh + 1) * LANES] = _sub_norm(x[hh, 0] - lam * x[hh, 1], gsub, lam_init)


def _sample_attend(page_table, qa_rows, qb_rows, selp, seln, kan, van, kbn, vbn, lamv, gsub,
                   cak, cav, cbk, cbv, pg, tnew, lam_init):
    db, n_pages = page_table.shape
    per_b = lambda *shape: pl.BlockSpec((None,) + shape, lambda b, g, pt: (b,) + (0,) * len(shape))
    const = lambda shape: pl.BlockSpec(shape, lambda b, g, pt: (0,) * len(shape))
    grid_spec = pltpu.PrefetchScalarGridSpec(
        num_scalar_prefetch=1,
        grid=(db, n_pages // pg),
        in_specs=[per_b(64, 128), per_b(64, 256),
                  pl.BlockSpec((None, 8, pg * LANES), lambda b, g, pt: (b, 0, g)), per_b(8, LANES),
                  per_b(LANES, 128), per_b(LANES, 128), per_b(LANES, 256), per_b(LANES, 256),
                  const((8, LANES)), const((1, LANES))]
                 + _page_specs(128, pg) + _page_specs(128, pg)
                 + _page_specs(256, pg) + _page_specs(256, pg),
        out_specs=[per_b(64, 128), per_b(8, 512)],
        scratch_shapes=[pltpu.VMEM((64, LANES), F32), pltpu.VMEM((64, 2 * LANES), F32),
                        pltpu.VMEM((64, LANES), F32), pltpu.VMEM((64, 2 * LANES), F32)],
    )
    return pl.pallas_call(
        functools.partial(_s2_kernel, pg=pg, tnew=tnew, lam_init=lam_init),
        out_shape=[jax.ShapeDtypeStruct((db, 64, 128), F32), jax.ShapeDtypeStruct((db, 8, 512), F32)],
        grid_spec=grid_spec,
        compiler_params=_cparams("arbitrary", "arbitrary"),
        name="sample_attend",
    )(page_table, qa_rows, qb_rows, selp, seln, kan, van, kbn, vbn, lamv, gsub,
      *([cak] * pg), *([cav] * pg), *([cbk] * pg), *([cbv] * pg))


TOKEN_ROWS = 8


def _outproj_kernel(oa_ref, ob_ref, x_ref, gt_ref, sh_ref, sc_ref, g_ref, wo_ref, wr_ref, br_ref,
                    h_ref, xm_ref, comb_ref, sel_ref):
    half = oa_ref.shape[1]
    tm = x_ref.shape[0]
    o = _dot(oa_ref[...].astype(BF16), wo_ref[:half, :]) + _dot(ob_ref[...].astype(BF16), wo_ref[half:, :])
    h = x_ref[...] + gt_ref[...] * o
    h_ref[...] = h
    ms = jnp.mean(h * h, axis=-1, keepdims=True)
    xm = h * lax.rsqrt(ms + EPS) * g_ref[...]
    xm = xm * (1.0 + sc_ref[...]) + sh_ref[...]
    for c in range(TOKEN_ROWS):
        xm_ref[pl.ds(c, tm, stride=TOKEN_ROWS), :] = xm[:, c * LANES:(c + 1) * LANES]
    logits = _dot3(xm, wr_ref[...]) + br_ref[...]
    lane = lax.broadcasted_iota(I32, logits.shape, 1).astype(F32)
    vals, hots = [], []
    for _ in range(TOP_K):
        m = jnp.max(logits, axis=-1, keepdims=True)
        idx = jnp.min(jnp.where(logits == m, lane, float(LANES)), axis=-1, keepdims=True)
        hot = lane == idx
        vals.append(m)
        hots.append(hot)
        logits = jnp.where(hot, -3e38, logits)
    es = [jnp.exp(v - vals[0]) for v in vals]
    den = es[0] + es[1] + es[2] + es[3]
    comb = jnp.zeros(logits.shape, F32)
    sel = jnp.zeros(logits.shape, F32)
    for e, hot in zip(es, hots):
        comb = comb + jnp.where(hot, e / den, 0.0)
        sel = sel + jnp.where(hot, 1.0, 0.0)
    comb_ref[...] = comb
    sel_ref[...] = sel.astype(BF16)


def _outproj(oa, ob, x2, gt, sh, sc, g_ffn, w_out_b, w_router_p, b_router_p, tm, steps_per_seq):
    n, d = x2.shape
    assert d == TOKEN_ROWS * LANES
    const = lambda shape: pl.BlockSpec(shape, lambda i: (0,) * len(shape))
    row = lambda w: pl.BlockSpec((tm, w), lambda i: (i, 0))
    return pl.pallas_call(
        _outproj_kernel,
        out_shape=[jax.ShapeDtypeStruct((n, d), F32), jax.ShapeDtypeStruct((n * TOKEN_ROWS, LANES), F32),
                   jax.ShapeDtypeStruct((n, LANES), F32), jax.ShapeDtypeStruct((n, LANES), BF16)],
        grid=(n // tm,),
        in_specs=[row(oa.shape[1]), row(ob.shape[1]), row(d),
                  _mod_spec(gt, tm, steps_per_seq), _mod_spec(sh, tm, steps_per_seq),
                  _mod_spec(sc, tm, steps_per_seq),
                  const((1, d)), const(w_out_b.shape), const((d, LANES)), const((1, LANES))],
        out_specs=[row(d), pl.BlockSpec((tm * TOKEN_ROWS, LANES), lambda i: (i, 0)), row(LANES), row(LANES)],
        compiler_params=_cparams("arbitrary"),
        name="outproj",
    )(oa, ob, x2, gt, sh, sc, g_ffn.reshape(1, d), w_out_b, w_router_p, b_router_p)


MOE_TILE = 256
TAB_ROWS = 8


def _rank_kernel(sel_ref, tri_ref, before_ref, base_ref, cnt_ref, total_ref, carry_ref):
    @pl.when(pl.program_id(0) == 0)
    def _():
        carry_ref[...] = jnp.zeros(carry_ref.shape, F32)

    sel = sel_ref[...]
    tm = sel.shape[0]
    before = _dot(tri_ref[...], sel)
    before_ref[...] = before
    carry = carry_ref[...]
    base_ref[...] = carry
    cnt = jnp.broadcast_to(before[tm - 1:tm, :] + sel[tm - 1:tm, :].astype(F32), carry.shape)
    cnt_ref[...] = cnt
    carry = carry + cnt
    carry_ref[...] = carry
    total_ref[...] = carry


def _moe_rank(sel, tm):
    n = sel.shape[0]
    r = lax.broadcasted_iota(I32, (tm, tm), 0)
    c = lax.broadcasted_iota(I32, (tm, tm), 1)
    tri = jnp.where(c < r, 1.0, 0.0).astype(BF16)
    per_tile = pl.BlockSpec((8, LANES), lambda i: (i, 0))
    return pl.pallas_call(
        _rank_kernel,
        out_shape=[jax.ShapeDtypeStruct((n, LANES), F32), jax.ShapeDtypeStruct((n // tm * 8, LANES), F32),
                   jax.ShapeDtypeStruct((n // tm * 8, LANES), F32), jax.ShapeDtypeStruct((8, LANES), F32)],
        grid=(n // tm,),
        in_specs=[pl.BlockSpec((tm, LANES), lambda i: (i, 0)), pl.BlockSpec((tm, tm), lambda i: (0, 0))],
        out_specs=[pl.BlockSpec((tm, LANES), lambda i: (i, 0)), per_tile, per_tile,
                   pl.BlockSpec((8, LANES), lambda i: (0, 0))],
        scratch_shapes=[pltpu.VMEM((8, LANES), F32)],
        compiler_params=_cparams("arbitrary"),
        name="moe_rank",
    )(sel, tri)


def _slots_kernel(before_ref, sel_ref, comb_ref, base_ref, cnt_ref, starts_ref, upper_ref,
                  pos_ref, gate_ref, tab_ref):
    cnt = cnt_ref[...]
    offs = _dot(cnt.astype(BF16), upper_ref[...])
    posv = before_ref[...] + offs[0:1, :]
    comb = comb_ref[...]
    sel = sel_ref[...].astype(F32) > 0.0
    lane = lax.broadcasted_iota(I32, posv.shape, 1)
    lanef = lane.astype(F32)
    pos = jnp.zeros(posv.shape, F32)
    gates = jnp.zeros(posv.shape, F32)
    for k in range(TOP_K):
        idx = jnp.min(jnp.where(sel, lanef, float(LANES)), axis=-1, keepdims=True)
        hot = lanef == idx
        pk = jnp.sum(jnp.where(hot, posv, 0.0), axis=-1, keepdims=True)
        gk = jnp.sum(jnp.where(hot, comb, 0.0), axis=-1, keepdims=True)
        pos = jnp.where(lane == k, pk, pos)
        gates = jnp.where(lane == k, gk, gates)
        sel = jnp.logical_and(sel, jnp.logical_not(hot))
    pos_ref[...] = pos.astype(I32)
    gate_ref[...] = gates
    first = starts_ref[...] + base_ref[0:1, :]
    tab = jnp.concatenate([first, cnt[0:1, :], offs[0:1, :], jnp.zeros((TAB_ROWS - 3, LANES), F32)], axis=0)
    tab_ref[...] = tab.astype(I32)


def _moe_slots(before, sel, comb, base, cnt, starts, tm):
    n = before.shape[0]
    row = pl.BlockSpec((tm, LANES), lambda i: (i, 0))
    per_tile = pl.BlockSpec((8, LANES), lambda i: (i, 0))
    r = lax.broadcasted_iota(I32, (LANES, LANES), 0)
    c = lax.broadcasted_iota(I32, (LANES, LANES), 1)
    upper = jnp.where(r < c, 1.0, 0.0).astype(BF16)
    return pl.pallas_call(
        _slots_kernel,
        out_shape=[jax.ShapeDtypeStruct((n, LANES), I32), jax.ShapeDtypeStruct((n, LANES), F32),
                   jax.ShapeDtypeStruct((n // tm * TAB_ROWS, LANES), I32)],
        grid=(n // tm,),
        in_specs=[row, row, row, per_tile, per_tile, pl.BlockSpec((1, LANES), lambda i: (0, 0)),
                  pl.BlockSpec((LANES, LANES), lambda i: (0, 0))],
        out_specs=[row, row, pl.BlockSpec((TAB_ROWS, LANES), lambda i: (i, 0))],
        compiler_params=_cparams("arbitrary"),
        name="moe_slots",
    )(before, sel, comb, base, cnt, starts, upper)


def _slab_rows(ref, index, count=1):
    return ref.at[pl.ds(pl.multiple_of(index * TOKEN_ROWS, TOKEN_ROWS), count * TOKEN_ROWS), :]


def _run_copies(src_ref, dst_ref, src0, dst0, length, max_len, sem, wait):
    for b in range(max_len.bit_length() - 1, -1, -1):
        size = 1 << b

        @pl.when((length & size) != 0)
        def _():
            done = (length >> (b + 1)) << (b + 1)
            cp = pltpu.make_async_copy(_slab_rows(src_ref, src0 + done, size),
                                       _slab_rows(dst_ref, dst0 + done, size), sem)
            if wait:
                cp.wait()
            else:
                cp.start()


def _for_each_run(tab_ref, fn):
    def body(e, carry):
        fn(tab_ref[e], tab_ref[LANES + e], tab_ref[2 * LANES + e])
        return carry
    lax.fori_loop(0, N_EXPERTS, body, 0)


def _dispatch_kernel(pos_ref, tab_ref, x_ref, xs_ref, stage_ref, sem):
    tm = x_ref.shape[0] // TOKEN_ROWS

    def place(t, carry):
        v = _slab_rows(x_ref, t)[...]
        for k in range(TOP_K):
            _slab_rows(stage_ref, pos_ref[t * TOP_K + k])[...] = v
        return carry

    lax.fori_loop(0, tm, place, 0)
    for wait in (False, True):
        _for_each_run(tab_ref, lambda first, length, off, wait=wait: _run_copies(
            stage_ref, xs_ref, off, first, length, tm, sem, wait))


def _smem_words(words):
    return pl.BlockSpec((words,), lambda i: (i,), memory_space=pltpu.SMEM)


def _moe_dispatch(pos_flat, tab_flat, xm_slabs, tm):
    n = xm_slabs.shape[0] // TOKEN_ROWS
    return pl.pallas_call(
        _dispatch_kernel,
        out_shape=jax.ShapeDtypeStruct((n * TOP_K * TOKEN_ROWS, LANES), F32),
        grid=(n // tm,),
        in_specs=[_smem_words(tm * TOP_K), _smem_words(TAB_ROWS * LANES),
                  pl.BlockSpec((tm * TOKEN_ROWS, LANES), lambda i: (i, 0))],
        out_specs=pl.BlockSpec(memory_space=pl.ANY),
        scratch_shapes=[pltpu.VMEM((tm * TOP_K * TOKEN_ROWS, LANES), F32), pltpu.SemaphoreType.DMA],
        compiler_params=_cparams("arbitrary"),
        name="moe_dispatch",
    )(pos_flat, tab_flat, xm_slabs)


def _ffn_kernel(tile_ref, exp_ref, lo_ref, hi_ref, xs_ref, wgu_ref, bgu_ref, wd_ref, bd_ref, ys_ref,
                wgu_b_ref, wd_b_ref):
    g = pl.program_id(0)
    lo, hi = lo_ref[g], hi_ref[g]
    tme = xs_ref.shape[0] // TOKEN_ROWS
    dff = wd_ref.shape[0]

    @pl.when(jnp.logical_or(g == 0, exp_ref[g] != exp_ref[jnp.maximum(g - 1, 0)]))
    def _():
        wgu_b_ref[...] = wgu_ref[...].astype(BF16)
        wd_b_ref[...] = wd_ref[...].astype(BF16)

    @pl.when(hi > lo)
    def _():
        x = jnp.concatenate([xs_ref[pl.ds(c, tme, stride=TOKEN_ROWS), :] for c in range(TOKEN_ROWS)], axis=1)
        hgu = _dot(x.astype(BF16), wgu_b_ref[...]) + bgu_ref[...]
        gate = jnp.minimum(hgu[:, :dff], SWIGLU_LIMIT)
        up = jnp.clip(hgu[:, dff:], -SWIGLU_LIMIT, SWIGLU_LIMIT)
        a = (up + 1.0) * (gate * jax.nn.sigmoid(SWIGLU_ALPHA * gate))
        yo = _dot(a.astype(BF16), wd_b_ref[...]) + bd_ref[...]
        row = lax.broadcasted_iota(I32, (tme, LANES), 0)
        mine = jnp.logical_and(row >= lo, row < hi)

        @pl.when(lo == 0)
        def _():
            for c in range(TOKEN_ROWS):
                ys_ref[pl.ds(c, tme, stride=TOKEN_ROWS), :] = yo[:, c * LANES:(c + 1) * LANES]

        @pl.when(lo != 0)
        def _():
            for c in range(TOKEN_ROWS):
                dst = ys_ref.at[pl.ds(c, tme, stride=TOKEN_ROWS), :]
                dst[...] = jnp.where(mine, yo[:, c * LANES:(c + 1) * LANES], dst[...])


def _moe_plan(counts, n_slots, tme):
    cnt = counts[:N_EXPERTS].astype(I32)
    ends = jnp.cumsum(cnt)
    starts = ends - cnt
    first_tile = starts // tme
    n_items = jnp.where(cnt > 0, (ends - 1) // tme - first_tile + 1, 0)
    item_end = jnp.cumsum(n_items)
    item_start = item_end - n_items
    n_work = n_slots // tme + N_EXPERTS - 1
    g = jnp.arange(n_work, dtype=I32)
    gi = jnp.minimum(g, item_end[-1] - 1)
    e = jnp.sum((item_end[None, :] <= gi[:, None]).astype(I32), axis=1)
    tile = first_tile[e] + (gi - item_start[e])
    lo = jnp.maximum(starts[e], tile * tme) - tile * tme
    hi = jnp.minimum(ends[e], (tile + 1) * tme) - tile * tme
    real = g < item_end[-1]
    return tile, e, jnp.where(real, lo, 0), jnp.where(real, hi, 0), starts


def _moe_ffn(plan, xs, w_gu, b_gu, w_down, b_down, tme):
    tile, e, lo, hi = plan
    ne, d, dff2 = w_gu.shape
    dff = dff2 // 2
    slab_spec = pl.BlockSpec((tme * TOKEN_ROWS, LANES), lambda g, tile, e, lo, hi: (tile[g], 0))
    per_e = lambda *shape: pl.BlockSpec((None,) + shape, lambda g, tile, e, lo, hi: (e[g],) + (0,) * len(shape))
    grid_spec = pltpu.PrefetchScalarGridSpec(
        num_scalar_prefetch=4,
        grid=(tile.shape[0],),
        in_specs=[slab_spec, per_e(d, dff2), per_e(1, dff2), per_e(dff, d), per_e(1, d)],
        out_specs=slab_spec,
        scratch_shapes=[pltpu.VMEM((d, dff2), BF16), pltpu.VMEM((dff, d), BF16)],
    )
    return pl.pallas_call(
        _ffn_kernel,
        out_shape=jax.ShapeDtypeStruct(xs.shape, F32),
        grid_spec=grid_spec,
        compiler_params=_cparams("arbitrary"),
        name="moe_ffn",
    )(tile, e, lo, hi, xs, w_gu, b_gu.reshape(ne, 1, dff2), w_down, b_down.reshape(ne, 1, d))


def _combine_kernel(pos_ref, gate_ref, tab_ref, h_ref, gt_ref, ys_ref, y_ref, stage_ref, acc_ref, sem):
    tm = h_ref.shape[0]
    for wait in (False, True):
        _for_each_run(tab_ref, lambda first, length, off, wait=wait: _run_copies(
            ys_ref, stage_ref, first, off, length, tm, sem, wait))

    def gather(t, carry):
        acc = None
        for k in range(TOP_K):
            part = gate_ref[t * TOP_K + k] * _slab_rows(stage_ref, pos_ref[t * TOP_K + k])[...]
            acc = part if acc is None else acc + part
        _slab_rows(acc_ref, t)[...] = acc
        return carry

    lax.fori_loop(0, tm, gather, 0)
    gt = gt_ref[...]
    for c in range(TOKEN_ROWS):
        sl = slice(c * LANES, (c + 1) * LANES)
        y_ref[:, sl] = h_ref[:, sl] + gt[:, sl] * acc_ref[pl.ds(c, tm, stride=TOKEN_ROWS), :]


def _moe_combine(pos_flat, gate_flat, tab_flat, h, gt, ys, tm, steps_per_seq):
    n, d = h.shape
    return pl.pallas_call(
        _combine_kernel,
        out_shape=jax.ShapeDtypeStruct((n, d), F32),
        grid=(n // tm,),
        in_specs=[_smem_words(tm * TOP_K), _smem_words(tm * TOP_K), _smem_words(TAB_ROWS * LANES),
                  pl.BlockSpec((tm, d), lambda i: (i, 0)),
                  _mod_spec(gt, tm, steps_per_seq), pl.BlockSpec(memory_space=pl.ANY)],
        out_specs=pl.BlockSpec((tm, d), lambda i: (i, 0)),
        scratch_shapes=[pltpu.VMEM((tm * TOP_K * TOKEN_ROWS, LANES), F32),
                        pltpu.VMEM((tm * TOKEN_ROWS, LANES), F32), pltpu.SemaphoreType.DMA],
        compiler_params=_cparams("arbitrary"),
        name="moe_combine",
    )(pos_flat, gate_flat, tab_flat, h, gt, ys)


def _moe(xm_slabs, comb, sel, h, gt, wgu_b, b_gu, wd_b, b_down, tokens_per_seq):
    n, d = h.shape
    tm = min(MOE_TILE, _token_tile(n))
    tme = min(512, _token_tile(n * TOP_K))
    before, base, cnt, total = _moe_rank(sel, tm)
    tile, e, lo, hi, starts = _moe_plan(total[0], n * TOP_K, tme)
    starts_row = jnp.zeros((1, LANES), F32).at[0, :N_EXPERTS].set(starts.astype(F32))
    pos4, gate4, tab = _moe_slots(before, sel, comb, base, cnt, starts_row, tm)
    pos_flat = pos4[:, :TOP_K].reshape(n * TOP_K)
    gate_flat = gate4[:, :TOP_K].reshape(n * TOP_K)
    tab_flat = tab.reshape(-1)
    xs = _moe_dispatch(pos_flat, tab_flat, xm_slabs, tm)
    ys = _moe_ffn((tile, e, lo, hi), xs, wgu_b, b_gu, wd_b, b_down, tme)
    return _moe_combine(pos_flat, gate_flat, tab_flat, h, gt, ys, tm, max(1, tokens_per_seq // tm))


def _token_tile(n):
    for tm in (512, 256, 128, 64, 32, 16, 8):
        if n % tm == 0:
            return tm
    raise ValueError(f"token count {n} must be a multiple of 8")


def kernel(x_prompt, x_sample, cache_a_k, cache_a_v, cache_idx_k, cache_b_k, cache_b_v, page_table,
           c_prompt, c_sample, w_ada, b_ada, g_attn, w_in, g_qa, g_ka, g_ki, g_qb, g_kb,
           lam_q1, lam_k1, lam_q2, lam_k2, g_sub, w_out, g_ffn, w_router, b_router,
           w_gu, b_gu, w_down, b_down):
    bsz, t_p, d = x_prompt.shape
    db, t_s, _ = x_sample.shape
    depth = w_in.shape[0]
    n_pool, page = cache_a_k.shape[1], cache_a_k.shape[2]
    n_pages = page_table.shape[1]
    past = n_pages * page
    assert page == LANES and t_s == 8 and t_p % Q_BLOCK == 0 and d == 1024
    n_p, n_s = bsz * t_p, db * t_s
    pg_scores, pg_attend = math.gcd(n_pages, 32), math.gcd(n_pages, 16)

    cs_p, sn_p = _rope_tables(jnp.arange(t_p, dtype=I32))
    cs_s, sn_s = _rope_tables(past + jnp.arange(t_s, dtype=I32))
    cs_s, sn_s = jnp.tile(cs_s, (db, 1)), jnp.tile(sn_s, (db, 1))
    tile2 = lambda g: jnp.concatenate([g, g]).astype(F32)
    ksel_p = min(TOPK_MAX, t_p // 4)
    ksel_s = min(TOPK_MAX, (past + t_s) // 4)
    tm_p, tm_s = _token_tile(t_p), _token_tile(n_s)
    sps_p = t_p // tm_p
    per_tok = lambda m: jnp.repeat(m, t_s, axis=0)

    hp = x_prompt.reshape(n_p, d)
    hs = x_sample.reshape(n_s, d)
    rows_p = [[] for _ in range(5)]
    rows_s = [[] for _ in range(5)]
    for l in range(depth):
        lam_init = 0.8 - 0.6 * math.exp(-0.3 * l)
        mods = _adaln(jnp.concatenate([c_prompt, c_sample], axis=0), w_ada[l], b_ada[l])
        mp = [m.reshape(bsz, 1, d) for m in jnp.split(mods[:bsz], 6, axis=-1)]
        ms_ = [per_tok(m) for m in jnp.split(mods[bsz:], 6, axis=-1)]
        w_perm = _permute_w_in(w_in[l])
        gains = jnp.zeros((8, LANES), F32).at[:5].set(
            jnp.stack([tile2(g_qa[l]), tile2(g_ka[l]), tile2(g_ki[l]), tile2(g_qb[l]), tile2(g_kb[l])]))
        lamv = jnp.zeros((8, LANES), F32).at[:4, :HD].set(
            jnp.stack([lam_q1[l], lam_k1[l], lam_q2[l], lam_k2[l]]).astype(F32))
        gsub = g_sub[l].reshape(1, VB).astype(F32)
        w_out_b = w_out[l].astype(BF16)
        w_router_p = jnp.zeros((d, LANES), F32).at[:, :N_EXPERTS].set(w_router[l])
        b_router_p = jnp.full((1, LANES), NEG_BIAS, F32).at[0, :N_EXPERTS].set(b_router[l])
        wgu_b = w_gu[l]
        wd_b = w_down[l]

        (qa, qi, qb, ka, kb, ki, wi, va, vb, kab, kbb, kib, vab, vbb) = _proj(
            hp, mp[0], mp[1], g_attn[l], w_perm, cs_p, sn_p, gains, tm_p, sps_p, True)
        r3 = lambda a: a.reshape(bsz, t_p, a.shape[-1])
        oa = _dsa_prompt(r3(qi), r3(wi), r3(kib), r3(qa), r3(kab), r3(vab), ksel_p)
        ob = _diff_prompt(r3(qb), r3(kbb), r3(vbb), lamv, gsub, lam_init)
        h1, xm, comb, sel = _outproj(oa.reshape(n_p, 512), ob.reshape(n_p, 512), hp, mp[2], mp[3], mp[4],
                                     g_ffn[l], w_out_b, w_router_p, b_router_p, tm_p, sps_p)
        hp = _moe(xm, comb, sel, h1, mp[5], wgu_b, b_gu[l], wd_b, b_down[l], t_p)
        for lst, r in zip(rows_p, (ka.reshape(bsz, KVA, HD, t_p).transpose(0, 3, 1, 2),
                                   va.reshape(bsz, KVA, HD, t_p).transpose(0, 3, 1, 2),
                                   ki.transpose(0, 2, 1),
                                   kb.reshape(bsz, KVB, 2, HD, t_p).transpose(0, 4, 1, 2, 3),
                                   vb.reshape(bsz, t_p, KVB, VB))):
            lst.append(r)

        (qa, qi, qb, ka, kb, ki, wi, va, vb, kab, kbb, kib, vab, vbb) = _proj(
            hs, ms_[0], ms_[1], g_attn[l], w_perm, cs_s, sn_s, gains, tm_s, 1, False)
        qm = qi.reshape(db, t_s, H_IDX, D_IDX).transpose(0, 2, 1, 3).reshape(db, 64, D_IDX)
        wcol = jnp.broadcast_to(
            wi[:, :H_IDX].reshape(db, t_s, H_IDX).transpose(0, 2, 1).reshape(db, 64, 1), (db, 64, LANES))
        pad_new = lambda a: jnp.pad(a.reshape(db, t_s, a.shape[-1]), ((0, 0), (0, LANES - t_s), (0, 0)))
        idx_t = jnp.transpose(cache_idx_k[l], (0, 2, 1))
        ak_t = jnp.transpose(cache_a_k[l], (0, 2, 3, 1)).reshape(n_pool, KVA * HD, page)
        av_t = jnp.transpose(cache_a_v[l], (0, 2, 3, 1)).reshape(n_pool, KVA * HD, page)
        bk_t = jnp.transpose(cache_b_k[l], (0, 2, 3, 4, 1)).reshape(n_pool, KVB * 2 * HD, page)
        bv_r = cache_b_v[l].reshape(n_pool, page * KVB, VB)
        sc_past, sc_new = _sample_scores(page_table, qm, wcol, pad_new(kib[:, :D_IDX]), idx_t, pg_scores, t_s)
        sct = jnp.concatenate([sc_past, sc_new], axis=-1).reshape(n_s, n_pages + 1, LANES).transpose(1, 0, 2)
        sel = _sample_select(sct, ksel_s).transpose(1, 0, 2).reshape(db, t_s, (n_pages + 1) * LANES)
        qa4 = qa.reshape(db, t_s, HA, HD).transpose(0, 2, 1, 3)
        zq = jnp.zeros_like(qa4[:, :HA // 2])
        qa_rows = jnp.concatenate([jnp.concatenate([qa4[:, :HA // 2], zq], axis=-1),
                                   jnp.concatenate([zq, qa4[:, HA // 2:]], axis=-1)], axis=1).reshape(db, 64, 128)
        qb5 = qb.reshape(db, t_s, HB, 2, HD).transpose(0, 2, 3, 1, 4)
        zb = jnp.zeros_like(qb5[:, 0, 0])
        blocks = []
        for hh in range(HB):
            for m in range(2):
                pos = (hh // (HB // KVB)) * 2 + m
                blocks.append(jnp.concatenate([qb5[:, hh, m] if p == pos else zb for p in range(4)], axis=-1))
        qb_rows = jnp.stack(blocks, axis=1).reshape(db, 64, 256)
        oa_rows, ob = _sample_attend(
            page_table, qa_rows, qb_rows, sel[:, :, :past], sel[:, :, past:],
            pad_new(kab), pad_new(vab), pad_new(kbb), pad_new(vbb), lamv, gsub,
            ak_t, av_t, bk_t, bv_r, pg_attend, t_s, lam_init)
        oa5 = oa_rows.reshape(db, HA, t_s, KVA, HD)
        oa = jnp.concatenate([oa5[:, :HA // 2, :, 0], oa5[:, HA // 2:, :, 1]], axis=1)
        oa = oa.transpose(0, 2, 1, 3).reshape(n_s, HA * HD)
        h1, xm, comb, sel = _outproj(oa, ob.reshape(n_s, 512), hs, ms_[2], ms_[3], ms_[4],
                                     g_ffn[l], w_out_b, w_router_p, b_router_p, tm_s, 1)
        hs = _moe(xm, comb, sel, h1, ms_[5], wgu_b, b_gu[l], wd_b, b_down[l], t_s)
        for lst, r in zip(rows_s, (ka.reshape(db, t_s, KVA, HD), va.reshape(db, t_s, KVA, HD),
                                   ki.reshape(db, t_s, D_IDX), kb.reshape(db, t_s, KVB, 2, HD),
                                   vb.reshape(db, t_s, KVB, VB))):
            lst.append(r)

    outs_p = [jnp.stack(r) for r in rows_p]
    outs_s = [jnp.stack(r) for r in rows_s]
    return (hp.reshape(bsz, t_p, d), hs.reshape(db, t_s, d), *outs_p, *outs_s)
```

```python
import functools
import math

import jax
import jax.numpy as jnp
from jax import lax
from jax.experimental import pallas as pl
from jax.experimental.pallas import tpu as pltpu

F32 = jnp.float32
BF16 = jnp.bfloat16
I32 = jnp.int32

HD = 64
HA = 8
KVA = 2
H_IDX = 8
D_IDX = 64
HB = 4
KVB = 2
VB = 128
N_EXPERTS = 32
TOP_K = 4
TOPK_MAX = 256
SWIGLU_LIMIT = 7.0
SWIGLU_ALPHA = 1.702
ROPE_THETA = 10000.0
EPS = 1e-6
Q_BLOCK = 256
DSA_Q_BLOCK = 512
DIFF_Q_BLOCK = 512
K_CHUNK = 256
LANES = 128
INT_MIN = -2 ** 31
NEG_BIAS = -1e30
M_INIT = -1e20
VMEM_LIMIT = 56 * 1024 * 1024
Q_SCALE = HD ** -0.5 * math.log2(math.e)

C_QA, C_QI, C_QB, C_KA, C_KB, C_KI, C_WI, C_VA, C_VB, C_END = (
    0, 512, 1024, 1536, 1664, 1920, 2048, 2176, 2304, 2560)


def _dot(a, b):
    return jnp.dot(a, b, preferred_element_type=F32)


def _dot_nt(a, b):
    return lax.dot_general(a, b, (((1,), (1,)), ((), ())), preferred_element_type=F32)


def _split_bf16(a):
    hi = a.astype(BF16)
    lo = (a - hi.astype(F32)).astype(BF16)
    return hi, lo


def _dot3(a, b):
    ah, al = _split_bf16(a)
    bh, bl = _split_bf16(b)
    return _dot(ah, bh) + (_dot(ah, bl) + _dot(al, bh))


def _cparams(*sem):
    return pltpu.CompilerParams(dimension_semantics=sem, vmem_limit_bytes=VMEM_LIMIT)


def _adaln_kernel(c_ref, w_ref, b_ref, o_ref):
    c = c_ref[...]
    s = c * jax.nn.sigmoid(c)
    o_ref[...] = _dot3(s, w_ref[...]) + b_ref[...]


def _adaln(c, w_ada, b_ada):
    n, d = c.shape
    e = w_ada.shape[1]
    tn = 1024
    return pl.pallas_call(
        _adaln_kernel,
        out_shape=jax.ShapeDtypeStruct((n, e), F32),
        grid=(e // tn,),
        in_specs=[pl.BlockSpec((n, d), lambda j: (0, 0)),
                  pl.BlockSpec((d, tn), lambda j: (0, j)),
                  pl.BlockSpec((1, tn), lambda j: (0, j))],
        out_specs=pl.BlockSpec((n, tn), lambda j: (0, j)),
        compiler_params=_cparams("arbitrary"),
        name="adaln",
    )(c, w_ada, b_ada.reshape(1, e))


def _rot_half(v):
    lane = lax.broadcasted_iota(I32, v.shape, 1)
    return jnp.where((lane % HD) < HD // 2, pltpu.roll(v, LANES - HD // 2, 1), pltpu.roll(v, HD // 2, 1))


def _norm_rope(z, c0, width, gain, pn, cs, sn, out_scale):
    outs = []
    for g in range(width // LANES):
        v = z[:, c0 + g * LANES:c0 + (g + 1) * LANES]
        if gain is not None:
            ms = _dot((v * v).astype(BF16), pn)
            v = v * lax.rsqrt(ms + EPS) * gain
        v = v * cs + _rot_half(v) * sn
        if out_scale != 1.0:
            v = v * out_scale
        outs.append(v)
    return outs


def _proj_kernel(x_ref, sh_ref, sc_ref, g_ref, w_ref, cs_ref, sn_ref, gains_ref, pn_ref,
                 qa_ref, qi_ref, qb_ref, ka_ref, kb_ref, ki_ref, wi_ref, va_ref, vb_ref,
                 kab_ref, kbb_ref, kib_ref, vab_ref, vbb_ref, *, token_minor):
    tm = x_ref.shape[0]

    def put(ref, r0, v):
        if token_minor:
            ref[r0:r0 + v.shape[1], :] = v.T
        else:
            ref[:, r0:r0 + v.shape[1]] = v

    x = x_ref[...]
    ms = jnp.mean(x * x, axis=-1, keepdims=True)
    xn = x * lax.rsqrt(ms + EPS) * g_ref[...]
    xn = xn * (1.0 + sc_ref[...]) + sh_ref[...]
    z = _dot(xn.astype(BF16), w_ref[...])
    cs = cs_ref[...]
    sn = sn_ref[...]
    pn = pn_ref[...]
    gains = gains_ref[...]
    g_qa, g_ka, g_ki, g_qb, g_kb = (gains[r:r + 1, :] for r in range(5))

    qa = _norm_rope(z, C_QA, 512, g_qa, pn, cs, sn, Q_SCALE)
    qi = _norm_rope(z, C_QI, 512, None, pn, cs, sn, 1.0)
    qb = _norm_rope(z, C_QB, 512, g_qb, pn, cs, sn, Q_SCALE)
    for g in range(4):
        sl = slice(g * LANES, (g + 1) * LANES)
        qa_ref[:, sl] = qa[g].astype(BF16)
        qi_ref[:, sl] = qi[g].astype(BF16)
        qb_ref[:, sl] = qb[g].astype(BF16)
    ka = _norm_rope(z, C_KA, 128, g_ka, pn, cs, sn, 1.0)[0]
    put(ka_ref, 0, ka)
    kab_ref[...] = ka.astype(BF16)
    kb = _norm_rope(z, C_KB, 256, g_kb, pn, cs, sn, 1.0)
    for g in range(2):
        put(kb_ref, g * LANES, kb[g])
        kbb_ref[:, g * LANES:(g + 1) * LANES] = kb[g].astype(BF16)
    ki = _norm_rope(z, C_KI, 128, g_ki, pn, cs, sn, 1.0)[0]
    if token_minor:
        ki_ref[...] = ki.T[:D_IDX, :]
    else:
        ki_ref[...] = ki[:, :D_IDX]
    kib_ref[...] = ki.astype(BF16)
    wi_ref[...] = z[:, C_WI:C_WI + LANES] * (H_IDX ** -0.5 * D_IDX ** -0.5)
    va = z[:, C_VA:C_VA + 128]
    put(va_ref, 0, va)
    vab_ref[...] = va.astype(BF16)
    vb = z[:, C_VB:C_VB + 256]
    if token_minor:
        for j in range(KVB):
            vb_ref[pl.ds(j, tm, stride=KVB), :] = vb[:, j * VB:(j + 1) * VB]
    else:
        vb_ref[...] = vb
    vbb_ref[...] = vb.astype(BF16)


def _permute_w_in(w_in):
    d = w_in.shape[0]
    seg = lambda a, b: w_in[:, a:b]
    cols = [seg(0, 512), seg(768, 1280), seg(1352, 1864), seg(512, 640), seg(1864, 2120),
            seg(1280, 1344), seg(1280, 1344), seg(1344, 1352), jnp.zeros((d, LANES - H_IDX), w_in.dtype),
            seg(640, 768), seg(2120, 2376)]
    return jnp.concatenate(cols, axis=1).astype(BF16)


def _rope_tables(pos):
    half = HD // 2
    inv = ROPE_THETA ** (-jnp.arange(half, dtype=F32) / half)
    ang = pos.astype(F32)[:, None] * inv[None, :]
    cos, sin = jnp.cos(ang), jnp.sin(ang)
    cs = jnp.concatenate([cos, cos, cos, cos], axis=1)
    sn = jnp.concatenate([-sin, sin, -sin, sin], axis=1)
    return cs, sn


def _head_mean_matrix():
    r = lax.broadcasted_iota(I32, (LANES, LANES), 0) // HD
    c = lax.broadcasted_iota(I32, (LANES, LANES), 1) // HD
    return jnp.where(r == c, 1.0 / HD, 0.0).astype(BF16)


def _mod_spec(arr, tm, steps_per_seq):
    if arr.ndim == 3:
        return pl.BlockSpec((None, 1, arr.shape[-1]), lambda i: (i // steps_per_seq, 0, 0))
    return pl.BlockSpec((tm, arr.shape[-1]), lambda i: (i, 0))


def _proj(x2, sh, sc, g_attn, w_perm, cs, sn, gains, tm, steps_per_seq, token_minor):
    n, d = x2.shape
    if cs.shape[0] == n:
        tab_spec = pl.BlockSpec((tm, LANES), lambda i: (i, 0))
    else:
        tab_spec = pl.BlockSpec((tm, LANES), lambda i: (i % steps_per_seq, 0))
    const = lambda shape: pl.BlockSpec(shape, lambda i: (0,) * len(shape))
    widths = [(512, BF16), (512, BF16), (512, BF16), (128, F32), (256, F32), (D_IDX, F32), (128, F32),
              (128, F32), (256, F32), (128, BF16), (256, BF16), (128, BF16), (128, BF16), (256, BF16)]
    shapes = [jax.ShapeDtypeStruct((n, w), dt) for w, dt in widths]
    specs = [pl.BlockSpec((tm, w), lambda i: (i, 0)) for w, _ in widths]
    if token_minor:
        nseq, t = n // (tm * steps_per_seq), tm * steps_per_seq
        for o in (3, 4, 5, 7):
            w = widths[o][0]
            shapes[o] = jax.ShapeDtypeStruct((nseq, w, t), F32)
            specs[o] = pl.BlockSpec((None, w, tm), lambda i: (i // steps_per_seq, 0, i % steps_per_seq))
        shapes[8] = jax.ShapeDtypeStruct((n * KVB, VB), F32)
        specs[8] = pl.BlockSpec((tm * KVB, VB), lambda i: (i, 0))
    return pl.pallas_call(
        functools.partial(_proj_kernel, token_minor=token_minor),
        out_shape=shapes,
        grid=(n // tm,),
        in_specs=[pl.BlockSpec((tm, d), lambda i: (i, 0)),
                  _mod_spec(sh, tm, steps_per_seq), _mod_spec(sc, tm, steps_per_seq),
                  const((1, d)), const((d, C_END)), tab_spec, tab_spec,
                  const((8, LANES)), const((LANES, LANES))],
        out_specs=specs,
        compiler_params=_cparams("arbitrary"),
        name="proj",
    )(x2, sh, sc, g_attn.reshape(1, d), w_perm, cs, sn, gains, _head_mean_matrix())


def _sortable_key(x):
    bits = lax.bitcast_convert_type(x, I32)
    return bits ^ ((bits >> 31) & 0x7FFFFFFF)


COUNT_ROWS = 128


def _count_keys(pred, skey_ref, nkc, rows, ck, group):
    rb = min(rows, COUNT_ROWS)
    ngroups = (nkc + group - 1) // group
    parts = []
    for r0 in range(0, rows, rb):
        def body(cg, acc, r0=r0):
            for u in range(group):
                c = cg * group + u
                for g in range(ck // LANES):
                    k = skey_ref[c, r0:r0 + rb, g * LANES:(g + 1) * LANES]
                    acc = acc + jnp.where(pred(k, c, g * LANES, r0), 1.0, 0.0)
            return acc
        parts.append(lax.fori_loop(0, ngroups, body, jnp.zeros((rb, LANES), F32)))
    acc = parts[0] if len(parts) == 1 else jnp.concatenate(parts, axis=0)
    return jnp.broadcast_to(jnp.sum(acc, axis=-1, keepdims=True), (rows, LANES))


def _select_topk(skey_ref, nkc, ksel, rows, ck, group=1):
    kf = float(ksel)
    rb = min(rows, COUNT_ROWS)
    strip = lambda a, r0: a[r0:r0 + rb]
    count = functools.partial(_count_keys, skey_ref=skey_ref, nkc=nkc, rows=rows, ck=ck, group=group)

    def bit_body(i, t):
        cand = t + lax.shift_left(jnp.int32(1), 31 - i)
        cnt = count(lambda k, c, c0, r0: k >= strip(cand, r0))
        return jnp.where(cnt >= kf, cand, t)

    thr = lax.fori_loop(0, 32, bit_body, jnp.full((rows, LANES), INT_MIN, I32))
    n_ge = count(lambda k, c, c0, r0: k >= strip(thr, r0))
    n_gt = count(lambda k, c, c0, r0: k > strip(thr, r0))
    excess = jnp.logical_and(n_ge > kf, thr != INT_MIN)
    need = kf - n_gt

    @pl.when(jnp.max(jnp.where(excess, 1.0, 0.0)) > 0.0)
    def _():
        lane = lax.broadcasted_iota(I32, (rb, LANES), 1)
        nbits = max(1, (skey_ref.shape[0] * ck).bit_length())

        def idx_body(i, p):
            cand = p | lax.shift_left(jnp.int32(1), nbits - 1 - i)
            tied_before = lambda k, c, c0, r0: jnp.logical_and(
                k == strip(thr, r0), c * ck + c0 + lane < strip(cand, r0))
            cnt = count(tied_before)
            return jnp.where(cnt <= need, cand, p)

        p_keep = lax.fori_loop(0, nbits, idx_body, jnp.zeros((rows, LANES), I32))

        def drop_body(c, _):
            for r0 in range(0, rows, rb):
                for g in range(ck // LANES):
                    k = skey_ref[c, r0:r0 + rb, g * LANES:(g + 1) * LANES]
                    drop = jnp.logical_and(k == strip(thr, r0), c * ck + g * LANES + lane >= strip(p_keep, r0))
                    drop = jnp.logical_and(drop, strip(excess, r0))
                    skey_ref[c, r0:r0 + rb, g * LANES:(g + 1) * LANES] = jnp.where(drop, INT_MIN, k)
            return 0

        lax.fori_loop(0, nkc, drop_body, 0)

    return jnp.maximum(thr, INT_MIN + 1)


def _flash_init(m_ref, acc_ref):
    m_ref[...] = jnp.full(m_ref.shape, M_INIT, F32)
    acc_ref[...] = jnp.zeros(acc_ref.shape, F32)


def _flash_update(s, pv_fn, m_ref, acc_ref):
    m_prev = m_ref[...]
    m_new = jnp.maximum(m_prev, jnp.max(s, axis=-1, keepdims=True))
    alpha = jnp.exp2(m_prev - m_new)
    rep = lambda a, n: a if n == 1 else jnp.concatenate([a] * n, axis=1)
    p = jnp.exp2(s - rep(m_new, s.shape[1] // LANES)).astype(BF16)
    acc_ref[...] = rep(alpha, acc_ref.shape[1] // LANES) * acc_ref[...] + pv_fn(p)
    m_ref[...] = m_new


def _pv_pages(vaug_list):
    def fn(p):
        upd = None
        for i, va in enumerate(vaug_list):
            nk = va.shape[0]
            d = _dot(p[:, i * nk:(i + 1) * nk], va)
            upd = d if upd is None else upd + d
        return upd
    return fn


def _with_ones(v):
    return jnp.concatenate([v, jnp.ones((v.shape[0], LANES), v.dtype)], axis=1)


def _dsa_kernel(qi_ref, wi_ref, ki_ref, qa_ref, ka_ref, va_ref, o_ref, skey_ref, m_ref, acc_ref,
                *, tq, ck, ksel):
    i = pl.program_id(1)
    nkc = (i + 1) * (tq // ck)
    lane = lax.broadcasted_iota(I32, (tq, LANES), 1)
    hi_half = lane >= HD
    qi = qi_ref[...]
    w = wi_ref[...]
    zero_b = jnp.zeros((tq, LANES), BF16)
    q_heads = []
    for h in range(H_IDX):
        grp = qi[:, (h // 2) * LANES:(h // 2 + 1) * LANES]
        q_heads.append(jnp.where(hi_half if h % 2 else jnp.logical_not(hi_half), grp, zero_b))
    w_heads = [w[:, h:h + 1] for h in range(H_IDX)]
    rows = i * tq + lax.broadcasted_iota(I32, (tq, ck), 0)
    col_in = lax.broadcasted_iota(I32, (tq, ck), 1)

    def score_body(c, _):
        kc = ki_ref[pl.ds(pl.multiple_of(c * ck, ck), ck), :]
        acc = jnp.zeros((tq, ck), F32)
        for h in range(H_IDX):
            acc = acc + jnp.maximum(_dot_nt(q_heads[h], kc), 0.0) * w_heads[h]
        key = _sortable_key(acc)
        skey_ref[c] = jnp.where(c * ck + col_in <= rows, key, INT_MIN)
        return 0

    lax.fori_loop(0, nkc, score_body, 0)
    group = 2 if skey_ref.shape[0] % 2 == 0 else 1

    @pl.when(nkc % group != 0)
    def _():
        skey_ref[nkc] = jnp.full((tq, ck), INT_MIN, I32)

    thr = _select_topk(skey_ref, nkc, ksel, tq, ck, group)
    thr = jnp.concatenate([thr] * (ck // LANES), axis=1)

    qa = qa_ref[...].astype(F32)

    def head_at_half(h, half):
        grp = qa[:, (h // 2) * LANES:(h // 2 + 1) * LANES]
        if h % 2 != half:
            grp = pltpu.roll(grp, HD, 1)
        return jnp.where(hi_half if half else jnp.logical_not(hi_half), grp, 0.0)

    g_per = HA // KVA
    outs = []
    for j in range(KVA):
        q_rows = jnp.concatenate([head_at_half(g_per * j + g, j) for g in range(g_per)], axis=0).astype(BF16)
        _flash_init(m_ref, acc_ref)
        own_half = (lax.broadcasted_iota(I32, (ck, LANES), 1) >= HD) == bool(j)
        one_b = jnp.ones((ck, LANES), BF16)

        def att_body(c, _):
            off = pl.multiple_of(c * ck, ck)
            s = _dot_nt(q_rows, ka_ref[pl.ds(off, ck), :])
            bias = jnp.where(skey_ref[c] >= thr, 0.0, NEG_BIAS)
            s = (s.reshape(g_per, tq, ck) + bias[None]).reshape(g_per * tq, ck)
            va = jnp.where(own_half, va_ref[pl.ds(off, ck), :], one_b)
            _flash_update(s, _pv_pages([va]), m_ref, acc_ref)
            return 0

        lax.fori_loop(0, nkc, att_body, 0)
        acc = acc_ref[...]
        outs.append(acc / pltpu.roll(acc, HD, 1))

    for gg in range(HA // 2):
        j = (2 * gg) // g_per
        a = outs[j][((2 * gg) % g_per) * tq:((2 * gg) % g_per + 1) * tq]
        b = outs[j][((2 * gg + 1) % g_per) * tq:((2 * gg + 1) % g_per + 1) * tq]
        if j == 1:
            a = pltpu.roll(a, HD, 1)
        else:
            b = pltpu.roll(b, HD, 1)
        o_ref[:, gg * LANES:(gg + 1) * LANES] = jnp.where(hi_half, b, a).astype(o_ref.dtype)


def _dsa_prompt(qi, wi, kib, qa, kab, vab, ksel):
    b, t, _ = qi.shape
    tq, ck = (DSA_Q_BLOCK if t % DSA_Q_BLOCK == 0 else Q_BLOCK), K_CHUNK
    qspec = lambda w: pl.BlockSpec((None, tq, w), lambda bb, i: (bb, i, 0))
    kspec = lambda w: pl.BlockSpec((None, t, w), lambda bb, i: (bb, 0, 0))
    g_per = HA // KVA
    return pl.pallas_call(
        functools.partial(_dsa_kernel, tq=tq, ck=ck, ksel=ksel),
        out_shape=jax.ShapeDtypeStruct((b, t, 512), BF16),
        grid=(b, t // tq),
        in_specs=[qspec(512), qspec(128), kspec(128), qspec(512), kspec(128), kspec(128)],
        out_specs=qspec(512),
        scratch_shapes=[pltpu.VMEM((t // ck, tq, ck), I32),
                        pltpu.VMEM((g_per * tq, LANES), F32),
                        pltpu.VMEM((g_per * tq, LANES), F32)],
        compiler_params=_cparams("arbitrary", "arbitrary"),
        name="dsa_prompt",
    )(qi, wi, kib, qa, kab, vab)


def _lambda_value(lamv, lam_init):
    l1 = jnp.sum(lamv[0:1, :] * lamv[1:2, :], axis=-1, keepdims=True)
    l2 = jnp.sum(lamv[2:3, :] * lamv[3:4, :], axis=-1, keepdims=True)
    return jnp.exp(l1) - jnp.exp(l2) + lam_init


def _sub_norm(o, gsub, lam_init):
    ms = jnp.mean(o * o, axis=-1, keepdims=True)
    return o * lax.rsqrt(ms + EPS) * gsub * (1.0 - lam_init)


def _diff_kernel(qb_ref, kb_ref, vb_ref, lamv_ref, gsub_ref, o_ref, m_ref, acc_ref, *, tq, ck, lam_init):
    i = pl.program_id(1)
    n_full = i * (tq // ck)
    lane = lax.broadcasted_iota(I32, (tq, LANES), 1)
    hi_half = lane >= HD
    qb = qb_ref[...]
    zero_b = jnp.zeros((tq, LANES), BF16)
    lam = _lambda_value(lamv_ref[...], lam_init)
    gsub = gsub_ref[...]
    g_per = HB // KVB
    r_in = lax.broadcasted_iota(I32, (tq, ck), 0)
    c_in = lax.broadcasted_iota(I32, (tq, ck), 1)

    for j in range(KVB):
        parts = []
        for g in range(g_per):
            grp = qb[:, (g_per * j + g) * LANES:(g_per * j + g + 1) * LANES]
            parts.append(jnp.where(jnp.logical_not(hi_half), grp, zero_b))
            parts.append(jnp.where(hi_half, grp, zero_b))
        q_rows = jnp.concatenate(parts, axis=0)
        kcols = slice(j * LANES, (j + 1) * LANES)
        _flash_init(m_ref, acc_ref)

        def step(c, bias):
            off = pl.multiple_of(c * ck, ck)
            s = _dot_nt(q_rows, kb_ref[pl.ds(off, ck), kcols])
            if bias is not None:
                s = (s.reshape(2 * g_per, tq, ck) + bias[None]).reshape(2 * g_per * tq, ck)
            _flash_update(s, _pv_pages([_with_ones(vb_ref[pl.ds(off, ck), kcols])]), m_ref, acc_ref)

        def full_body(c, _):
            step(c, None)
            return 0

        lax.fori_loop(0, n_full, full_body, 0)
        for d in range(tq // ck):
            bias = jnp.where(d * ck + c_in <= r_in, 0.0, NEG_BIAS)
            step(n_full + d, bias)

        acc = acc_ref[...]
        o = acc[:, :LANES] / acc[:, LANES:]
        for g in range(g_per):
            o1 = o[(2 * g) * tq:(2 * g + 1) * tq]
            o2 = o[(2 * g + 1) * tq:(2 * g + 2) * tq]
            hh = g_per * j + g
            o_ref[:, hh * LANES:(hh + 1) * LANES] = _sub_norm(o1 - lam * o2, gsub, lam_init).astype(o_ref.dtype)


def _diff_prompt(qb, kbb, vbb, lamv, gsub, lam_init):
    b, t, _ = qb.shape
    tq, ck = (DIFF_Q_BLOCK if t % DIFF_Q_BLOCK == 0 else Q_BLOCK), K_CHUNK
    qspec = lambda w: pl.BlockSpec((None, tq, w), lambda bb, i: (bb, i, 0))
    kspec = lambda w: pl.BlockSpec((None, t, w), lambda bb, i: (bb, 0, 0))
    const = lambda shape: pl.BlockSpec(shape, lambda bb, i: (0,) * len(shape))
    rows = 2 * (HB // KVB) * tq
    return pl.pallas_call(
        functools.partial(_diff_kernel, tq=tq, ck=ck, lam_init=lam_init),
        out_shape=jax.ShapeDtypeStruct((b, t, 512), BF16),
        grid=(b, t // tq),
        in_specs=[qspec(512), kspec(256), kspec(256), const((8, LANES)), const((1, LANES))],
        out_specs=qspec(512),
        scratch_shapes=[pltpu.VMEM((rows, LANES), F32), pltpu.VMEM((rows, 2 * LANES), F32)],
        compiler_params=_cparams("arbitrary", "arbitrary"),
        name="diff_prompt",
    )(qb, kbb, vbb, lamv, gsub)


def _head_sum(r):
    r3 = r.reshape(H_IDX, 8, r.shape[1])
    s = r3[0]
    for h in range(1, H_IDX):
        s = s + r3[h]
    return s + 0.0


def _s1_kernel(pt_ref, qm_ref, wcol_ref, knew_ref, *rest, pg, tnew):
    pages = rest[:pg]
    sc_ref, scn_ref = rest[pg], rest[pg + 1]
    qm = qm_ref[...]
    wcol = wcol_ref[...]
    for i in range(pg):
        s = _dot(qm, pages[i][...].astype(BF16))
        sc_ref[:, i * LANES:(i + 1) * LANES] = _head_sum(jnp.maximum(s, 0.0) * wcol)

    @pl.when(pl.program_id(1) == 0)
    def _():
        s = _dot_nt(qm, knew_ref[...])
        sn = _head_sum(jnp.maximum(s, 0.0) * wcol)
        r = lax.broadcasted_iota(I32, sn.shape, 0)
        c = lax.broadcasted_iota(I32, sn.shape, 1)
        scn_ref[...] = jnp.where(jnp.logical_and(c <= r, c < tnew), sn, -jnp.inf)


def _page_specs(rows, pg):
    return [pl.BlockSpec((None, rows, LANES), functools.partial(lambda b, g, pt, i: (pt[b, g * pg + i], 0, 0), i=i))
            for i in range(pg)]


def _sample_scores(page_table, qm, wcol, knew_i, cik, pg, tnew):
    db, n_pages = page_table.shape
    grid_spec = pltpu.PrefetchScalarGridSpec(
        num_scalar_prefetch=1,
        grid=(db, n_pages // pg),
        in_specs=[pl.BlockSpec((None, 64, D_IDX), lambda b, g, pt: (b, 0, 0)),
                  pl.BlockSpec((None, 64, LANES), lambda b, g, pt: (b, 0, 0)),
                  pl.BlockSpec((None, LANES, D_IDX), lambda b, g, pt: (b, 0, 0))]
                 + _page_specs(D_IDX, pg),
        out_specs=[pl.BlockSpec((None, 8, pg * LANES), lambda b, g, pt: (b, 0, g)),
                   pl.BlockSpec((None, 8, LANES), lambda b, g, pt: (b, 0, 0))],
    )
    return pl.pallas_call(
        functools.partial(_s1_kernel, pg=pg, tnew=tnew),
        out_shape=[jax.ShapeDtypeStruct((db, 8, n_pages * LANES), F32),
                   jax.ShapeDtypeStruct((db, 8, LANES), F32)],
        grid_spec=grid_spec,
        compiler_params=_cparams("arbitrary", "arbitrary"),
        name="sample_scores",
    )(page_table, qm, wcol, knew_i, *([cik] * pg))


def _s1b_kernel(sc_ref, o_ref, skey_ref, *, ksel):
    nc, rows, ck = skey_ref.shape

    def conv(c, _):
        x = sc_ref[c]
        skey_ref[c] = jnp.where(x == -jnp.inf, INT_MIN, _sortable_key(x))
        return 0

    lax.fori_loop(0, nc, conv, 0)
    thr = _select_topk(skey_ref, nc, ksel, rows, ck)

    def emit(c, _):
        o_ref[c] = jnp.where(skey_ref[c] >= thr, 0.0, NEG_BIAS)
        return 0

    lax.fori_loop(0, nc, emit, 0)


def _sample_select(sct, ksel):
    nc, r, _ = sct.shape
    rb = min(r, 128)
    spec = pl.BlockSpec((nc, rb, LANES), lambda i: (0, i, 0))
    return pl.pallas_call(
        functools.partial(_s1b_kernel, ksel=ksel),
        out_shape=jax.ShapeDtypeStruct(sct.shape, F32),
        grid=(r // rb,),
        in_specs=[spec],
        out_specs=spec,
        scratch_shapes=[pltpu.VMEM((nc, rb, LANES), I32)],
        compiler_params=_cparams("arbitrary"),
        name="sample_select",
    )(sct)


def _s2_kernel(pt_ref, qa_ref, qb_ref, selp_ref, seln_ref, kan_ref, van_ref, kbn_ref, vbn_ref,
               lamv_ref, gsub_ref, ak_hbm, av_hbm, bk_hbm, bv_hbm, oa_ref, ob_ref,
               ma_ref, acca_ref, mb_ref, accb_ref, ak_buf, av_buf, bk_buf, bv_buf, sems, *, pg, tnew, lam_init):
    g = pl.program_id(1)
    n_groups = pl.num_programs(1)
    step = pl.program_id(0) * n_groups + g
    slot = step % 2

    def page_copies(s, sl):
        b_of, g_of = s // n_groups, s % n_groups
        out = []
        for i in range(pg):
            page = pt_ref[b_of, g_of * pg + i]
            for hbm, buf in ((ak_hbm, ak_buf), (av_hbm, av_buf), (bk_hbm, bk_buf), (bv_hbm, bv_buf)):
                out.append(pltpu.make_async_copy(hbm.at[page], buf.at[sl, i], sems.at[sl]))
        return out

    @pl.when(step == 0)
    def _():
        for cp in page_copies(step, slot):
            cp.start()

    @pl.when(step + 1 < pl.num_programs(0) * n_groups)
    def _():
        for cp in page_copies(step + 1, 1 - slot):
            cp.start()

    for cp in page_copies(step, slot):
        cp.wait()

    akp = [ak_buf.at[slot, i] for i in range(pg)]
    avp = [av_buf.at[slot, i] for i in range(pg)]
    bkp = [bk_buf.at[slot, i] for i in range(pg)]
    bvp = [bv_buf.at[slot, i] for i in range(pg)]
    qa = qa_ref[...]
    qb = qb_ref[...]

    rows_j = qb.shape[0] // KVB
    ones_kd = jnp.ones((LANES, LANES), BF16)

    def pv_b(values_of_kv):
        def fn(p):
            parts = []
            for j in range(KVB):
                upd = None
                for i in range(p.shape[1] // LANES):
                    va = jnp.concatenate([values_of_kv(i, j), ones_kd], axis=1)
                    d = _dot(p[j * rows_j:(j + 1) * rows_j, i * LANES:(i + 1) * LANES], va)
                    upd = d if upd is None else upd + d
                parts.append(upd)
            return jnp.concatenate(parts, axis=0)
        return fn

    @pl.when(g == 0)
    def _():
        _flash_init(ma_ref, acca_ref)
        _flash_init(mb_ref, accb_ref)
        sa = _dot_nt(qa, kan_ref[...]) + jnp.tile(seln_ref[...], (HA, 1))
        _flash_update(sa, _pv_pages([_with_ones(van_ref[...])]), ma_ref, acca_ref)
        r = lax.broadcasted_iota(I32, (8, LANES), 0)
        c = lax.broadcasted_iota(I32, (8, LANES), 1)
        causal = jnp.where(jnp.logical_and(c <= r, c < tnew), 0.0, NEG_BIAS)
        sb = _dot_nt(qb, kbn_ref[...]) + jnp.tile(causal, (8, 1))
        _flash_update(sb, pv_b(lambda i, j: vbn_ref[:, j * LANES:(j + 1) * LANES]), mb_ref, accb_ref)

    sa = jnp.concatenate([_dot(qa, akp[i][...].astype(BF16)) for i in range(pg)], axis=1)
    sa = sa + jnp.tile(selp_ref[...], (HA, 1))

    def pv_a(p):
        upd = None
        for i in range(pg):
            vt = jnp.concatenate([avp[i][...].astype(BF16), ones_kd], axis=0)
            d = _dot_nt(p[:, i * LANES:(i + 1) * LANES], vt)
            upd = d if upd is None else upd + d
        return upd

    _flash_update(sa, pv_a, ma_ref, acca_ref)
    sb = jnp.concatenate([_dot(qb, bkp[i][...].astype(BF16)) for i in range(pg)], axis=1)
    _flash_update(sb, pv_b(lambda i, j: bvp[i][pl.ds(j, LANES, stride=KVB), :].astype(BF16)), mb_ref, accb_ref)

    @pl.when(g == pl.num_programs(1) - 1)
    def _():
        acca = acca_ref[...]
        oa_ref[...] = acca[:, :LANES] / acca[:, LANES:]
        accb = accb_ref[...]
        x = (accb[:, :LANES] / accb[:, LANES:]).reshape(HB, 2, 8, LANES)
        lam = _lambda_value(lamv_ref[...], lam_init)
        gsub = gsub_ref[...]
        for hh in range(HB):
            ob_ref[:, hh * LANES:(hh + 1) * LANES] = _sub_norm(x[hh, 0] - lam * x[hh, 1], gsub, lam_init)


def _sample_attend(page_table, qa_rows, qb_rows, selp, seln, kan, van, kbn, vbn, lamv, gsub,
                   cak, cav, cbk, cbv, pg, tnew, lam_init):
    db, n_pages = page_table.shape
    per_b = lambda *shape: pl.BlockSpec((None,) + shape, lambda b, g, pt: (b,) + (0,) * len(shape))
    const = lambda shape: pl.BlockSpec(shape, lambda b, g, pt: (0,) * len(shape))
    in_hbm = pl.BlockSpec(memory_space=pl.ANY)
    page_buf = lambda arr: pltpu.VMEM((2, pg) + arr.shape[1:], arr.dtype)
    grid_spec = pltpu.PrefetchScalarGridSpec(
        num_scalar_prefetch=1,
        grid=(db, n_pages // pg),
        in_specs=[per_b(64, 128), per_b(64, 256),
                  pl.BlockSpec((None, 8, pg * LANES), lambda b, g, pt: (b, 0, g)), per_b(8, LANES),
                  per_b(LANES, 128), per_b(LANES, 128), per_b(LANES, 256), per_b(LANES, 256),
                  const((8, LANES)), const((1, LANES)), in_hbm, in_hbm, in_hbm, in_hbm],
        out_specs=[per_b(64, 128), per_b(8, 512)],
        scratch_shapes=[pltpu.VMEM((64, LANES), F32), pltpu.VMEM((64, 2 * LANES), F32),
                        pltpu.VMEM((64, LANES), F32), pltpu.VMEM((64, 2 * LANES), F32),
                        page_buf(cak), page_buf(cav), page_buf(cbk), page_buf(cbv),
                        pltpu.SemaphoreType.DMA((2,))],
    )
    return pl.pallas_call(
        functools.partial(_s2_kernel, pg=pg, tnew=tnew, lam_init=lam_init),
        out_shape=[jax.ShapeDtypeStruct((db, 64, 128), F32), jax.ShapeDtypeStruct((db, 8, 512), F32)],
        grid_spec=grid_spec,
        compiler_params=_cparams("arbitrary", "arbitrary"),
        name="sample_attend",
    )(page_table, qa_rows, qb_rows, selp, seln, kan, van, kbn, vbn, lamv, gsub, cak, cav, cbk, cbv)


TOKEN_ROWS = 8


def _outproj_kernel(oa_ref, ob_ref, x_ref, gt_ref, sh_ref, sc_ref, g_ref, wo_ref, wr_ref, br_ref,
                    h_ref, xm_ref, comb_ref, sel_ref):
    half = oa_ref.shape[1]
    tm = x_ref.shape[0]
    o = _dot(oa_ref[...].astype(BF16), wo_ref[:half, :]) + _dot(ob_ref[...].astype(BF16), wo_ref[half:, :])
    h = x_ref[...] + gt_ref[...] * o
    h_ref[...] = h
    ms = jnp.mean(h * h, axis=-1, keepdims=True)
    xm = h * lax.rsqrt(ms + EPS) * g_ref[...]
    xm = xm * (1.0 + sc_ref[...]) + sh_ref[...]
    for c in range(TOKEN_ROWS):
        xm_ref[pl.ds(c, tm, stride=TOKEN_ROWS), :] = xm[:, c * LANES:(c + 1) * LANES]
    logits = _dot3(xm, wr_ref[...]) + br_ref[...]
    lane = lax.broadcasted_iota(I32, logits.shape, 1).astype(F32)
    vals, hots = [], []
    for _ in range(TOP_K):
        m = jnp.max(logits, axis=-1, keepdims=True)
        idx = jnp.min(jnp.where(logits == m, lane, float(LANES)), axis=-1, keepdims=True)
        hot = lane == idx
        vals.append(m)
        hots.append(hot)
        logits = jnp.where(hot, -3e38, logits)
    es = [jnp.exp(v - vals[0]) for v in vals]
    den = es[0] + es[1] + es[2] + es[3]
    comb = jnp.zeros(logits.shape, F32)
    sel = jnp.zeros(logits.shape, F32)
    for e, hot in zip(es, hots):
        comb = comb + jnp.where(hot, e / den, 0.0)
        sel = sel + jnp.where(hot, 1.0, 0.0)
    comb_ref[...] = comb
    sel_ref[...] = sel.astype(BF16)


def _outproj(oa, ob, x2, gt, sh, sc, g_ffn, w_out_b, w_router_p, b_router_p, tm, steps_per_seq):
    n, d = x2.shape
    assert d == TOKEN_ROWS * LANES
    const = lambda shape: pl.BlockSpec(shape, lambda i: (0,) * len(shape))
    row = lambda w: pl.BlockSpec((tm, w), lambda i: (i, 0))
    return pl.pallas_call(
        _outproj_kernel,
        out_shape=[jax.ShapeDtypeStruct((n, d), F32), jax.ShapeDtypeStruct((n * TOKEN_ROWS, LANES), F32),
                   jax.ShapeDtypeStruct((n, LANES), F32), jax.ShapeDtypeStruct((n, LANES), BF16)],
        grid=(n // tm,),
        in_specs=[row(oa.shape[1]), row(ob.shape[1]), row(d),
                  _mod_spec(gt, tm, steps_per_seq), _mod_spec(sh, tm, steps_per_seq),
                  _mod_spec(sc, tm, steps_per_seq),
                  const((1, d)), const(w_out_b.shape), const((d, LANES)), const((1, LANES))],
        out_specs=[row(d), pl.BlockSpec((tm * TOKEN_ROWS, LANES), lambda i: (i, 0)), row(LANES), row(LANES)],
        compiler_params=_cparams("arbitrary"),
        name="outproj",
    )(oa, ob, x2, gt, sh, sc, g_ffn.reshape(1, d), w_out_b, w_router_p, b_router_p)


MOE_TILE = 256
TAB_ROWS = 8


def _rank_kernel(sel_ref, tri_ref, before_ref, base_ref, cnt_ref, total_ref, carry_ref):
    @pl.when(pl.program_id(0) == 0)
    def _():
        carry_ref[...] = jnp.zeros(carry_ref.shape, F32)

    sel = sel_ref[...]
    tm = sel.shape[0]
    before = _dot(tri_ref[...], sel)
    before_ref[...] = before
    carry = carry_ref[...]
    base_ref[...] = carry
    cnt = jnp.broadcast_to(before[tm - 1:tm, :] + sel[tm - 1:tm, :].astype(F32), carry.shape)
    cnt_ref[...] = cnt
    carry = carry + cnt
    carry_ref[...] = carry
    total_ref[...] = carry


def _moe_rank(sel, tm):
    n = sel.shape[0]
    r = lax.broadcasted_iota(I32, (tm, tm), 0)
    c = lax.broadcasted_iota(I32, (tm, tm), 1)
    tri = jnp.where(c < r, 1.0, 0.0).astype(BF16)
    per_tile = pl.BlockSpec((8, LANES), lambda i: (i, 0))
    return pl.pallas_call(
        _rank_kernel,
        out_shape=[jax.ShapeDtypeStruct((n, LANES), F32), jax.ShapeDtypeStruct((n // tm * 8, LANES), F32),
                   jax.ShapeDtypeStruct((n // tm * 8, LANES), F32), jax.ShapeDtypeStruct((8, LANES), F32)],
        grid=(n // tm,),
        in_specs=[pl.BlockSpec((tm, LANES), lambda i: (i, 0)), pl.BlockSpec((tm, tm), lambda i: (0, 0))],
        out_specs=[pl.BlockSpec((tm, LANES), lambda i: (i, 0)), per_tile, per_tile,
                   pl.BlockSpec((8, LANES), lambda i: (0, 0))],
        scratch_shapes=[pltpu.VMEM((8, LANES), F32)],
        compiler_params=_cparams("arbitrary"),
        name="moe_rank",
    )(sel, tri)


def _slots_kernel(before_ref, sel_ref, comb_ref, base_ref, cnt_ref, starts_ref, upper_ref,
                  pos_ref, gate_ref, tab_ref):
    cnt = cnt_ref[...]
    offs = _dot(cnt.astype(BF16), upper_ref[...])
    posv = before_ref[...] + offs[0:1, :]
    comb = comb_ref[...]
    sel = sel_ref[...].astype(F32) > 0.0
    lane = lax.broadcasted_iota(I32, posv.shape, 1)
    lanef = lane.astype(F32)
    pos = jnp.zeros(posv.shape, F32)
    gates = jnp.zeros(posv.shape, F32)
    for k in range(TOP_K):
        idx = jnp.min(jnp.where(sel, lanef, float(LANES)), axis=-1, keepdims=True)
        hot = lanef == idx
        pk = jnp.sum(jnp.where(hot, posv, 0.0), axis=-1, keepdims=True)
        gk = jnp.sum(jnp.where(hot, comb, 0.0), axis=-1, keepdims=True)
        pos = jnp.where(lane == k, pk, pos)
        gates = jnp.where(lane == k, gk, gates)
        sel = jnp.logical_and(sel, jnp.logical_not(hot))
    pos_ref[...] = pos.astype(I32)
    gate_ref[...] = gates
    first = starts_ref[...] + base_ref[0:1, :]
    tab = jnp.concatenate([first, cnt[0:1, :], offs[0:1, :], jnp.zeros((TAB_ROWS - 3, LANES), F32)], axis=0)
    tab_ref[...] = tab.astype(I32)


def _moe_slots(before, sel, comb, base, cnt, starts, tm):
    n = before.shape[0]
    row = pl.BlockSpec((tm, LANES), lambda i: (i, 0))
    per_tile = pl.BlockSpec((8, LANES), lambda i: (i, 0))
    r = lax.broadcasted_iota(I32, (LANES, LANES), 0)
    c = lax.broadcasted_iota(I32, (LANES, LANES), 1)
    upper = jnp.where(r < c, 1.0, 0.0).astype(BF16)
    return pl.pallas_call(
        _slots_kernel,
        out_shape=[jax.ShapeDtypeStruct((n, LANES), I32), jax.ShapeDtypeStruct((n, LANES), F32),
                   jax.ShapeDtypeStruct((n // tm * TAB_ROWS, LANES), I32)],
        grid=(n // tm,),
        in_specs=[row, row, row, per_tile, per_tile, pl.BlockSpec((1, LANES), lambda i: (0, 0)),
                  pl.BlockSpec((LANES, LANES), lambda i: (0, 0))],
        out_specs=[row, row, pl.BlockSpec((TAB_ROWS, LANES), lambda i: (i, 0))],
        compiler_params=_cparams("arbitrary"),
        name="moe_slots",
    )(before, sel, comb, base, cnt, starts, upper)


def _slab_rows(ref, index, count=1):
    return ref.at[pl.ds(pl.multiple_of(index * TOKEN_ROWS, TOKEN_ROWS), count * TOKEN_ROWS), :]


def _run_copies(src_ref, dst_ref, src0, dst0, length, max_len, sem, wait):
    for b in range(max_len.bit_length() - 1, -1, -1):
        size = 1 << b

        @pl.when((length & size) != 0)
        def _():
            done = (length >> (b + 1)) << (b + 1)
            cp = pltpu.make_async_copy(_slab_rows(src_ref, src0 + done, size),
                                       _slab_rows(dst_ref, dst0 + done, size), sem)
            if wait:
                cp.wait()
            else:
                cp.start()


def _for_each_run(tab_ref, fn):
    def body(e, carry):
        fn(tab_ref[e], tab_ref[LANES + e], tab_ref[2 * LANES + e])
        return carry
    lax.fori_loop(0, N_EXPERTS, body, 0)


def _dispatch_kernel(pos_ref, tab_ref, x_ref, xs_ref, stage_ref, sem):
    tm = x_ref.shape[0] // TOKEN_ROWS

    def place(t, carry):
        v = _slab_rows(x_ref, t)[...]
        for k in range(TOP_K):
            _slab_rows(stage_ref, pos_ref[t * TOP_K + k])[...] = v
        return carry

    lax.fori_loop(0, tm, place, 0)
    for wait in (False, True):
        _for_each_run(tab_ref, lambda first, length, off, wait=wait: _run_copies(
            stage_ref, xs_ref, off, first, length, tm, sem, wait))


def _smem_words(words):
    return pl.BlockSpec((words,), lambda i: (i,), memory_space=pltpu.SMEM)


def _moe_dispatch(pos_flat, tab_flat, xm_slabs, tm):
    n = xm_slabs.shape[0] // TOKEN_ROWS
    return pl.pallas_call(
        _dispatch_kernel,
        out_shape=jax.ShapeDtypeStruct((n * TOP_K * TOKEN_ROWS, LANES), F32),
        grid=(n // tm,),
        in_specs=[_smem_words(tm * TOP_K), _smem_words(TAB_ROWS * LANES),
                  pl.BlockSpec((tm * TOKEN_ROWS, LANES), lambda i: (i, 0))],
        out_specs=pl.BlockSpec(memory_space=pl.ANY),
        scratch_shapes=[pltpu.VMEM((tm * TOP_K * TOKEN_ROWS, LANES), F32), pltpu.SemaphoreType.DMA],
        compiler_params=_cparams("arbitrary"),
        name="moe_dispatch",
    )(pos_flat, tab_flat, xm_slabs)


def _ffn_kernel(tile_ref, exp_ref, lo_ref, hi_ref, xs_ref, wgu_ref, bgu_ref, wd_ref, bd_ref, ys_ref,
                wgu_b_ref, wd_b_ref):
    g = pl.program_id(0)
    lo, hi = lo_ref[g], hi_ref[g]
    tme = xs_ref.shape[0] // TOKEN_ROWS
    dff = wd_ref.shape[0]

    @pl.when(jnp.logical_or(g == 0, exp_ref[g] != exp_ref[jnp.maximum(g - 1, 0)]))
    def _():
        wgu_b_ref[...] = wgu_ref[...].astype(BF16)
        wd_b_ref[...] = wd_ref[...].astype(BF16)

    @pl.when(hi > lo)
    def _():
        x = jnp.concatenate([xs_ref[pl.ds(c, tme, stride=TOKEN_ROWS), :] for c in range(TOKEN_ROWS)], axis=1)
        hgu = _dot(x.astype(BF16), wgu_b_ref[...]) + bgu_ref[...]
        gate = jnp.minimum(hgu[:, :dff], SWIGLU_LIMIT)
        up = jnp.clip(hgu[:, dff:], -SWIGLU_LIMIT, SWIGLU_LIMIT)
        a = (up + 1.0) * (gate * jax.nn.sigmoid(SWIGLU_ALPHA * gate))
        yo = _dot(a.astype(BF16), wd_b_ref[...]) + bd_ref[...]
        row = lax.broadcasted_iota(I32, (tme, LANES), 0)
        mine = jnp.logical_and(row >= lo, row < hi)

        @pl.when(lo == 0)
        def _():
            for c in range(TOKEN_ROWS):
                ys_ref[pl.ds(c, tme, stride=TOKEN_ROWS), :] = yo[:, c * LANES:(c + 1) * LANES]

        @pl.when(lo != 0)
        def _():
            for c in range(TOKEN_ROWS):
                dst = ys_ref.at[pl.ds(c, tme, stride=TOKEN_ROWS), :]
                dst[...] = jnp.where(mine, yo[:, c * LANES:(c + 1) * LANES], dst[...])


def _moe_plan(counts, n_slots, tme):
    cnt = counts[:N_EXPERTS].astype(I32)
    ends = jnp.cumsum(cnt)
    starts = ends - cnt
    first_tile = starts // tme
    n_items = jnp.where(cnt > 0, (ends - 1) // tme - first_tile + 1, 0)
    item_end = jnp.cumsum(n_items)
    item_start = item_end - n_items
    n_work = n_slots // tme + N_EXPERTS - 1
    g = jnp.arange(n_work, dtype=I32)
    gi = jnp.minimum(g, item_end[-1] - 1)
    e = jnp.sum((item_end[None, :] <= gi[:, None]).astype(I32), axis=1)
    tile = first_tile[e] + (gi - item_start[e])
    lo = jnp.maximum(starts[e], tile * tme) - tile * tme
    hi = jnp.minimum(ends[e], (tile + 1) * tme) - tile * tme
    real = g < item_end[-1]
    return tile, e, jnp.where(real, lo, 0), jnp.where(real, hi, 0), starts


def _moe_ffn(plan, xs, w_gu, b_gu, w_down, b_down, tme):
    tile, e, lo, hi = plan
    ne, d, dff2 = w_gu.shape
    dff = dff2 // 2
    slab_spec = pl.BlockSpec((tme * TOKEN_ROWS, LANES), lambda g, tile, e, lo, hi: (tile[g], 0))
    per_e = lambda *shape: pl.BlockSpec((None,) + shape, lambda g, tile, e, lo, hi: (e[g],) + (0,) * len(shape))
    grid_spec = pltpu.PrefetchScalarGridSpec(
        num_scalar_prefetch=4,
        grid=(tile.shape[0],),
        in_specs=[slab_spec, per_e(d, dff2), per_e(1, dff2), per_e(dff, d), per_e(1, d)],
        out_specs=slab_spec,
        scratch_shapes=[pltpu.VMEM((d, dff2), BF16), pltpu.VMEM((dff, d), BF16)],
    )
    return pl.pallas_call(
        _ffn_kernel,
        out_shape=jax.ShapeDtypeStruct(xs.shape, F32),
        grid_spec=grid_spec,
        compiler_params=_cparams("arbitrary"),
        name="moe_ffn",
    )(tile, e, lo, hi, xs, w_gu, b_gu.reshape(ne, 1, dff2), w_down, b_down.reshape(ne, 1, d))


def _combine_kernel(pos_ref, gate_ref, tab_ref, h_ref, gt_ref, ys_ref, y_ref, stage_ref, acc_ref, sem):
    tm = h_ref.shape[0]
    for wait in (False, True):
        _for_each_run(tab_ref, lambda first, length, off, wait=wait: _run_copies(
            ys_ref, stage_ref, first, off, length, tm, sem, wait))

    def gather(t, carry):
        acc = None
        for k in range(TOP_K):
            part = gate_ref[t * TOP_K + k] * _slab_rows(stage_ref, pos_ref[t * TOP_K + k])[...]
            acc = part if acc is None else acc + part
        _slab_rows(acc_ref, t)[...] = acc
        return carry

    lax.fori_loop(0, tm, gather, 0, unroll=4)
    gt = gt_ref[...]
    for c in range(TOKEN_ROWS):
        sl = slice(c * LANES, (c + 1) * LANES)
        y_ref[:, sl] = h_ref[:, sl] + gt[:, sl] * acc_ref[pl.ds(c, tm, stride=TOKEN_ROWS), :]


def _moe_combine(pos_flat, gate_flat, tab_flat, h, gt, ys, tm, steps_per_seq):
    n, d = h.shape
    return pl.pallas_call(
        _combine_kernel,
        out_shape=jax.ShapeDtypeStruct((n, d), F32),
        grid=(n // tm,),
        in_specs=[_smem_words(tm * TOP_K), _smem_words(tm * TOP_K), _smem_words(TAB_ROWS * LANES),
                  pl.BlockSpec((tm, d), lambda i: (i, 0)),
                  _mod_spec(gt, tm, steps_per_seq), pl.BlockSpec(memory_space=pl.ANY)],
        out_specs=pl.BlockSpec((tm, d), lambda i: (i, 0)),
        scratch_shapes=[pltpu.VMEM((tm * TOP_K * TOKEN_ROWS, LANES), F32),
                        pltpu.VMEM((tm * TOKEN_ROWS, LANES), F32), pltpu.SemaphoreType.DMA],
        compiler_params=_cparams("arbitrary"),
        name="moe_combine",
    )(pos_flat, gate_flat, tab_flat, h, gt, ys)


def _moe(xm_slabs, comb, sel, h, gt, wgu_b, b_gu, wd_b, b_down, tokens_per_seq):
    n, d = h.shape
    tm = min(MOE_TILE, _token_tile(n))
    tme = min(512, _token_tile(n * TOP_K))
    before, base, cnt, total = _moe_rank(sel, tm)
    tile, e, lo, hi, starts = _moe_plan(total[0], n * TOP_K, tme)
    starts_row = jnp.zeros((1, LANES), F32).at[0, :N_EXPERTS].set(starts.astype(F32))
    pos4, gate4, tab = _moe_slots(before, sel, comb, base, cnt, starts_row, tm)
    pos_flat = pos4[:, :TOP_K].reshape(n * TOP_K)
    gate_flat = gate4[:, :TOP_K].reshape(n * TOP_K)
    tab_flat = tab.reshape(-1)
    xs = _moe_dispatch(pos_flat, tab_flat, xm_slabs, tm)
    ys = _moe_ffn((tile, e, lo, hi), xs, wgu_b, b_gu, wd_b, b_down, tme)
    return _moe_combine(pos_flat, gate_flat, tab_flat, h, gt, ys, tm, max(1, tokens_per_seq // tm))


def _token_tile(n):
    for tm in (512, 256, 128, 64, 32, 16, 8):
        if n % tm == 0:
            return tm
    raise ValueError(f"token count {n} must be a multiple of 8")


def kernel(x_prompt, x_sample, cache_a_k, cache_a_v, cache_idx_k, cache_b_k, cache_b_v, page_table,
           c_prompt, c_sample, w_ada, b_ada, g_attn, w_in, g_qa, g_ka, g_ki, g_qb, g_kb,
           lam_q1, lam_k1, lam_q2, lam_k2, g_sub, w_out, g_ffn, w_router, b_router,
           w_gu, b_gu, w_down, b_down):
    bsz, t_p, d = x_prompt.shape
    db, t_s, _ = x_sample.shape
    depth = w_in.shape[0]
    n_pool, page = cache_a_k.shape[1], cache_a_k.shape[2]
    n_pages = page_table.shape[1]
    past = n_pages * page
    assert page == LANES and t_s == 8 and t_p % Q_BLOCK == 0 and d == 1024
    n_p, n_s = bsz * t_p, db * t_s
    pg_scores, pg_attend = math.gcd(n_pages, 32), math.gcd(n_pages, 16)

    cs_p, sn_p = _rope_tables(jnp.arange(t_p, dtype=I32))
    cs_s, sn_s = _rope_tables(past + jnp.arange(t_s, dtype=I32))
    cs_s, sn_s = jnp.tile(cs_s, (db, 1)), jnp.tile(sn_s, (db, 1))
    tile2 = lambda g: jnp.concatenate([g, g]).astype(F32)
    ksel_p = min(TOPK_MAX, t_p // 4)
    ksel_s = min(TOPK_MAX, (past + t_s) // 4)
    tm_p, tm_s = _token_tile(t_p), _token_tile(n_s)
    sps_p = t_p // tm_p
    per_tok = lambda m: jnp.repeat(m, t_s, axis=0)

    hp = x_prompt.reshape(n_p, d)
    hs = x_sample.reshape(n_s, d)
    rows_p = [[] for _ in range(5)]
    rows_s = [[] for _ in range(5)]
    for l in range(depth):
        lam_init = 0.8 - 0.6 * math.exp(-0.3 * l)
        mods = _adaln(jnp.concatenate([c_prompt, c_sample], axis=0), w_ada[l], b_ada[l])
        mp = [m.reshape(bsz, 1, d) for m in jnp.split(mods[:bsz], 6, axis=-1)]
        ms_ = [per_tok(m) for m in jnp.split(mods[bsz:], 6, axis=-1)]
        w_perm = _permute_w_in(w_in[l])
        gains = jnp.zeros((8, LANES), F32).at[:5].set(
            jnp.stack([tile2(g_qa[l]), tile2(g_ka[l]), tile2(g_ki[l]), tile2(g_qb[l]), tile2(g_kb[l])]))
        lamv = jnp.zeros((8, LANES), F32).at[:4, :HD].set(
            jnp.stack([lam_q1[l], lam_k1[l], lam_q2[l], lam_k2[l]]).astype(F32))
        gsub = g_sub[l].reshape(1, VB).astype(F32)
        w_out_b = w_out[l].astype(BF16)
        w_router_p = jnp.zeros((d, LANES), F32).at[:, :N_EXPERTS].set(w_router[l])
        b_router_p = jnp.full((1, LANES), NEG_BIAS, F32).at[0, :N_EXPERTS].set(b_router[l])
        wgu_b = w_gu[l]
        wd_b = w_down[l]

        (qa, qi, qb, ka, kb, ki, wi, va, vb, kab, kbb, kib, vab, vbb) = _proj(
            hp, mp[0], mp[1], g_attn[l], w_perm, cs_p, sn_p, gains, tm_p, sps_p, True)
        r3 = lambda a: a.reshape(bsz, t_p, a.shape[-1])
        oa = _dsa_prompt(r3(qi), r3(wi), r3(kib), r3(qa), r3(kab), r3(vab), ksel_p)
        ob = _diff_prompt(r3(qb), r3(kbb), r3(vbb), lamv, gsub, lam_init)
        h1, xm, comb, sel = _outproj(oa.reshape(n_p, 512), ob.reshape(n_p, 512), hp, mp[2], mp[3], mp[4],
                                     g_ffn[l], w_out_b, w_router_p, b_router_p, tm_p, sps_p)
        hp = _moe(xm, comb, sel, h1, mp[5], wgu_b, b_gu[l], wd_b, b_down[l], t_p)
        for lst, r in zip(rows_p, (ka.reshape(bsz, KVA, HD, t_p).transpose(0, 3, 1, 2),
                                   va.reshape(bsz, KVA, HD, t_p).transpose(0, 3, 1, 2),
                                   ki.transpose(0, 2, 1),
                                   kb.reshape(bsz, KVB, 2, HD, t_p).transpose(0, 4, 1, 2, 3),
                                   vb.reshape(bsz, t_p, KVB, VB))):
            lst.append(r)

        (qa, qi, qb, ka, kb, ki, wi, va, vb, kab, kbb, kib, vab, vbb) = _proj(
            hs, ms_[0], ms_[1], g_attn[l], w_perm, cs_s, sn_s, gains, tm_s, 1, False)
        qm = qi.reshape(db, t_s, H_IDX, D_IDX).transpose(0, 2, 1, 3).reshape(db, 64, D_IDX)
        wcol = jnp.broadcast_to(
            wi[:, :H_IDX].reshape(db, t_s, H_IDX).transpose(0, 2, 1).reshape(db, 64, 1), (db, 64, LANES))
        pad_new = lambda a: jnp.pad(a.reshape(db, t_s, a.shape[-1]), ((0, 0), (0, LANES - t_s), (0, 0)))
        idx_t = jnp.transpose(cache_idx_k[l], (0, 2, 1))
        ak_t = jnp.transpose(cache_a_k[l], (0, 2, 3, 1)).reshape(n_pool, KVA * HD, page)
        av_t = jnp.transpose(cache_a_v[l], (0, 2, 3, 1)).reshape(n_pool, KVA * HD, page)
        bk_t = jnp.transpose(cache_b_k[l], (0, 2, 3, 4, 1)).reshape(n_pool, KVB * 2 * HD, page)
        bv_r = cache_b_v[l].reshape(n_pool, page * KVB, VB)
        sc_past, sc_new = _sample_scores(page_table, qm, wcol, pad_new(kib[:, :D_IDX]), idx_t, pg_scores, t_s)
        sct = jnp.concatenate([sc_past, sc_new], axis=-1).reshape(n_s, n_pages + 1, LANES).transpose(1, 0, 2)
        sel = _sample_select(sct, ksel_s).transpose(1, 0, 2).reshape(db, t_s, (n_pages + 1) * LANES)
        qa4 = qa.reshape(db, t_s, HA, HD).transpose(0, 2, 1, 3)
        zq = jnp.zeros_like(qa4[:, :HA // 2])
        qa_rows = jnp.concatenate([jnp.concatenate([qa4[:, :HA // 2], zq], axis=-1),
                                   jnp.concatenate([zq, qa4[:, HA // 2:]], axis=-1)], axis=1).reshape(db, 64, 128)
        qb5 = qb.reshape(db, t_s, HB, 2, HD).transpose(0, 2, 3, 1, 4)
        zb = jnp.zeros_like(qb5[:, 0, 0])
        blocks = []
        for hh in range(HB):
            for m in range(2):
                pos = (hh // (HB // KVB)) * 2 + m
                blocks.append(jnp.concatenate([qb5[:, hh, m] if p == pos else zb for p in range(4)], axis=-1))
        qb_rows = jnp.stack(blocks, axis=1).reshape(db, 64, 256)
        oa_rows, ob = _sample_attend(
            page_table, qa_rows, qb_rows, sel[:, :, :past], sel[:, :, past:],
            pad_new(kab), pad_new(vab), pad_new(kbb), pad_new(vbb), lamv, gsub,
            ak_t, av_t, bk_t, bv_r, pg_attend, t_s, lam_init)
        oa5 = oa_rows.reshape(db, HA, t_s, KVA, HD)
        oa = jnp.concatenate([oa5[:, :HA // 2, :, 0], oa5[:, HA // 2:, :, 1]], axis=1)
        oa = oa.transpose(0, 2, 1, 3).reshape(n_s, HA * HD)
        h1, xm, comb, sel = _outproj(oa, ob.reshape(n_s, 512), hs, ms_[2], ms_[3], ms_[4],
                                     g_ffn[l], w_out_b, w_router_p, b_router_p, tm_s, 1)
        hs = _moe(xm, comb, sel, h1, ms_[5], wgu_b, b_gu[l], wd_b, b_down[l], t_s)
        for lst, r in zip(rows_s, (ka.reshape(db, t_s, KVA, HD), va.reshape(db, t_s, KVA, HD),
                                   ki.reshape(db, t_s, D_IDX), kb.reshape(db, t_s, KVB, 2, HD),
                                   vb.reshape(db, t_s, KVB, VB))):
            lst.append(r)

    outs_p = [jnp.stack(r) for r in rows_p]
    outs_s = [jnp.stack(r) for r in rows_s]
    return (hp.reshape(bsz, t_p, d), hs.reshape(db, t_s, d), *outs_p, *outs_s)
```

```python
import functools
import math

import jax
import jax.numpy as jnp
from jax import lax
from jax.experimental import pallas as pl
from jax.experimental.pallas import tpu as pltpu

F32 = jnp.float32
BF16 = jnp.bfloat16
I32 = jnp.int32

HD = 64
HA = 8
KVA = 2
H_IDX = 8
D_IDX = 64
HB = 4
KVB = 2
VB = 128
N_EXPERTS = 32
TOP_K = 4
TOPK_MAX = 256
SWIGLU_LIMIT = 7.0
SWIGLU_ALPHA = 1.702
ROPE_THETA = 10000.0
EPS = 1e-6
Q_BLOCK = 256
DSA_Q_BLOCK = 512
DIFF_Q_BLOCK = 512
K_CHUNK = 256
LANES = 128
INT_MIN = -2 ** 31
NEG_BIAS = -1e30
M_INIT = -1e20
VMEM_LIMIT = 56 * 1024 * 1024
Q_SCALE = HD ** -0.5 * math.log2(math.e)

C_QA, C_QI, C_QB, C_KA, C_KB, C_KI, C_WI, C_VA, C_VB, C_END = (
    0, 512, 1024, 1536, 1664, 1920, 2048, 2176, 2304, 2560)


def _dot(a, b):
    return jnp.dot(a, b, preferred_element_type=F32)


def _dot_nt(a, b):
    return lax.dot_general(a, b, (((1,), (1,)), ((), ())), preferred_element_type=F32)


def _split_bf16(a):
    hi = a.astype(BF16)
    lo = (a - hi.astype(F32)).astype(BF16)
    return hi, lo


def _dot3(a, b):
    ah, al = _split_bf16(a)
    bh, bl = _split_bf16(b)
    return _dot(ah, bh) + (_dot(ah, bl) + _dot(al, bh))


def _cparams(*sem):
    return pltpu.CompilerParams(dimension_semantics=sem, vmem_limit_bytes=VMEM_LIMIT)


def _adaln_kernel(c_ref, w_ref, b_ref, o_ref):
    c = c_ref[...]
    s = c * jax.nn.sigmoid(c)
    o_ref[...] = _dot3(s, w_ref[...]) + b_ref[...]


def _adaln(c, w_ada, b_ada):
    n, d = c.shape
    e = w_ada.shape[1]
    tn = 1024
    return pl.pallas_call(
        _adaln_kernel,
        out_shape=jax.ShapeDtypeStruct((n, e), F32),
        grid=(e // tn,),
        in_specs=[pl.BlockSpec((n, d), lambda j: (0, 0)),
                  pl.BlockSpec((d, tn), lambda j: (0, j)),
                  pl.BlockSpec((1, tn), lambda j: (0, j))],
        out_specs=pl.BlockSpec((n, tn), lambda j: (0, j)),
        compiler_params=_cparams("arbitrary"),
        name="adaln",
    )(c, w_ada, b_ada.reshape(1, e))


def _rot_half(v):
    lane = lax.broadcasted_iota(I32, v.shape, 1)
    return jnp.where((lane % HD) < HD // 2, pltpu.roll(v, LANES - HD // 2, 1), pltpu.roll(v, HD // 2, 1))


def _norm_rope(z, c0, width, gain, pn, cs, sn, out_scale):
    outs = []
    for g in range(width // LANES):
        v = z[:, c0 + g * LANES:c0 + (g + 1) * LANES]
        if gain is not None:
            ms = _dot((v * v).astype(BF16), pn)
            v = v * lax.rsqrt(ms + EPS) * gain
        v = v * cs + _rot_half(v) * sn
        if out_scale != 1.0:
            v = v * out_scale
        outs.append(v)
    return outs


def _proj_kernel(x_ref, sh_ref, sc_ref, g_ref, w_ref, cs_ref, sn_ref, gains_ref, pn_ref,
                 qa_ref, qi_ref, qb_ref, ka_ref, kb_ref, ki_ref, wi_ref, va_ref, vb_ref,
                 kab_ref, kbb_ref, kib_ref, vab_ref, vbb_ref, *, token_minor):
    tm = x_ref.shape[0]

    def put(ref, r0, v):
        if token_minor:
            ref[r0:r0 + v.shape[1], :] = v.T
        else:
            ref[:, r0:r0 + v.shape[1]] = v

    x = x_ref[...]
    ms = jnp.mean(x * x, axis=-1, keepdims=True)
    xn = x * lax.rsqrt(ms + EPS) * g_ref[...]
    xn = xn * (1.0 + sc_ref[...]) + sh_ref[...]
    z = _dot(xn.astype(BF16), w_ref[...])
    cs = cs_ref[...]
    sn = sn_ref[...]
    pn = pn_ref[...]
    gains = gains_ref[...]
    g_qa, g_ka, g_ki, g_qb, g_kb = (gains[r:r + 1, :] for r in range(5))

    qa = _norm_rope(z, C_QA, 512, g_qa, pn, cs, sn, Q_SCALE)
    qi = _norm_rope(z, C_QI, 512, None, pn, cs, sn, 1.0)
    qb = _norm_rope(z, C_QB, 512, g_qb, pn, cs, sn, Q_SCALE)
    for g in range(4):
        sl = slice(g * LANES, (g + 1) * LANES)
        qa_ref[:, sl] = qa[g].astype(BF16)
        qi_ref[:, sl] = qi[g].astype(BF16)
        qb_ref[:, sl] = qb[g].astype(BF16)
    ka = _norm_rope(z, C_KA, 128, g_ka, pn, cs, sn, 1.0)[0]
    put(ka_ref, 0, ka)
    kab_ref[...] = ka.astype(BF16)
    kb = _norm_rope(z, C_KB, 256, g_kb, pn, cs, sn, 1.0)
    for g in range(2):
        put(kb_ref, g * LANES, kb[g])
        kbb_ref[:, g * LANES:(g + 1) * LANES] = kb[g].astype(BF16)
    ki = _norm_rope(z, C_KI, 128, g_ki, pn, cs, sn, 1.0)[0]
    if token_minor:
        ki_ref[...] = ki.T[:D_IDX, :]
    else:
        ki_ref[...] = ki[:, :D_IDX]
    kib_ref[...] = ki.astype(BF16)
    wi_ref[...] = z[:, C_WI:C_WI + LANES] * (H_IDX ** -0.5 * D_IDX ** -0.5)
    va = z[:, C_VA:C_VA + 128]
    put(va_ref, 0, va)
    vab_ref[...] = va.astype(BF16)
    vb = z[:, C_VB:C_VB + 256]
    if token_minor:
        for j in range(KVB):
            vb_ref[pl.ds(j, tm, stride=KVB), :] = vb[:, j * VB:(j + 1) * VB]
    else:
        vb_ref[...] = vb
    vbb_ref[...] = vb.astype(BF16)


def _permute_w_in(w_in):
    d = w_in.shape[0]
    seg = lambda a, b: w_in[:, a:b]
    cols = [seg(0, 512), seg(768, 1280), seg(1352, 1864), seg(512, 640), seg(1864, 2120),
            seg(1280, 1344), seg(1280, 1344), seg(1344, 1352), jnp.zeros((d, LANES - H_IDX), w_in.dtype),
            seg(640, 768), seg(2120, 2376)]
    return jnp.concatenate(cols, axis=1).astype(BF16)


def _rope_tables(pos):
    half = HD // 2
    inv = ROPE_THETA ** (-jnp.arange(half, dtype=F32) / half)
    ang = pos.astype(F32)[:, None] * inv[None, :]
    cos, sin = jnp.cos(ang), jnp.sin(ang)
    cs = jnp.concatenate([cos, cos, cos, cos], axis=1)
    sn = jnp.concatenate([-sin, sin, -sin, sin], axis=1)
    return cs, sn


def _head_mean_matrix():
    r = lax.broadcasted_iota(I32, (LANES, LANES), 0) // HD
    c = lax.broadcasted_iota(I32, (LANES, LANES), 1) // HD
    return jnp.where(r == c, 1.0 / HD, 0.0).astype(BF16)


def _mod_spec(arr, tm, steps_per_seq):
    if arr.ndim == 3:
        return pl.BlockSpec((None, 1, arr.shape[-1]), lambda i: (i // steps_per_seq, 0, 0))
    return pl.BlockSpec((tm, arr.shape[-1]), lambda i: (i, 0))


def _proj(x2, sh, sc, g_attn, w_perm, cs, sn, gains, tm, steps_per_seq, token_minor):
    n, d = x2.shape
    if cs.shape[0] == n:
        tab_spec = pl.BlockSpec((tm, LANES), lambda i: (i, 0))
    else:
        tab_spec = pl.BlockSpec((tm, LANES), lambda i: (i % steps_per_seq, 0))
    const = lambda shape: pl.BlockSpec(shape, lambda i: (0,) * len(shape))
    widths = [(512, BF16), (512, BF16), (512, BF16), (128, F32), (256, F32), (D_IDX, F32), (128, F32),
              (128, F32), (256, F32), (128, BF16), (256, BF16), (128, BF16), (128, BF16), (256, BF16)]
    shapes = [jax.ShapeDtypeStruct((n, w), dt) for w, dt in widths]
    specs = [pl.BlockSpec((tm, w), lambda i: (i, 0)) for w, _ in widths]
    if token_minor:
        nseq, t = n // (tm * steps_per_seq), tm * steps_per_seq
        for o in (3, 4, 5, 7):
            w = widths[o][0]
            shapes[o] = jax.ShapeDtypeStruct((nseq, w, t), F32)
            specs[o] = pl.BlockSpec((None, w, tm), lambda i: (i // steps_per_seq, 0, i % steps_per_seq))
        shapes[8] = jax.ShapeDtypeStruct((n * KVB, VB), F32)
        specs[8] = pl.BlockSpec((tm * KVB, VB), lambda i: (i, 0))
    return pl.pallas_call(
        functools.partial(_proj_kernel, token_minor=token_minor),
        out_shape=shapes,
        grid=(n // tm,),
        in_specs=[pl.BlockSpec((tm, d), lambda i: (i, 0)),
                  _mod_spec(sh, tm, steps_per_seq), _mod_spec(sc, tm, steps_per_seq),
                  const((1, d)), const((d, C_END)), tab_spec, tab_spec,
                  const((8, LANES)), const((LANES, LANES))],
        out_specs=specs,
        compiler_params=_cparams("arbitrary"),
        name="proj",
    )(x2, sh, sc, g_attn.reshape(1, d), w_perm, cs, sn, gains, _head_mean_matrix())


def _sortable_key(x):
    bits = lax.bitcast_convert_type(x, I32)
    return bits ^ ((bits >> 31) & 0x7FFFFFFF)


COUNT_ROWS = 128


def _count_keys(pred, skey_ref, nkc, rows, ck, group):
    rb = min(rows, COUNT_ROWS)
    ngroups = (nkc + group - 1) // group
    parts = []
    for r0 in range(0, rows, rb):
        def body(cg, acc, r0=r0):
            for u in range(group):
                c = cg * group + u
                for g in range(ck // LANES):
                    k = skey_ref[c, r0:r0 + rb, g * LANES:(g + 1) * LANES]
                    acc = acc + jnp.where(pred(k, c, g * LANES, r0), 1.0, 0.0)
            return acc
        parts.append(lax.fori_loop(0, ngroups, body, jnp.zeros((rb, LANES), F32)))
    acc = parts[0] if len(parts) == 1 else jnp.concatenate(parts, axis=0)
    return jnp.broadcast_to(jnp.sum(acc, axis=-1, keepdims=True), (rows, LANES))


def _select_topk(skey_ref, nkc, ksel, rows, ck, group=1):
    kf = float(ksel)
    rb = min(rows, COUNT_ROWS)
    strip = lambda a, r0: a[r0:r0 + rb]
    count = functools.partial(_count_keys, skey_ref=skey_ref, nkc=nkc, rows=rows, ck=ck, group=group)

    def bit_body(i, t):
        cand = t + lax.shift_left(jnp.int32(1), 31 - i)
        cnt = count(lambda k, c, c0, r0: k >= strip(cand, r0))
        return jnp.where(cnt >= kf, cand, t)

    thr = lax.fori_loop(0, 32, bit_body, jnp.full((rows, LANES), INT_MIN, I32))
    n_ge = count(lambda k, c, c0, r0: k >= strip(thr, r0))
    n_gt = count(lambda k, c, c0, r0: k > strip(thr, r0))
    excess = jnp.logical_and(n_ge > kf, thr != INT_MIN)
    need = kf - n_gt

    @pl.when(jnp.max(jnp.where(excess, 1.0, 0.0)) > 0.0)
    def _():
        lane = lax.broadcasted_iota(I32, (rb, LANES), 1)
        nbits = max(1, (skey_ref.shape[0] * ck).bit_length())

        def idx_body(i, p):
            cand = p | lax.shift_left(jnp.int32(1), nbits - 1 - i)
            tied_before = lambda k, c, c0, r0: jnp.logical_and(
                k == strip(thr, r0), c * ck + c0 + lane < strip(cand, r0))
            cnt = count(tied_before)
            return jnp.where(cnt <= need, cand, p)

        p_keep = lax.fori_loop(0, nbits, idx_body, jnp.zeros((rows, LANES), I32))

        def drop_body(c, _):
            for r0 in range(0, rows, rb):
                for g in range(ck // LANES):
                    k = skey_ref[c, r0:r0 + rb, g * LANES:(g + 1) * LANES]
                    drop = jnp.logical_and(k == strip(thr, r0), c * ck + g * LANES + lane >= strip(p_keep, r0))
                    drop = jnp.logical_and(drop, strip(excess, r0))
                    skey_ref[c, r0:r0 + rb, g * LANES:(g + 1) * LANES] = jnp.where(drop, INT_MIN, k)
            return 0

        lax.fori_loop(0, nkc, drop_body, 0)

    return jnp.maximum(thr, INT_MIN + 1)


def _flash_init(m_ref, acc_ref):
    m_ref[...] = jnp.full(m_ref.shape, M_INIT, F32)
    acc_ref[...] = jnp.zeros(acc_ref.shape, F32)


def _flash_update(s, pv_fn, m_ref, acc_ref):
    m_prev = m_ref[...]
    m_new = jnp.maximum(m_prev, jnp.max(s, axis=-1, keepdims=True))
    alpha = jnp.exp2(m_prev - m_new)
    rep = lambda a, n: a if n == 1 else jnp.concatenate([a] * n, axis=1)
    p = jnp.exp2(s - rep(m_new, s.shape[1] // LANES)).astype(BF16)
    acc_ref[...] = rep(alpha, acc_ref.shape[1] // LANES) * acc_ref[...] + pv_fn(p)
    m_ref[...] = m_new


def _pv_pages(vaug_list):
    def fn(p):
        upd = None
        for i, va in enumerate(vaug_list):
            nk = va.shape[0]
            d = _dot(p[:, i * nk:(i + 1) * nk], va)
            upd = d if upd is None else upd + d
        return upd
    return fn


def _with_ones(v):
    return jnp.concatenate([v, jnp.ones((v.shape[0], LANES), v.dtype)], axis=1)


def _dsa_kernel(qi_ref, wi_ref, ki_ref, qa_ref, ka_ref, va_ref, o_ref, skey_ref, m_ref, acc_ref,
                *, tq, ck, ksel):
    i = pl.program_id(1)
    nkc = (i + 1) * (tq // ck)
    lane = lax.broadcasted_iota(I32, (tq, LANES), 1)
    hi_half = lane >= HD
    qi = qi_ref[...]
    w = wi_ref[...]
    zero_b = jnp.zeros((tq, LANES), BF16)
    q_heads = []
    for h in range(H_IDX):
        grp = qi[:, (h // 2) * LANES:(h // 2 + 1) * LANES]
        q_heads.append(jnp.where(hi_half if h % 2 else jnp.logical_not(hi_half), grp, zero_b))
    w_heads = [w[:, h:h + 1] for h in range(H_IDX)]
    rows = i * tq + lax.broadcasted_iota(I32, (tq, ck), 0)
    col_in = lax.broadcasted_iota(I32, (tq, ck), 1)

    def score_body(c, _):
        kc = ki_ref[pl.ds(pl.multiple_of(c * ck, ck), ck), :]
        acc = jnp.zeros((tq, ck), F32)
        for h in range(H_IDX):
            acc = acc + jnp.maximum(_dot_nt(q_heads[h], kc), 0.0) * w_heads[h]
        key = _sortable_key(acc)
        skey_ref[c] = jnp.where(c * ck + col_in <= rows, key, INT_MIN)
        return 0

    lax.fori_loop(0, nkc, score_body, 0)
    group = 2 if skey_ref.shape[0] % 2 == 0 else 1

    @pl.when(nkc % group != 0)
    def _():
        skey_ref[nkc] = jnp.full((tq, ck), INT_MIN, I32)

    thr = _select_topk(skey_ref, nkc, ksel, tq, ck, group)
    thr = jnp.concatenate([thr] * (ck // LANES), axis=1)

    qa = qa_ref[...].astype(F32)

    def head_at_half(h, half):
        grp = qa[:, (h // 2) * LANES:(h // 2 + 1) * LANES]
        if h % 2 != half:
            grp = pltpu.roll(grp, HD, 1)
        return jnp.where(hi_half if half else jnp.logical_not(hi_half), grp, 0.0)

    g_per = HA // KVA
    outs = []
    for j in range(KVA):
        q_rows = jnp.concatenate([head_at_half(g_per * j + g, j) for g in range(g_per)], axis=0).astype(BF16)
        _flash_init(m_ref, acc_ref)
        own_half = (lax.broadcasted_iota(I32, (ck, LANES), 1) >= HD) == bool(j)
        one_b = jnp.ones((ck, LANES), BF16)

        def att_body(c, _):
            off = pl.multiple_of(c * ck, ck)
            s = _dot_nt(q_rows, ka_ref[pl.ds(off, ck), :])
            bias = jnp.where(skey_ref[c] >= thr, 0.0, NEG_BIAS)
            s = (s.reshape(g_per, tq, ck) + bias[None]).reshape(g_per * tq, ck)
            va = jnp.where(own_half, va_ref[pl.ds(off, ck), :], one_b)
            _flash_update(s, _pv_pages([va]), m_ref, acc_ref)
            return 0

        lax.fori_loop(0, nkc, att_body, 0)
        acc = acc_ref[...]
        outs.append(acc / pltpu.roll(acc, HD, 1))

    for gg in range(HA // 2):
        j = (2 * gg) // g_per
        a = outs[j][((2 * gg) % g_per) * tq:((2 * gg) % g_per + 1) * tq]
        b = outs[j][((2 * gg + 1) % g_per) * tq:((2 * gg + 1) % g_per + 1) * tq]
        if j == 1:
            a = pltpu.roll(a, HD, 1)
        else:
            b = pltpu.roll(b, HD, 1)
        o_ref[:, gg * LANES:(gg + 1) * LANES] = jnp.where(hi_half, b, a).astype(o_ref.dtype)


def _dsa_prompt(qi, wi, kib, qa, kab, vab, ksel):
    b, t, _ = qi.shape
    tq, ck = (DSA_Q_BLOCK if t % DSA_Q_BLOCK == 0 else Q_BLOCK), K_CHUNK
    qspec = lambda w: pl.BlockSpec((None, tq, w), lambda bb, i: (bb, i, 0))
    kspec = lambda w: pl.BlockSpec((None, t, w), lambda bb, i: (bb, 0, 0))
    g_per = HA // KVA
    return pl.pallas_call(
        functools.partial(_dsa_kernel, tq=tq, ck=ck, ksel=ksel),
        out_shape=jax.ShapeDtypeStruct((b, t, 512), BF16),
        grid=(b, t // tq),
        in_specs=[qspec(512), qspec(128), kspec(128), qspec(512), kspec(128), kspec(128)],
        out_specs=qspec(512),
        scratch_shapes=[pltpu.VMEM((t // ck, tq, ck), I32),
                        pltpu.VMEM((g_per * tq, LANES), F32),
                        pltpu.VMEM((g_per * tq, LANES), F32)],
        compiler_params=_cparams("arbitrary", "arbitrary"),
        name="dsa_prompt",
    )(qi, wi, kib, qa, kab, vab)


def _lambda_value(lamv, lam_init):
    l1 = jnp.sum(lamv[0:1, :] * lamv[1:2, :], axis=-1, keepdims=True)
    l2 = jnp.sum(lamv[2:3, :] * lamv[3:4, :], axis=-1, keepdims=True)
    return jnp.exp(l1) - jnp.exp(l2) + lam_init


def _sub_norm(o, gsub, lam_init):
    ms = jnp.mean(o * o, axis=-1, keepdims=True)
    return o * lax.rsqrt(ms + EPS) * gsub * (1.0 - lam_init)


def _diff_kernel(qb_ref, kb_ref, vb_ref, lamv_ref, gsub_ref, o_ref, m_ref, acc_ref, *, tq, ck, lam_init):
    i = pl.program_id(1)
    n_full = i * (tq // ck)
    lane = lax.broadcasted_iota(I32, (tq, LANES), 1)
    hi_half = lane >= HD
    qb = qb_ref[...]
    zero_b = jnp.zeros((tq, LANES), BF16)
    lam = _lambda_value(lamv_ref[...], lam_init)
    gsub = gsub_ref[...]
    g_per = HB // KVB
    r_in = lax.broadcasted_iota(I32, (tq, ck), 0)
    c_in = lax.broadcasted_iota(I32, (tq, ck), 1)

    for j in range(KVB):
        parts = []
        for g in range(g_per):
            grp = qb[:, (g_per * j + g) * LANES:(g_per * j + g + 1) * LANES]
            parts.append(jnp.where(jnp.logical_not(hi_half), grp, zero_b))
            parts.append(jnp.where(hi_half, grp, zero_b))
        q_rows = jnp.concatenate(parts, axis=0)
        kcols = slice(j * LANES, (j + 1) * LANES)
        _flash_init(m_ref, acc_ref)

        def step(c, bias):
            off = pl.multiple_of(c * ck, ck)
            s = _dot_nt(q_rows, kb_ref[pl.ds(off, ck), kcols])
            if bias is not None:
                s = (s.reshape(2 * g_per, tq, ck) + bias[None]).reshape(2 * g_per * tq, ck)
            _flash_update(s, _pv_pages([_with_ones(vb_ref[pl.ds(off, ck), kcols])]), m_ref, acc_ref)

        def full_body(c, _):
            step(c, None)
            return 0

        lax.fori_loop(0, n_full, full_body, 0)
        for d in range(tq // ck):
            bias = jnp.where(d * ck + c_in <= r_in, 0.0, NEG_BIAS)
            step(n_full + d, bias)

        acc = acc_ref[...]
        o = acc[:, :LANES] / acc[:, LANES:]
        for g in range(g_per):
            o1 = o[(2 * g) * tq:(2 * g + 1) * tq]
            o2 = o[(2 * g + 1) * tq:(2 * g + 2) * tq]
            hh = g_per * j + g
            o_ref[:, hh * LANES:(hh + 1) * LANES] = _sub_norm(o1 - lam * o2, gsub, lam_init).astype(o_ref.dtype)


def _diff_prompt(qb, kbb, vbb, lamv, gsub, lam_init):
    b, t, _ = qb.shape
    tq, ck = (DIFF_Q_BLOCK if t % DIFF_Q_BLOCK == 0 else Q_BLOCK), K_CHUNK
    qspec = lambda w: pl.BlockSpec((None, tq, w), lambda bb, i: (bb, i, 0))
    kspec = lambda w: pl.BlockSpec((None, t, w), lambda bb, i: (bb, 0, 0))
    const = lambda shape: pl.BlockSpec(shape, lambda bb, i: (0,) * len(shape))
    rows = 2 * (HB // KVB) * tq
    return pl.pallas_call(
        functools.partial(_diff_kernel, tq=tq, ck=ck, lam_init=lam_init),
        out_shape=jax.ShapeDtypeStruct((b, t, 512), BF16),
        grid=(b, t // tq),
        in_specs=[qspec(512), kspec(256), kspec(256), const((8, LANES)), const((1, LANES))],
        out_specs=qspec(512),
        scratch_shapes=[pltpu.VMEM((rows, LANES), F32), pltpu.VMEM((rows, 2 * LANES), F32)],
        compiler_params=_cparams("arbitrary", "arbitrary"),
        name="diff_prompt",
    )(qb, kbb, vbb, lamv, gsub)


def _head_sum(r):
    r3 = r.reshape(H_IDX, 8, r.shape[1])
    s = r3[0]
    for h in range(1, H_IDX):
        s = s + r3[h]
    return s + 0.0


def _s1_kernel(pt_ref, qm_ref, wcol_ref, knew_ref, *rest, pg, tnew):
    pages = rest[:pg]
    sc_ref, scn_ref = rest[pg], rest[pg + 1]
    qm = qm_ref[...]
    wcol = wcol_ref[...]
    for i in range(pg):
        s = _dot(qm, pages[i][...].astype(BF16))
        sc_ref[:, i * LANES:(i + 1) * LANES] = _head_sum(jnp.maximum(s, 0.0) * wcol)

    @pl.when(pl.program_id(1) == 0)
    def _():
        s = _dot_nt(qm, knew_ref[...])
        sn = _head_sum(jnp.maximum(s, 0.0) * wcol)
        r = lax.broadcasted_iota(I32, sn.shape, 0)
        c = lax.broadcasted_iota(I32, sn.shape, 1)
        scn_ref[...] = jnp.where(jnp.logical_and(c <= r, c < tnew), sn, -jnp.inf)


def _page_specs(rows, pg):
    return [pl.BlockSpec((None, rows, LANES), functools.partial(lambda b, g, pt, i: (pt[b, g * pg + i], 0, 0), i=i))
            for i in range(pg)]


def _sample_scores(page_table, qm, wcol, knew_i, cik, pg, tnew):
    db, n_pages = page_table.shape
    grid_spec = pltpu.PrefetchScalarGridSpec(
        num_scalar_prefetch=1,
        grid=(db, n_pages // pg),
        in_specs=[pl.BlockSpec((None, 64, D_IDX), lambda b, g, pt: (b, 0, 0)),
                  pl.BlockSpec((None, 64, LANES), lambda b, g, pt: (b, 0, 0)),
                  pl.BlockSpec((None, LANES, D_IDX), lambda b, g, pt: (b, 0, 0))]
                 + _page_specs(D_IDX, pg),
        out_specs=[pl.BlockSpec((None, 8, pg * LANES), lambda b, g, pt: (b, 0, g)),
                   pl.BlockSpec((None, 8, LANES), lambda b, g, pt: (b, 0, 0))],
    )
    return pl.pallas_call(
        functools.partial(_s1_kernel, pg=pg, tnew=tnew),
        out_shape=[jax.ShapeDtypeStruct((db, 8, n_pages * LANES), F32),
                   jax.ShapeDtypeStruct((db, 8, LANES), F32)],
        grid_spec=grid_spec,
        compiler_params=_cparams("arbitrary", "arbitrary"),
        name="sample_scores",
    )(page_table, qm, wcol, knew_i, *([cik] * pg))


def _s1b_kernel(sc_ref, o_ref, skey_ref, *, ksel):
    nc, rows, ck = skey_ref.shape

    def conv(c, _):
        x = sc_ref[c]
        skey_ref[c] = jnp.where(x == -jnp.inf, INT_MIN, _sortable_key(x))
        return 0

    lax.fori_loop(0, nc, conv, 0)
    thr = _select_topk(skey_ref, nc, ksel, rows, ck)

    def emit(c, _):
        o_ref[c] = jnp.where(skey_ref[c] >= thr, 0.0, NEG_BIAS)
        return 0

    lax.fori_loop(0, nc, emit, 0)


def _sample_select(sct, ksel):
    nc, r, _ = sct.shape
    rb = min(r, 128)
    spec = pl.BlockSpec((nc, rb, LANES), lambda i: (0, i, 0))
    return pl.pallas_call(
        functools.partial(_s1b_kernel, ksel=ksel),
        out_shape=jax.ShapeDtypeStruct(sct.shape, F32),
        grid=(r // rb,),
        in_specs=[spec],
        out_specs=spec,
        scratch_shapes=[pltpu.VMEM((nc, rb, LANES), I32)],
        compiler_params=_cparams("arbitrary"),
        name="sample_select",
    )(sct)


def _s2_kernel(pt_ref, qa_ref, qb_ref, selp_ref, seln_ref, kan_ref, van_ref, kbn_ref, vbn_ref,
               lamv_ref, gsub_ref, ak_hbm, av_hbm, bk_hbm, bv_hbm, oa_ref, ob_ref,
               ma_ref, acca_ref, mb_ref, accb_ref, ak_buf, av_buf, bk_buf, bv_buf, sems, *, pg, tnew, lam_init):
    g = pl.program_id(1)
    n_groups = pl.num_programs(1)
    step = pl.program_id(0) * n_groups + g
    slot = step % 2

    def page_copies(s, sl):
        b_of, g_of = s // n_groups, s % n_groups
        out = []
        for i in range(pg):
            page = pt_ref[b_of, g_of * pg + i]
            for hbm, buf in ((ak_hbm, ak_buf), (av_hbm, av_buf), (bk_hbm, bk_buf), (bv_hbm, bv_buf)):
                out.append(pltpu.make_async_copy(hbm.at[page], buf.at[sl, i], sems.at[sl]))
        return out

    @pl.when(step == 0)
    def _():
        for cp in page_copies(step, slot):
            cp.start()

    @pl.when(step + 1 < pl.num_programs(0) * n_groups)
    def _():
        for cp in page_copies(step + 1, 1 - slot):
            cp.start()

    for cp in page_copies(step, slot):
        cp.wait()

    akp = [ak_buf.at[slot, i] for i in range(pg)]
    avp = [av_buf.at[slot, i] for i in range(pg)]
    bkp = [bk_buf.at[slot, i] for i in range(pg)]
    bvp = [bv_buf.at[slot, i] for i in range(pg)]
    qa = qa_ref[...]
    qb = qb_ref[...]

    rows_j = qb.shape[0] // KVB
    ones_kd = jnp.ones((LANES, LANES), BF16)

    def pv_b(values_of_kv):
        def fn(p):
            parts = []
            for j in range(KVB):
                upd = None
                for i in range(p.shape[1] // LANES):
                    va = jnp.concatenate([values_of_kv(i, j), ones_kd], axis=1)
                    d = _dot(p[j * rows_j:(j + 1) * rows_j, i * LANES:(i + 1) * LANES], va)
                    upd = d if upd is None else upd + d
                parts.append(upd)
            return jnp.concatenate(parts, axis=0)
        return fn

    @pl.when(g == 0)
    def _():
        _flash_init(ma_ref, acca_ref)
        _flash_init(mb_ref, accb_ref)
        sa = _dot_nt(qa, kan_ref[...]) + jnp.tile(seln_ref[...], (HA, 1))
        _flash_update(sa, _pv_pages([_with_ones(van_ref[...])]), ma_ref, acca_ref)
        r = lax.broadcasted_iota(I32, (8, LANES), 0)
        c = lax.broadcasted_iota(I32, (8, LANES), 1)
        causal = jnp.where(jnp.logical_and(c <= r, c < tnew), 0.0, NEG_BIAS)
        sb = _dot_nt(qb, kbn_ref[...]) + jnp.tile(causal, (8, 1))
        _flash_update(sb, pv_b(lambda i, j: vbn_ref[:, j * LANES:(j + 1) * LANES]), mb_ref, accb_ref)

    sa = jnp.concatenate([_dot(qa, akp[i][...].astype(BF16)) for i in range(pg)], axis=1)
    sa = sa + jnp.tile(selp_ref[...], (HA, 1))

    def pv_a(p):
        upd = None
        for i in range(pg):
            vt = jnp.concatenate([avp[i][...].astype(BF16), ones_kd], axis=0)
            d = _dot_nt(p[:, i * LANES:(i + 1) * LANES], vt)
            upd = d if upd is None else upd + d
        return upd

    _flash_update(sa, pv_a, ma_ref, acca_ref)
    sb = jnp.concatenate([_dot(qb, bkp[i][...].astype(BF16)) for i in range(pg)], axis=1)
    _flash_update(sb, pv_b(lambda i, j: bvp[i][pl.ds(j, LANES, stride=KVB), :].astype(BF16)), mb_ref, accb_ref)

    @pl.when(g == pl.num_programs(1) - 1)
    def _():
        acca = acca_ref[...]
        oa_ref[...] = acca[:, :LANES] / acca[:, LANES:]
        accb = accb_ref[...]
        x = (accb[:, :LANES] / accb[:, LANES:]).reshape(HB, 2, 8, LANES)
        lam = _lambda_value(lamv_ref[...], lam_init)
        gsub = gsub_ref[...]
        for hh in range(HB):
            ob_ref[:, hh * LANES:(hh + 1) * LANES] = _sub_norm(x[hh, 0] - lam * x[hh, 1], gsub, lam_init)


def _sample_attend(page_table, qa_rows, qb_rows, selp, seln, kan, van, kbn, vbn, lamv, gsub,
                   cak, cav, cbk, cbv, pg, tnew, lam_init):
    db, n_pages = page_table.shape
    per_b = lambda *shape: pl.BlockSpec((None,) + shape, lambda b, g, pt: (b,) + (0,) * len(shape))
    const = lambda shape: pl.BlockSpec(shape, lambda b, g, pt: (0,) * len(shape))
    in_hbm = pl.BlockSpec(memory_space=pl.ANY)
    page_buf = lambda arr: pltpu.VMEM((2, pg) + arr.shape[1:], arr.dtype)
    grid_spec = pltpu.PrefetchScalarGridSpec(
        num_scalar_prefetch=1,
        grid=(db, n_pages // pg),
        in_specs=[per_b(64, 128), per_b(64, 256),
                  pl.BlockSpec((None, 8, pg * LANES), lambda b, g, pt: (b, 0, g)), per_b(8, LANES),
                  per_b(LANES, 128), per_b(LANES, 128), per_b(LANES, 256), per_b(LANES, 256),
                  const((8, LANES)), const((1, LANES)), in_hbm, in_hbm, in_hbm, in_hbm],
        out_specs=[per_b(64, 128), per_b(8, 512)],
        scratch_shapes=[pltpu.VMEM((64, LANES), F32), pltpu.VMEM((64, 2 * LANES), F32),
                        pltpu.VMEM((64, LANES), F32), pltpu.VMEM((64, 2 * LANES), F32),
                        page_buf(cak), page_buf(cav), page_buf(cbk), page_buf(cbv),
                        pltpu.SemaphoreType.DMA((2,))],
    )
    return pl.pallas_call(
        functools.partial(_s2_kernel, pg=pg, tnew=tnew, lam_init=lam_init),
        out_shape=[jax.ShapeDtypeStruct((db, 64, 128), F32), jax.ShapeDtypeStruct((db, 8, 512), F32)],
        grid_spec=grid_spec,
        compiler_params=_cparams("arbitrary", "arbitrary"),
        name="sample_attend",
    )(page_table, qa_rows, qb_rows, selp, seln, kan, van, kbn, vbn, lamv, gsub, cak, cav, cbk, cbv)


TOKEN_ROWS = 8


def _outproj_kernel(oa_ref, ob_ref, x_ref, gt_ref, sh_ref, sc_ref, g_ref, wo_ref, wr_ref, br_ref,
                    h_ref, xm_ref, comb_ref, sel_ref):
    half = oa_ref.shape[1]
    tm = x_ref.shape[0]
    o = _dot(oa_ref[...].astype(BF16), wo_ref[:half, :]) + _dot(ob_ref[...].astype(BF16), wo_ref[half:, :])
    h = x_ref[...] + gt_ref[...] * o
    h_ref[...] = h
    ms = jnp.mean(h * h, axis=-1, keepdims=True)
    xm = h * lax.rsqrt(ms + EPS) * g_ref[...]
    xm = xm * (1.0 + sc_ref[...]) + sh_ref[...]
    for c in range(TOKEN_ROWS):
        xm_ref[pl.ds(c, tm, stride=TOKEN_ROWS), :] = xm[:, c * LANES:(c + 1) * LANES]
    logits = _dot3(xm, wr_ref[...]) + br_ref[...]
    lane = lax.broadcasted_iota(I32, logits.shape, 1).astype(F32)
    vals, hots = [], []
    for _ in range(TOP_K):
        m = jnp.max(logits, axis=-1, keepdims=True)
        idx = jnp.min(jnp.where(logits == m, lane, float(LANES)), axis=-1, keepdims=True)
        hot = lane == idx
        vals.append(m)
        hots.append(hot)
        logits = jnp.where(hot, -3e38, logits)
    es = [jnp.exp(v - vals[0]) for v in vals]
    den = es[0] + es[1] + es[2] + es[3]
    comb = jnp.zeros(logits.shape, F32)
    sel = jnp.zeros(logits.shape, F32)
    for e, hot in zip(es, hots):
        comb = comb + jnp.where(hot, e / den, 0.0)
        sel = sel + jnp.where(hot, 1.0, 0.0)
    comb_ref[...] = comb
    sel_ref[...] = sel.astype(BF16)


def _outproj(oa, ob, x2, gt, sh, sc, g_ffn, w_out_b, w_router_p, b_router_p, tm, steps_per_seq):
    n, d = x2.shape
    assert d == TOKEN_ROWS * LANES
    const = lambda shape: pl.BlockSpec(shape, lambda i: (0,) * len(shape))
    row = lambda w: pl.BlockSpec((tm, w), lambda i: (i, 0))
    return pl.pallas_call(
        _outproj_kernel,
        out_shape=[jax.ShapeDtypeStruct((n, d), F32), jax.ShapeDtypeStruct((n * TOKEN_ROWS, LANES), F32),
                   jax.ShapeDtypeStruct((n, LANES), F32), jax.ShapeDtypeStruct((n, LANES), BF16)],
        grid=(n // tm,),
        in_specs=[row(oa.shape[1]), row(ob.shape[1]), row(d),
                  _mod_spec(gt, tm, steps_per_seq), _mod_spec(sh, tm, steps_per_seq),
                  _mod_spec(sc, tm, steps_per_seq),
                  const((1, d)), const(w_out_b.shape), const((d, LANES)), const((1, LANES))],
        out_specs=[row(d), pl.BlockSpec((tm * TOKEN_ROWS, LANES), lambda i: (i, 0)), row(LANES), row(LANES)],
        compiler_params=_cparams("arbitrary"),
        name="outproj",
    )(oa, ob, x2, gt, sh, sc, g_ffn.reshape(1, d), w_out_b, w_router_p, b_router_p)


MOE_TILE = 256
TAB_ROWS = 8


def _rank_kernel(sel_ref, tri_ref, before_ref, base_ref, cnt_ref, total_ref, carry_ref):
    @pl.when(pl.program_id(0) == 0)
    def _():
        carry_ref[...] = jnp.zeros(carry_ref.shape, F32)

    sel = sel_ref[...]
    tm = sel.shape[0]
    before = _dot(tri_ref[...], sel)
    before_ref[...] = before
    carry = carry_ref[...]
    base_ref[...] = carry
    cnt = jnp.broadcast_to(before[tm - 1:tm, :] + sel[tm - 1:tm, :].astype(F32), carry.shape)
    cnt_ref[...] = cnt
    carry = carry + cnt
    carry_ref[...] = carry
    total_ref[...] = carry


def _moe_rank(sel, tm):
    n = sel.shape[0]
    r = lax.broadcasted_iota(I32, (tm, tm), 0)
    c = lax.broadcasted_iota(I32, (tm, tm), 1)
    tri = jnp.where(c < r, 1.0, 0.0).astype(BF16)
    per_tile = pl.BlockSpec((8, LANES), lambda i: (i, 0))
    return pl.pallas_call(
        _rank_kernel,
        out_shape=[jax.ShapeDtypeStruct((n, LANES), F32), jax.ShapeDtypeStruct((n // tm * 8, LANES), F32),
                   jax.ShapeDtypeStruct((n // tm * 8, LANES), F32), jax.ShapeDtypeStruct((8, LANES), F32)],
        grid=(n // tm,),
        in_specs=[pl.BlockSpec((tm, LANES), lambda i: (i, 0)), pl.BlockSpec((tm, tm), lambda i: (0, 0))],
        out_specs=[pl.BlockSpec((tm, LANES), lambda i: (i, 0)), per_tile, per_tile,
                   pl.BlockSpec((8, LANES), lambda i: (0, 0))],
        scratch_shapes=[pltpu.VMEM((8, LANES), F32)],
        compiler_params=_cparams("arbitrary"),
        name="moe_rank",
    )(sel, tri)


def _slots_kernel(before_ref, sel_ref, comb_ref, base_ref, cnt_ref, starts_ref, upper_ref,
                  pos_ref, gate_ref, tab_ref):
    cnt = cnt_ref[...]
    offs = _dot(cnt.astype(BF16), upper_ref[...])
    posv = before_ref[...] + offs[0:1, :]
    comb = comb_ref[...]
    sel = sel_ref[...].astype(F32) > 0.0
    lane = lax.broadcasted_iota(I32, posv.shape, 1)
    lanef = lane.astype(F32)
    pos = jnp.zeros(posv.shape, F32)
    gates = jnp.zeros(posv.shape, F32)
    for k in range(TOP_K):
        idx = jnp.min(jnp.where(sel, lanef, float(LANES)), axis=-1, keepdims=True)
        hot = lanef == idx
        pk = jnp.sum(jnp.where(hot, posv, 0.0), axis=-1, keepdims=True)
        gk = jnp.sum(jnp.where(hot, comb, 0.0), axis=-1, keepdims=True)
        pos = jnp.where(lane == k, pk, pos)
        gates = jnp.where(lane == k, gk, gates)
        sel = jnp.logical_and(sel, jnp.logical_not(hot))
    pos_ref[...] = pos.astype(I32)
    gate_ref[...] = gates
    first = starts_ref[...] + base_ref[0:1, :]
    tab = jnp.concatenate([first, cnt[0:1, :], offs[0:1, :], jnp.zeros((TAB_ROWS - 3, LANES), F32)], axis=0)
    tab_ref[...] = tab.astype(I32)


def _moe_slots(before, sel, comb, base, cnt, starts, tm):
    n = before.shape[0]
    row = pl.BlockSpec((tm, LANES), lambda i: (i, 0))
    per_tile = pl.BlockSpec((8, LANES), lambda i: (i, 0))
    r = lax.broadcasted_iota(I32, (LANES, LANES), 0)
    c = lax.broadcasted_iota(I32, (LANES, LANES), 1)
    upper = jnp.where(r < c, 1.0, 0.0).astype(BF16)
    return pl.pallas_call(
        _slots_kernel,
        out_shape=[jax.ShapeDtypeStruct((n, LANES), I32), jax.ShapeDtypeStruct((n, LANES), F32),
                   jax.ShapeDtypeStruct((n // tm * TAB_ROWS, LANES), I32)],
        grid=(n // tm,),
        in_specs=[row, row, row, per_tile, per_tile, pl.BlockSpec((1, LANES), lambda i: (0, 0)),
                  pl.BlockSpec((LANES, LANES), lambda i: (0, 0))],
        out_specs=[row, row, pl.BlockSpec((TAB_ROWS, LANES), lambda i: (i, 0))],
        compiler_params=_cparams("arbitrary"),
        name="moe_slots",
    )(before, sel, comb, base, cnt, starts, upper)


def _slab_rows(ref, index, count=1):
    return ref.at[pl.ds(pl.multiple_of(index * TOKEN_ROWS, TOKEN_ROWS), count * TOKEN_ROWS), :]


def _run_copies(src_ref, dst_ref, src0, dst0, length, max_len, sem, wait, priority=0):
    for b in range(max_len.bit_length() - 1, -1, -1):
        size = 1 << b

        @pl.when((length & size) != 0)
        def _():
            done = (length >> (b + 1)) << (b + 1)
            cp = pltpu.make_async_copy(_slab_rows(src_ref, src0 + done, size),
                                       _slab_rows(dst_ref, dst0 + done, size), sem)
            if wait:
                cp.wait()
            else:
                cp.start(priority=priority)


def _for_each_run(tab_ref, fn):
    def body(e2, carry):
        for parity in range(2):
            e = 2 * e2 + parity
            fn(tab_ref[e], tab_ref[LANES + e], tab_ref[2 * LANES + e], parity)
        return carry
    lax.fori_loop(0, N_EXPERTS // 2, body, 0)


def _dispatch_kernel(pos_ref, tab_ref, x_ref, xs_ref, stage_ref, sem):
    tm = x_ref.shape[0] // TOKEN_ROWS

    def place(t, carry):
        v = _slab_rows(x_ref, t)[...]
        for k in range(TOP_K):
            _slab_rows(stage_ref, pos_ref[t * TOP_K + k])[...] = v
        return carry

    lax.fori_loop(0, tm, place, 0)
    for wait in (False, True):
        _for_each_run(tab_ref, lambda first, length, off, parity, wait=wait: _run_copies(
            stage_ref, xs_ref, off, first, length, tm, sem, wait, parity))


def _smem_words(words):
    return pl.BlockSpec((words,), lambda i: (i,), memory_space=pltpu.SMEM)


def _moe_dispatch(pos_flat, tab_flat, xm_slabs, tm):
    n = xm_slabs.shape[0] // TOKEN_ROWS
    return pl.pallas_call(
        _dispatch_kernel,
        out_shape=jax.ShapeDtypeStruct((n * TOP_K * TOKEN_ROWS, LANES), F32),
        grid=(n // tm,),
        in_specs=[_smem_words(tm * TOP_K), _smem_words(TAB_ROWS * LANES),
                  pl.BlockSpec((tm * TOKEN_ROWS, LANES), lambda i: (i, 0))],
        out_specs=pl.BlockSpec(memory_space=pl.ANY),
        scratch_shapes=[pltpu.VMEM((tm * TOP_K * TOKEN_ROWS, LANES), F32), pltpu.SemaphoreType.DMA],
        compiler_params=_cparams("arbitrary"),
        name="moe_dispatch",
    )(pos_flat, tab_flat, xm_slabs)


def _ffn_kernel(tile_ref, exp_ref, lo_ref, hi_ref, xs_ref, wgu_ref, bgu_ref, wd_ref, bd_ref, ys_ref,
                wgu_b_ref, wd_b_ref):
    g = pl.program_id(0)
    lo, hi = lo_ref[g], hi_ref[g]
    tme = xs_ref.shape[0] // TOKEN_ROWS
    dff = wd_ref.shape[0]

    @pl.when(jnp.logical_or(g == 0, exp_ref[g] != exp_ref[jnp.maximum(g - 1, 0)]))
    def _():
        wgu_b_ref[...] = wgu_ref[...].astype(BF16)
        wd_b_ref[...] = wd_ref[...].astype(BF16)

    @pl.when(hi > lo)
    def _():
        x = jnp.concatenate([xs_ref[pl.ds(c, tme, stride=TOKEN_ROWS), :] for c in range(TOKEN_ROWS)], axis=1)
        hgu = _dot(x.astype(BF16), wgu_b_ref[...]) + bgu_ref[...]
        gate = jnp.minimum(hgu[:, :dff], SWIGLU_LIMIT)
        up = jnp.clip(hgu[:, dff:], -SWIGLU_LIMIT, SWIGLU_LIMIT)
        a = (up + 1.0) * (gate * jax.nn.sigmoid(SWIGLU_ALPHA * gate))
        yo = _dot(a.astype(BF16), wd_b_ref[...]) + bd_ref[...]
        row = lax.broadcasted_iota(I32, (tme, LANES), 0)
        mine = jnp.logical_and(row >= lo, row < hi)

        @pl.when(lo == 0)
        def _():
            for c in range(TOKEN_ROWS):
                ys_ref[pl.ds(c, tme, stride=TOKEN_ROWS), :] = yo[:, c * LANES:(c + 1) * LANES]

        @pl.when(lo != 0)
        def _():
            for c in range(TOKEN_ROWS):
                dst = ys_ref.at[pl.ds(c, tme, stride=TOKEN_ROWS), :]
                dst[...] = jnp.where(mine, yo[:, c * LANES:(c + 1) * LANES], dst[...])


def _moe_plan(counts, n_slots, tme):
    cnt = counts[:N_EXPERTS].astype(I32)
    ends = jnp.cumsum(cnt)
    starts = ends - cnt
    first_tile = starts // tme
    n_items = jnp.where(cnt > 0, (ends - 1) // tme - first_tile + 1, 0)
    item_end = jnp.cumsum(n_items)
    item_start = item_end - n_items
    n_work = n_slots // tme + N_EXPERTS - 1
    g = jnp.arange(n_work, dtype=I32)
    gi = jnp.minimum(g, item_end[-1] - 1)
    e = jnp.sum((item_end[None, :] <= gi[:, None]).astype(I32), axis=1)
    tile = first_tile[e] + (gi - item_start[e])
    lo = jnp.maximum(starts[e], tile * tme) - tile * tme
    hi = jnp.minimum(ends[e], (tile + 1) * tme) - tile * tme
    real = g < item_end[-1]
    return tile, e, jnp.where(real, lo, 0), jnp.where(real, hi, 0), starts


def _moe_ffn(plan, xs, w_gu, b_gu, w_down, b_down, tme):
    tile, e, lo, hi = plan
    ne, d, dff2 = w_gu.shape
    dff = dff2 // 2
    slab_spec = pl.BlockSpec((tme * TOKEN_ROWS, LANES), lambda g, tile, e, lo, hi: (tile[g], 0))
    per_e = lambda *shape: pl.BlockSpec((None,) + shape, lambda g, tile, e, lo, hi: (e[g],) + (0,) * len(shape))
    grid_spec = pltpu.PrefetchScalarGridSpec(
        num_scalar_prefetch=4,
        grid=(tile.shape[0],),
        in_specs=[slab_spec, per_e(d, dff2), per_e(1, dff2), per_e(dff, d), per_e(1, d)],
        out_specs=slab_spec,
        scratch_shapes=[pltpu.VMEM((d, dff2), BF16), pltpu.VMEM((dff, d), BF16)],
    )
    return pl.pallas_call(
        _ffn_kernel,
        out_shape=jax.ShapeDtypeStruct(xs.shape, F32),
        grid_spec=grid_spec,
        compiler_params=_cparams("arbitrary"),
        name="moe_ffn",
    )(tile, e, lo, hi, xs, w_gu, b_gu.reshape(ne, 1, dff2), w_down, b_down.reshape(ne, 1, d))


def _combine_kernel(pos_ref, gate_ref, tab_ref, h_ref, gt_ref, ys_ref, y_ref, stage_ref, acc_ref, sem):
    tm = h_ref.shape[0]
    for wait in (False, True):
        _for_each_run(tab_ref, lambda first, length, off, parity, wait=wait: _run_copies(
            ys_ref, stage_ref, first, off, length, tm, sem, wait, parity))

    def gather(t, carry):
        acc = None
        for k in range(TOP_K):
            part = gate_ref[t * TOP_K + k] * _slab_rows(stage_ref, pos_ref[t * TOP_K + k])[...]
            acc = part if acc is None else acc + part
        _slab_rows(acc_ref, t)[...] = acc
        return carry

    lax.fori_loop(0, tm, gather, 0, unroll=4)
    gt = gt_ref[...]
    for c in range(TOKEN_ROWS):
        sl = slice(c * LANES, (c + 1) * LANES)
        y_ref[:, sl] = h_ref[:, sl] + gt[:, sl] * acc_ref[pl.ds(c, tm, stride=TOKEN_ROWS), :]


def _moe_combine(pos_flat, gate_flat, tab_flat, h, gt, ys, tm, steps_per_seq):
    n, d = h.shape
    return pl.pallas_call(
        _combine_kernel,
        out_shape=jax.ShapeDtypeStruct((n, d), F32),
        grid=(n // tm,),
        in_specs=[_smem_words(tm * TOP_K), _smem_words(tm * TOP_K), _smem_words(TAB_ROWS * LANES),
                  pl.BlockSpec((tm, d), lambda i: (i, 0)),
                  _mod_spec(gt, tm, steps_per_seq), pl.BlockSpec(memory_space=pl.ANY)],
        out_specs=pl.BlockSpec((tm, d), lambda i: (i, 0)),
        scratch_shapes=[pltpu.VMEM((tm * TOP_K * TOKEN_ROWS, LANES), F32),
                        pltpu.VMEM((tm * TOKEN_ROWS, LANES), F32), pltpu.SemaphoreType.DMA],
        compiler_params=_cparams("arbitrary"),
        name="moe_combine",
    )(pos_flat, gate_flat, tab_flat, h, gt, ys)


def _moe(xm_slabs, comb, sel, h, gt, wgu_b, b_gu, wd_b, b_down, tokens_per_seq):
    n, d = h.shape
    tm = min(MOE_TILE, _token_tile(n))
    tme = min(512, _token_tile(n * TOP_K))
    before, base, cnt, total = _moe_rank(sel, tm)
    tile, e, lo, hi, starts = _moe_plan(total[0], n * TOP_K, tme)
    starts_row = jnp.zeros((1, LANES), F32).at[0, :N_EXPERTS].set(starts.astype(F32))
    pos4, gate4, tab = _moe_slots(before, sel, comb, base, cnt, starts_row, tm)
    pos_flat = pos4[:, :TOP_K].reshape(n * TOP_K)
    gate_flat = gate4[:, :TOP_K].reshape(n * TOP_K)
    tab_flat = tab.reshape(-1)
    xs = _moe_dispatch(pos_flat, tab_flat, xm_slabs, tm)
    ys = _moe_ffn((tile, e, lo, hi), xs, wgu_b, b_gu, wd_b, b_down, tme)
    return _moe_combine(pos_flat, gate_flat, tab_flat, h, gt, ys, tm, max(1, tokens_per_seq // tm))


def _token_tile(n):
    for tm in (512, 256, 128, 64, 32, 16, 8):
        if n % tm == 0:
            return tm
    raise ValueError(f"token count {n} must be a multiple of 8")


def kernel(x_prompt, x_sample, cache_a_k, cache_a_v, cache_idx_k, cache_b_k, cache_b_v, page_table,
           c_prompt, c_sample, w_ada, b_ada, g_attn, w_in, g_qa, g_ka, g_ki, g_qb, g_kb,
           lam_q1, lam_k1, lam_q2, lam_k2, g_sub, w_out, g_ffn, w_router, b_router,
           w_gu, b_gu, w_down, b_down):
    bsz, t_p, d = x_prompt.shape
    db, t_s, _ = x_sample.shape
    depth = w_in.shape[0]
    n_pool, page = cache_a_k.shape[1], cache_a_k.shape[2]
    n_pages = page_table.shape[1]
    past = n_pages * page
    assert page == LANES and t_s == 8 and t_p % Q_BLOCK == 0 and d == 1024
    n_p, n_s = bsz * t_p, db * t_s
    pg_scores, pg_attend = math.gcd(n_pages, 32), math.gcd(n_pages, 16)

    cs_p, sn_p = _rope_tables(jnp.arange(t_p, dtype=I32))
    cs_s, sn_s = _rope_tables(past + jnp.arange(t_s, dtype=I32))
    cs_s, sn_s = jnp.tile(cs_s, (db, 1)), jnp.tile(sn_s, (db, 1))
    tile2 = lambda g: jnp.concatenate([g, g]).astype(F32)
    ksel_p = min(TOPK_MAX, t_p // 4)
    ksel_s = min(TOPK_MAX, (past + t_s) // 4)
    tm_p, tm_s = _token_tile(t_p), _token_tile(n_s)
    sps_p = t_p // tm_p
    per_tok = lambda m: jnp.repeat(m, t_s, axis=0)

    hp = x_prompt.reshape(n_p, d)
    hs = x_sample.reshape(n_s, d)
    rows_p = [[] for _ in range(5)]
    rows_s = [[] for _ in range(5)]
    for l in range(depth):
        lam_init = 0.8 - 0.6 * math.exp(-0.3 * l)
        mods = _adaln(jnp.concatenate([c_prompt, c_sample], axis=0), w_ada[l], b_ada[l])
        mp = [m.reshape(bsz, 1, d) for m in jnp.split(mods[:bsz], 6, axis=-1)]
        ms_ = [per_tok(m) for m in jnp.split(mods[bsz:], 6, axis=-1)]
        w_perm = _permute_w_in(w_in[l])
        gains = jnp.zeros((8, LANES), F32).at[:5].set(
            jnp.stack([tile2(g_qa[l]), tile2(g_ka[l]), tile2(g_ki[l]), tile2(g_qb[l]), tile2(g_kb[l])]))
        lamv = jnp.zeros((8, LANES), F32).at[:4, :HD].set(
            jnp.stack([lam_q1[l], lam_k1[l], lam_q2[l], lam_k2[l]]).astype(F32))
        gsub = g_sub[l].reshape(1, VB).astype(F32)
        w_out_b = w_out[l].astype(BF16)
        w_router_p = jnp.zeros((d, LANES), F32).at[:, :N_EXPERTS].set(w_router[l])
        b_router_p = jnp.full((1, LANES), NEG_BIAS, F32).at[0, :N_EXPERTS].set(b_router[l])
        wgu_b = w_gu[l]
        wd_b = w_down[l]

        (qa, qi, qb, ka, kb, ki, wi, va, vb, kab, kbb, kib, vab, vbb) = _proj(
            hp, mp[0], mp[1], g_attn[l], w_perm, cs_p, sn_p, gains, tm_p, sps_p, True)
        r3 = lambda a: a.reshape(bsz, t_p, a.shape[-1])
        oa = _dsa_prompt(r3(qi), r3(wi), r3(kib), r3(qa), r3(kab), r3(vab), ksel_p)
        ob = _diff_prompt(r3(qb), r3(kbb), r3(vbb), lamv, gsub, lam_init)
        h1, xm, comb, sel = _outproj(oa.reshape(n_p, 512), ob.reshape(n_p, 512), hp, mp[2], mp[3], mp[4],
                                     g_ffn[l], w_out_b, w_router_p, b_router_p, tm_p, sps_p)
        hp = _moe(xm, comb, sel, h1, mp[5], wgu_b, b_gu[l], wd_b, b_down[l], t_p)
        for lst, r in zip(rows_p, (ka.reshape(bsz, KVA, HD, t_p).transpose(0, 3, 1, 2),
                                   va.reshape(bsz, KVA, HD, t_p).transpose(0, 3, 1, 2),
                                   ki.transpose(0, 2, 1),
                                   kb.reshape(bsz, KVB, 2, HD, t_p).transpose(0, 4, 1, 2, 3),
                                   vb.reshape(bsz, t_p, KVB, VB))):
            lst.append(r)

        (qa, qi, qb, ka, kb, ki, wi, va, vb, kab, kbb, kib, vab, vbb) = _proj(
            hs, ms_[0], ms_[1], g_attn[l], w_perm, cs_s, sn_s, gains, tm_s, 1, False)
        qm = qi.reshape(db, t_s, H_IDX, D_IDX).transpose(0, 2, 1, 3).reshape(db, 64, D_IDX)
        wcol = jnp.broadcast_to(
            wi[:, :H_IDX].reshape(db, t_s, H_IDX).transpose(0, 2, 1).reshape(db, 64, 1), (db, 64, LANES))
        pad_new = lambda a: jnp.pad(a.reshape(db, t_s, a.shape[-1]), ((0, 0), (0, LANES - t_s), (0, 0)))
        idx_t = jnp.transpose(cache_idx_k[l], (0, 2, 1))
        ak_t = jnp.transpose(cache_a_k[l], (0, 2, 3, 1)).reshape(n_pool, KVA * HD, page)
        av_t = jnp.transpose(cache_a_v[l], (0, 2, 3, 1)).reshape(n_pool, KVA * HD, page)
        bk_t = jnp.transpose(cache_b_k[l], (0, 2, 3, 4, 1)).reshape(n_pool, KVB * 2 * HD, page)
        bv_r = cache_b_v[l].reshape(n_pool, page * KVB, VB)
        sc_past, sc_new = _sample_scores(page_table, qm, wcol, pad_new(kib[:, :D_IDX]), idx_t, pg_scores, t_s)
        sct = jnp.concatenate([sc_past, sc_new], axis=-1).reshape(n_s, n_pages + 1, LANES).transpose(1, 0, 2)
        sel = _sample_select(sct, ksel_s).transpose(1, 0, 2).reshape(db, t_s, (n_pages + 1) * LANES)
        qa4 = qa.reshape(db, t_s, HA, HD).transpose(0, 2, 1, 3)
        zq = jnp.zeros_like(qa4[:, :HA // 2])
        qa_rows = jnp.concatenate([jnp.concatenate([qa4[:, :HA // 2], zq], axis=-1),
                                   jnp.concatenate([zq, qa4[:, HA // 2:]], axis=-1)], axis=1).reshape(db, 64, 128)
        qb5 = qb.reshape(db, t_s, HB, 2, HD).transpose(0, 2, 3, 1, 4)
        zb = jnp.zeros_like(qb5[:, 0, 0])
        blocks = []
        for hh in range(HB):
            for m in range(2):
                pos = (hh // (HB // KVB)) * 2 + m
                blocks.append(jnp.concatenate([qb5[:, hh, m] if p == pos else zb for p in range(4)], axis=-1))
        qb_rows = jnp.stack(blocks, axis=1).reshape(db, 64, 256)
        oa_rows, ob = _sample_attend(
            page_table, qa_rows, qb_rows, sel[:, :, :past], sel[:, :, past:],
            pad_new(kab), pad_new(vab), pad_new(kbb), pad_new(vbb), lamv, gsub,
            ak_t, av_t, bk_t, bv_r, pg_attend, t_s, lam_init)
        oa5 = oa_rows.reshape(db, HA, t_s, KVA, HD)
        oa = jnp.concatenate([oa5[:, :HA // 2, :, 0], oa5[:, HA // 2:, :, 1]], axis=1)
        oa = oa.transpose(0, 2, 1, 3).reshape(n_s, HA * HD)
        h1, xm, comb, sel = _outproj(oa, ob.reshape(n_s, 512), hs, ms_[2], ms_[3], ms_[4],
                                     g_ffn[l], w_out_b, w_router_p, b_router_p, tm_s, 1)
        hs = _moe(xm, comb, sel, h1, ms_[5], wgu_b, b_gu[l], wd_b, b_down[l], t_s)
        for lst, r in zip(rows_s, (ka.reshape(db, t_s, KVA, HD), va.reshape(db, t_s, KVA, HD),
                                   ki.reshape(db, t_s, D_IDX), kb.reshape(db, t_s, KVB, 2, HD),
                                   vb.reshape(db, t_s, KVB, VB))):
            lst.append(r)

    outs_p = [jnp.stack(r) for r in rows_p]
    outs_s = [jnp.stack(r) for r in rows_s]
    return (hp.reshape(bsz, t_p, d), hs.reshape(db, t_s, d), *outs_p, *outs_s)
```
